```python
import jax, jax.numpy as jnp
from jax import lax
import numpy as np

D_MODEL = 2048
BATCH = 2
SEQ = 4096
DEPTH = 1
DEC_BATCH = 128
DEC_SEQ = 1
PAST_LEN = 2048
PAGE_SIZE = 128

HEAD_DIM = 128
N_HEADS_A = 8
N_KV_A = 4
N_HEADS_B = 8
N_KV_B = 4
WIDTH_A = N_HEADS_A * HEAD_DIM
WIDTH_B = N_HEADS_B * HEAD_DIM
MOBA_BLOCK = 256
MOBA_TOPK = 3
MOBA_Q_CHUNK = 64
FOX_Q_CHUNK = 128
POOL_NUM = 5
POOL_DEN = 4
NORM_EPS = 1e-6
NEG_INF = -1e30
FORGET_BIAS = 3.0
IN_SPLITS = (WIDTH_A, N_KV_A * HEAD_DIM, N_KV_A * HEAD_DIM, WIDTH_A,
             WIDTH_B, N_KV_B * HEAD_DIM, N_KV_B * HEAD_DIM, WIDTH_B,
             N_HEADS_B, D_MODEL, D_MODEL)
IN_COLS = sum(IN_SPLITS)

kernel_name = 'moba_fox_gated_hybrid_step'


def rms_norm(x, g):
    xf = x.astype(jnp.float32)
    y = xf * lax.rsqrt(jnp.mean(xf * xf, axis=-1, keepdims=True) + NORM_EPS)
    return (y * g.astype(jnp.float32)).astype(x.dtype)


def project(x, lw):
    g_norm, w_in, b_forget, b_merge, gq_a, gk_a, gq_b, gk_b = lw[:8]
    B, T, _ = x.shape
    h = rms_norm(x, g_norm)
    (qa, ka, va, za, qb, kb, vb, zb, fb, ga, gb) = jnp.split(
        h @ w_in, np.cumsum(IN_SPLITS)[:-1].tolist(), axis=-1)

    def heads(t, n):
        return t.reshape(B, T, n, HEAD_DIM)
    qa = rms_norm(heads(qa, N_HEADS_A), gq_a)
    ka = rms_norm(heads(ka, N_KV_A), gk_a)
    va = heads(va, N_KV_A)
    qb = rms_norm(heads(qb, N_HEADS_B), gq_b)
    kb = rms_norm(heads(kb, N_KV_B), gk_b)
    vb = heads(vb, N_KV_B)
    logf = jax.nn.log_sigmoid((fb + b_forget).astype(jnp.float32))
    gate_a = jax.nn.sigmoid(ga + b_merge[0])
    gate_b = jax.nn.sigmoid(gb + b_merge[1])
    return qa, ka, va, za, qb, kb, vb, zb, logf, gate_a, gate_b


def merge(x, oa, ob, za, zb, gate_a, gate_b, lw):
    w_br_a, w_br_b, w_out = lw[8:]
    B, T, _ = x.shape
    ya = (oa.reshape(B, T, WIDTH_A) * jax.nn.silu(za)) @ w_br_a
    yb = (ob.reshape(B, T, WIDTH_B) * jax.nn.silu(zb)) @ w_br_b
    return x + (gate_a * ya + gate_b * yb) @ w_out


def moba_blocks(k, v):
    B, L = k.shape[:2]
    nb = max(-(-L // MOBA_BLOCK), MOBA_TOPK)
    pad = ((0, 0), (0, nb * MOBA_BLOCK - L), (0, 0), (0, 0))

    def blocks(t):
        return jnp.pad(t, pad).reshape(B, nb, MOBA_BLOCK, N_KV_A, HEAD_DIM).transpose(0, 3, 1, 2, 4)
    kblk, vblk = blocks(k), blocks(v)
    kmean = jnp.mean(kblk, axis=3, dtype=jnp.float32)
    return kblk, vblk, kmean


def moba_query(q, q_pos, kblk, vblk, kmean):
    B, Tq = q.shape[:2]
    nb = kblk.shape[2]
    hk = jnp.arange(N_HEADS_A) // (N_HEADS_A // N_KV_A)
    own = q_pos // MOBA_BLOCK
    gate = jnp.einsum('bqhd,bhnd->bqhn', q.astype(jnp.float32), kmean[:, hk])
    past = jnp.arange(nb)[None, :] < own[:, None]
    gate = jnp.where(past[None, :, None, :], gate, NEG_INF)
    _, sel = lax.top_k(gate, MOBA_TOPK)
    sel_ok = jnp.arange(MOBA_TOPK)[None, :] < own[:, None]
    blk = jnp.concatenate(
        [sel, jnp.broadcast_to(own[None, :, None, None], (B, Tq, N_HEADS_A, 1)).astype(sel.dtype)], axis=-1)
    ok = jnp.concatenate([sel_ok, jnp.ones((Tq, 1), bool)], axis=-1)
    bi = jnp.arange(B)[:, None, None, None]
    hi = hk[None, None, :, None]
    kg = kblk[bi, hi, blk]
    vg = vblk[bi, hi, blk]
    kpos = blk[..., None] * MOBA_BLOCK + jnp.arange(MOBA_BLOCK)
    dist = (q_pos[None, :, None, None, None] - kpos).astype(jnp.float32)
    slopes = 2.0 ** (-8.0 * jnp.arange(1, N_HEADS_A + 1, dtype=jnp.float32) / N_HEADS_A)
    s = jnp.einsum('bqhd,bqhjkd->bqhjk', q, kg, preferred_element_type=jnp.float32) * (HEAD_DIM ** -0.5)
    s = s - slopes[None, None, :, None, None] * dist
    mask = ok[None, :, None, :, None] & (dist >= 0)
    s = jnp.where(mask, s, NEG_INF)
    p = jax.nn.softmax(s.reshape(B, Tq, N_HEADS_A, -1), axis=-1).reshape(s.shape)
    o = jnp.einsum('bqhjk,bqhjkd->bqhd', p.astype(vg.dtype), vg, preferred_element_type=jnp.float32)
    return o.astype(q.dtype)


def fox_attend(q, c_q, q_pos, k, v, c_k):
    B, Tq = q.shape[:2]
    L = k.shape[1]
    G = N_HEADS_B // N_KV_B
    qg = q.reshape(B, Tq, N_KV_B, G, HEAD_DIM)
    s = jnp.einsum('bqkgd,bskd->bkgqs', qg, k, preferred_element_type=jnp.float32) * (HEAD_DIM ** -0.5)
    cq = c_q.reshape(B, Tq, N_KV_B, G).transpose(0, 2, 3, 1)
    ck = c_k.reshape(B, L, N_KV_B, G).transpose(0, 2, 3, 1)
    s = s + cq[..., :, None] - ck[..., None, :]
    causal = jnp.arange(L)[None, :] <= q_pos[:, None]
    s = jnp.where(causal, s, NEG_INF)
    p = jax.nn.softmax(s, axis=-1)
    o = jnp.einsum('bkgqs,bskd->bqkgd', p.astype(v.dtype), v, preferred_element_type=jnp.float32)
    return o.reshape(B, Tq, N_HEADS_B, HEAD_DIM).astype(q.dtype)


def prompt_layer(x, lw):
    B, T, _ = x.shape
    qa, ka, va, za, qb, kb, vb, zb, logf, gate_a, gate_b = project(x, lw)
    pos = jnp.arange(T)
    kblk, vblk, kmean = moba_blocks(ka, va)
    na = T // MOBA_Q_CHUNK

    def moba_chunk(args):
        qc, pc = args
        return moba_query(qc, pc, kblk, vblk, kmean)
    oa = lax.map(moba_chunk, (qa.reshape(B, na, MOBA_Q_CHUNK, N_HEADS_A, HEAD_DIM).swapaxes(0, 1),
                              pos.reshape(na, MOBA_Q_CHUNK)))
    oa = oa.swapaxes(0, 1).reshape(B, T, N_HEADS_A, HEAD_DIM)
    c = jnp.cumsum(logf, axis=1)
    nf = T // FOX_Q_CHUNK

    def fox_chunk(args):
        qc, cc, pc = args
        return fox_attend(qc, cc, pc, kb, vb, c)
    ob = lax.map(fox_chunk, (qb.reshape(B, nf, FOX_Q_CHUNK, N_HEADS_B, HEAD_DIM).swapaxes(0, 1),
                             c.reshape(B, nf, FOX_Q_CHUNK, N_HEADS_B).swapaxes(0, 1),
                             pos.reshape(nf, FOX_Q_CHUNK)))
    ob = ob.swapaxes(0, 1).reshape(B, T, N_HEADS_B, HEAD_DIM)
    y = merge(x, oa, ob, za, zb, gate_a, gate_b, lw)
    return y, (ka, va, kb, vb, logf)


def gather_pages(pool, page_table):
    g = pool[page_table]
    return g.reshape((g.shape[0], g.shape[1] * g.shape[2]) + pool.shape[2:])


def sample_layer(x, ck_a, cv_a, ck_b, cv_b, clogf, page_table, lw):
    B, T, _ = x.shape
    qa, ka, va, za, qb, kb, vb, zb, logf, gate_a, gate_b = project(x, lw)
    P = page_table.shape[1] * ck_a.shape[1]
    q_pos = P + jnp.arange(T)
    ka_all = jnp.concatenate([gather_pages(ck_a, page_table), ka], axis=1)
    va_all = jnp.concatenate([gather_pages(cv_a, page_table), va], axis=1)
    kblk, vblk, kmean = moba_blocks(ka_all, va_all)
    oa = moba_query(qa, q_pos, kblk, vblk, kmean)
    kb_all = jnp.concatenate([gather_pages(ck_b, page_table), kb], axis=1)
    vb_all = jnp.concatenate([gather_pages(cv_b, page_table), vb], axis=1)
    logf_all = jnp.concatenate([gather_pages(clogf, page_table).astype(jnp.float32), logf], axis=1)
    c = jnp.cumsum(logf_all, axis=1)
    ob = fox_attend(qb, c[:, P:], q_pos, kb_all, vb_all, c)
    y = merge(x, oa, ob, za, zb, gate_a, gate_b, lw)
    return y, (ka, va, kb, vb, logf)


def setup_inputs(seed: int = 0) -> dict:
    key = jax.random.key(seed)
    ks = jax.random.split(key, 24)
    f32 = jnp.float32
    n_pages = PAST_LEN // PAGE_SIZE
    n_pool = (DEC_BATCH * n_pages * POOL_NUM) // POOL_DEN

    def nrm(k, shape, s=1.0):
        return s * jax.random.normal(k, shape, f32)
    perm = jax.random.permutation(ks[7], n_pool)[:DEC_BATCH * n_pages]
    page_table = perm.reshape(DEC_BATCH, n_pages).astype(jnp.int32)
    return {
        'x_prompt': nrm(ks[0], (BATCH, SEQ, D_MODEL)),
        'x_sample': nrm(ks[1], (DEC_BATCH, DEC_SEQ, D_MODEL)),
        'cache_k_moba': nrm(ks[2], (DEPTH, n_pool, PAGE_SIZE, N_KV_A, HEAD_DIM)),
        'cache_v_moba': nrm(ks[3], (DEPTH, n_pool, PAGE_SIZE, N_KV_A, HEAD_DIM)),
        'cache_k_fox': nrm(ks[4], (DEPTH, n_pool, PAGE_SIZE, N_KV_B, HEAD_DIM)),
        'cache_v_fox': nrm(ks[5], (DEPTH, n_pool, PAGE_SIZE, N_KV_B, HEAD_DIM)),
        'cache_logf_fox': jax.nn.log_sigmoid(FORGET_BIAS + nrm(ks[6], (DEPTH, n_pool, PAGE_SIZE, N_HEADS_B), 0.5)),
        'page_table': page_table,
        'g_norm': 1.0 + nrm(ks[8], (DEPTH, D_MODEL), 0.01),
        'w_in': nrm(ks[9], (DEPTH, D_MODEL, IN_COLS), D_MODEL ** -0.5),
        'b_forget': FORGET_BIAS + nrm(ks[10], (DEPTH, N_HEADS_B), 0.5),
        'b_merge': nrm(ks[11], (DEPTH, 2, D_MODEL), 0.01),
        'gq_moba': 1.0 + nrm(ks[12], (DEPTH, HEAD_DIM), 0.01),
        'gk_moba': 1.0 + nrm(ks[13], (DEPTH, HEAD_DIM), 0.01),
        'gq_fox': 1.0 + nrm(ks[14], (DEPTH, HEAD_DIM), 0.01),
        'gk_fox': 1.0 + nrm(ks[15], (DEPTH, HEAD_DIM), 0.01),
        'w_branch_moba': nrm(ks[16], (DEPTH, WIDTH_A, D_MODEL), WIDTH_A ** -0.5),
        'w_branch_fox': nrm(ks[17], (DEPTH, WIDTH_B, D_MODEL), WIDTH_B ** -0.5),
        'w_out': nrm(ks[18], (DEPTH, D_MODEL, D_MODEL), D_MODEL ** -0.5),
    }


def reference(x_prompt, x_sample, cache_k_moba, cache_v_moba, cache_k_fox, cache_v_fox,
              cache_logf_fox, page_table, g_norm, w_in, b_forget, b_merge, gq_moba, gk_moba,
              gq_fox, gk_fox, w_branch_moba, w_branch_fox, w_out):
    rows_p = []
    rows_s = []
    for l in range(DEPTH):
        lw = (g_norm[l], w_in[l], b_forget[l], b_merge[l], gq_moba[l], gk_moba[l],
              gq_fox[l], gk_fox[l], w_branch_moba[l], w_branch_fox[l], w_out[l])
        x_prompt, rp = prompt_layer(x_prompt, lw)
        x_sample, rs = sample_layer(x_sample, cache_k_moba[l], cache_v_moba[l], cache_k_fox[l],
                                    cache_v_fox[l], cache_logf_fox[l], page_table, lw)
        rows_p.append(rp)
        rows_s.append(rs)
    k_moba_p = jnp.stack([r[0] for r in rows_p])
    v_moba_p = jnp.stack([r[1] for r in rows_p])
    k_fox_p = jnp.stack([r[2] for r in rows_p])
    v_fox_p = jnp.stack([r[3] for r in rows_p])
    logf_fox_p = jnp.stack([r[4] for r in rows_p])
    k_moba_s = jnp.stack([r[0] for r in rows_s])
    v_moba_s = jnp.stack([r[1] for r in rows_s])
    k_fox_s = jnp.stack([r[2] for r in rows_s])
    v_fox_s = jnp.stack([r[3] for r in rows_s])
    logf_fox_s = jnp.stack([r[4] for r in rows_s])
    return (x_prompt, x_sample, k_moba_p, v_moba_p, k_fox_p, v_fox_p, logf_fox_p,
            k_moba_s, v_moba_s, k_fox_s, v_fox_s, logf_fox_s)
```

```python
import functools

import jax
import jax.numpy as jnp
from jax import lax
from jax.experimental import pallas as pl
from jax.experimental.pallas import tpu as pltpu

HEAD_DIM = 128
MOBA_BLOCK = 256
MOBA_TOPK = 3
NORM_EPS = 1e-6
NEG_INF = -1e30
SCALE = HEAD_DIM ** -0.5
LANES = 128
VMEM_LIMIT = 52 * 1024 * 1024

F32 = jnp.float32
BF16 = jnp.bfloat16
HI = lax.Precision.HIGHEST
NT_DIMS = (((1,), (1,)), ((), ()))


def _params(*sem):
    return pltpu.CompilerParams(dimension_semantics=sem, vmem_limit_bytes=VMEM_LIMIT)


def _row_tile(m, want):
    t = min(m, want)
    assert m % t == 0
    return t


def _rmsnorm_kernel(x_ref, g_ref, h_ref):
    x = x_ref[...]
    ms = jnp.mean(x * x, axis=-1, keepdims=True)
    h_ref[...] = (x * lax.rsqrt(ms + NORM_EPS) * g_ref[...]).astype(h_ref.dtype)


def _rmsnorm(x, g):
    m, d = x.shape
    tm = _row_tile(m, 512)
    return pl.pallas_call(
        _rmsnorm_kernel,
        grid=(m // tm,),
        in_specs=[pl.BlockSpec((tm, d), lambda i: (i, 0)), pl.BlockSpec((1, d), lambda i: (0, 0))],
        out_specs=pl.BlockSpec((tm, d), lambda i: (i, 0)),
        out_shape=jax.ShapeDtypeStruct((m, d), BF16),
        compiler_params=_params("parallel"),
        name="rmsnorm",
    )(x, g.reshape(1, d))


def _head_norm(a, g):
    ms = jnp.mean(a * a, axis=-1, keepdims=True)
    return a * lax.rsqrt(ms + NORM_EPS) * g


def _log_sigmoid(x):
    return jnp.minimum(x, 0.0) - jnp.log1p(jnp.exp(-jnp.abs(x)))


def _proj_q_kernel(h_ref, w_ref, g_ref, o_ref):
    acc = jnp.dot(h_ref[...], w_ref[...], preferred_element_type=F32)
    for j in range(acc.shape[1] // HEAD_DIM):
        sl = slice(j * HEAD_DIM, (j + 1) * HEAD_DIM)
        o_ref[:, sl] = _head_norm(acc[:, sl], g_ref[:, sl]).astype(o_ref.dtype)


def _proj_kv_kernel(h_ref, w_ref, g_ref, *o_refs, kvw, n_forget):
    k32_ref, v32_ref, k16_ref, v16_ref = o_refs[:4]
    acc = jnp.dot(h_ref[...], w_ref[...], preferred_element_type=F32)
    for j in range(kvw // HEAD_DIM):
        sl = slice(j * HEAD_DIM, (j + 1) * HEAD_DIM)
        kn = _head_norm(acc[:, sl], g_ref[:, sl])
        k32_ref[:, sl] = kn
        k16_ref[:, sl] = kn.astype(BF16)
    v = acc[:, kvw:2 * kvw]
    v32_ref[...] = v
    v16_ref[...] = v.astype(BF16)
    if n_forget:
        sl = slice(2 * kvw, 2 * kvw + n_forget)
        o_refs[4][...] = _log_sigmoid(acc[:, sl] + g_ref[:, sl])


def _proj_plain_kernel(h_ref, w_ref, o_ref):
    o_ref[...] = jnp.dot(h_ref[...], w_ref[...], preferred_element_type=F32).astype(o_ref.dtype)


def _proj_gate_kernel(h_ref, w_ref, b_ref, o_ref):
    acc = jnp.dot(h_ref[...], w_ref[...], preferred_element_type=F32)
    o_ref[...] = jax.nn.sigmoid(acc + b_ref[...]).astype(o_ref.dtype)


def _proj(body, h, w, params, out_dtypes_widths, tn=None):
    m, d = h.shape
    n = w.shape[1]
    tn = n if tn is None else tn
    tm = _row_tile(m, 512)
    in_specs = [pl.BlockSpec((tm, d), lambda j, i: (i, 0)), pl.BlockSpec((d, tn), lambda j, i: (0, j))]
    args = [h, w]
    if params is not None:
        in_specs.append(pl.BlockSpec((1, tn), lambda j, i: (0, j)))
        args.append(params)
    single = n // tn == 1
    out_specs, out_shape = [], []
    for dt, width in out_dtypes_widths:
        bw = width if single else tn
        out_specs.append(pl.BlockSpec((tm, bw), lambda j, i: (i, j)))
        out_shape.append(jax.ShapeDtypeStruct((m, width), dt))
    return pl.pallas_call(
        body,
        grid=(n // tn, m // tm),
        in_specs=in_specs,
        out_specs=out_specs,
        out_shape=out_shape,
        compiler_params=_params("parallel", "parallel"),
        name="proj",
    )(*args)


def _kmean_kernel(k_ref, o_ref, *, nblk):
    o_ref[...] = jnp.zeros_like(o_ref)
    for n in range(nblk):
        blk = k_ref[n * MOBA_BLOCK:(n + 1) * MOBA_BLOCK, :]
        o_ref[0, n:n + 1, :] = jnp.sum(blk, axis=0, keepdims=True) * (1.0 / MOBA_BLOCK)


def _kmean(k32, batch, seq):
    w = k32.shape[1]
    nblk = seq // MOBA_BLOCK
    assert nblk <= LANES
    return pl.pallas_call(
        functools.partial(_kmean_kernel, nblk=nblk),
        grid=(batch,),
        in_specs=[pl.BlockSpec((seq, w), lambda b: (b, 0))],
        out_specs=pl.BlockSpec((1, LANES, w), lambda b: (b, 0, 0)),
        out_shape=jax.ShapeDtypeStruct((batch, LANES, w), F32),
        compiler_params=_params("parallel"),
        name="kmean",
    )(k32)


def _online_softmax_step(h, s, v, m_sc, l_sc, acc_sc):
    m_prev = m_sc[h]
    m_new = jnp.maximum(m_prev, jnp.max(s, axis=-1, keepdims=True))
    alpha = jnp.exp(m_prev - m_new)
    p = jnp.exp(s - m_new)
    l_sc[h] = alpha * l_sc[h] + jnp.sum(p, axis=-1, keepdims=True)
    acc_sc[h] = alpha * acc_sc[h] + jnp.dot(p.astype(v.dtype), v, preferred_element_type=F32)
    m_sc[h] = m_new


def _init_softmax_state(m_sc, l_sc, acc_sc):
    m_sc[...] = jnp.full(m_sc.shape, -jnp.inf, F32)
    l_sc[...] = jnp.zeros(l_sc.shape, F32)
    acc_sc[...] = jnp.zeros(acc_sc.shape, F32)


def _moba_prompt_kernel(slopes_ref, q_ref, k_ref, v_ref, kmean_ref, o_ref, m_sc, l_sc, acc_sc, selb_sc,
                        *, group, nblk):
    tq = q_ref.shape[0]
    assert tq == MOBA_BLOCK
    kvh = pl.program_id(1)
    qi = pl.program_id(2)
    _init_softmax_state(m_sc, l_sc, acc_sc)

    lane = lax.broadcasted_iota(jnp.int32, (tq, LANES), 1)
    km = kmean_ref[0]
    for h in range(group):
        qh = q_ref[:, h * HEAD_DIM:(h + 1) * HEAD_DIM].astype(F32)
        g = lax.dot_general(qh, km, NT_DIMS, precision=HI, preferred_element_type=F32)
        cnt = jnp.zeros((tq, LANES), jnp.int32)
        for m in range(nblk):
            gm = g[:, m:m + 1]
            beats = (gm > g) | ((gm == g) & (lane > m))
            cnt = cnt + jnp.where(beats, (m < qi).astype(jnp.int32), 0)
        sel = (lane < qi) & (cnt < MOBA_TOPK)
        selb_sc[h] = jnp.where(sel, 0.0, NEG_INF)

    row = lax.broadcasted_iota(jnp.int32, (tq, MOBA_BLOCK), 0)
    col = lax.broadcasted_iota(jnp.int32, (tq, MOBA_BLOCK), 1)
    rel = (row - col).astype(F32)

    def scores(h, ki, k):
        qh = q_ref[:, h * HEAD_DIM:(h + 1) * HEAD_DIM]
        s = lax.dot_general(qh, k, NT_DIMS, preferred_element_type=F32) * SCALE
        slope = slopes_ref[kvh * group + h]
        dist = rel + ((qi - ki) * MOBA_BLOCK).astype(F32)
        return s - slope * dist

    def past_block(ki, carry):
        off = pl.multiple_of(ki * MOBA_BLOCK, MOBA_BLOCK)
        k = k_ref[pl.ds(off, MOBA_BLOCK), :]
        v = v_ref[pl.ds(off, MOBA_BLOCK), :]
        for h in range(group):
            selcol = jnp.sum(jnp.where(lane == ki, selb_sc[h], 0.0), axis=-1, keepdims=True)
            _online_softmax_step(h, scores(h, ki, k) + selcol, v, m_sc, l_sc, acc_sc)
        return carry

    lax.fori_loop(0, qi, past_block, 0)

    off = pl.multiple_of(qi * MOBA_BLOCK, MOBA_BLOCK)
    k = k_ref[pl.ds(off, MOBA_BLOCK), :]
    v = v_ref[pl.ds(off, MOBA_BLOCK), :]
    for h in range(group):
        s = jnp.where(row >= col, scores(h, qi, k), NEG_INF)
        _online_softmax_step(h, s, v, m_sc, l_sc, acc_sc)
        o_ref[:, h * HEAD_DIM:(h + 1) * HEAD_DIM] = (acc_sc[h] / l_sc[h]).astype(o_ref.dtype)


def _moba_prompt(q16, q_col0, k16, v16, kmean, slopes, batch, seq, n_heads, n_kv):
    group = n_heads // n_kv
    tq = MOBA_BLOCK
    nq = seq // tq
    gw = group * HEAD_DIM
    qb0 = q_col0 // gw
    return pl.pallas_call(
        functools.partial(_moba_prompt_kernel, group=group, nblk=seq // MOBA_BLOCK),
        grid=(batch, n_kv, nq),
        in_specs=[
            pl.BlockSpec(memory_space=pltpu.SMEM),
            pl.BlockSpec((tq, gw), lambda b, j, i: (b * nq + i, qb0 + j)),
            pl.BlockSpec((seq, HEAD_DIM), lambda b, j, i: (b, j)),
            pl.BlockSpec((seq, HEAD_DIM), lambda b, j, i: (b, j)),
            pl.BlockSpec((1, LANES, HEAD_DIM), lambda b, j, i: (b, 0, j)),
        ],
        out_specs=pl.BlockSpec((tq, gw), lambda b, j, i: (b * nq + i, j)),
        out_shape=jax.ShapeDtypeStruct((batch * seq, n_heads * HEAD_DIM), F32),
        scratch_shapes=[
            pltpu.VMEM((group, tq, 1), F32),
            pltpu.VMEM((group, tq, 1), F32),
            pltpu.VMEM((group, tq, HEAD_DIM), F32),
            pltpu.VMEM((group, tq, LANES), F32),
        ],
        compiler_params=_params("parallel", "parallel", "arbitrary"),
        name="moba_prompt",
    )(slopes, q16, k16, v16, kmean)


def _fox_prompt_kernel(q_ref, k_ref, v_ref, cq_ref, ck_ref, o_ref, m_sc, l_sc, acc_sc, *, group, tk):
    tq = q_ref.shape[0]
    assert tq == tk
    qi = pl.program_id(2)
    _init_softmax_state(m_sc, l_sc, acc_sc)

    def scores(h, off, k):
        qh = q_ref[:, h * HEAD_DIM:(h + 1) * HEAD_DIM]
        s = lax.dot_general(qh, k, NT_DIMS, preferred_element_type=F32) * SCALE
        return s + cq_ref[0, :, h:h + 1] - ck_ref[0, 0, h:h + 1, pl.ds(off, tk)]

    def past_block(ki, carry):
        off = pl.multiple_of(ki * tk, tk)
        k = k_ref[pl.ds(off, tk), :]
        v = v_ref[pl.ds(off, tk), :]
        for h in range(group):
            _online_softmax_step(h, scores(h, off, k), v, m_sc, l_sc, acc_sc)
        return carry

    lax.fori_loop(0, qi, past_block, 0)

    row = lax.broadcasted_iota(jnp.int32, (tq, tk), 0)
    col = lax.broadcasted_iota(jnp.int32, (tq, tk), 1)
    off = pl.multiple_of(qi * tk, tk)
    k = k_ref[pl.ds(off, tk), :]
    v = v_ref[pl.ds(off, tk), :]
    for h in range(group):
        s = jnp.where(row >= col, scores(h, off, k), NEG_INF)
        _online_softmax_step(h, s, v, m_sc, l_sc, acc_sc)
        o_ref[:, h * HEAD_DIM:(h + 1) * HEAD_DIM] = (acc_sc[h] / l_sc[h]).astype(o_ref.dtype)


def _fox_prompt(q16, q_col0, k16, v16, c_col, c_row, batch, seq, n_heads, n_kv):
    group = n_heads // n_kv
    tq = 256
    nq = seq // tq
    gw = group * HEAD_DIM
    qb0 = q_col0 // gw
    return pl.pallas_call(
        functools.partial(_fox_prompt_kernel, group=group, tk=tq),
        grid=(batch, n_kv, nq),
        in_specs=[
            pl.BlockSpec((tq, gw), lambda b, j, i: (b * nq + i, qb0 + j)),
            pl.BlockSpec((seq, HEAD_DIM), lambda b, j, i: (b, j)),
            pl.BlockSpec((seq, HEAD_DIM), lambda b, j, i: (b, j)),
            pl.BlockSpec((1, tq, group), lambda b, j, i: (j, b * nq + i, 0)),
            pl.BlockSpec((1, 1, group, seq), lambda b, j, i: (b, j, 0, 0)),
        ],
        out_specs=pl.BlockSpec((tq, gw), lambda b, j, i: (b * nq + i, j)),
        out_shape=jax.ShapeDtypeStruct((batch * seq, n_heads * HEAD_DIM), F32),
        scratch_shapes=[
            pltpu.VMEM((group, tq, 1), F32),
            pltpu.VMEM((group, tq, 1), F32),
            pltpu.VMEM((group, tq, HEAD_DIM), F32),
        ],
        compiler_params=_params("parallel", "parallel", "arbitrary"),
        name="fox_prompt",
    )(q16, k16, v16, c_col, c_row)


def _cumsum_kernel(x_ref, o_ref, *, chunk):
    rows, seq = x_ref.shape[1:]
    r = lax.broadcasted_iota(jnp.int32, (chunk, chunk), 0)
    c = lax.broadcasted_iota(jnp.int32, (chunk, chunk), 1)
    upper = (r <= c).astype(F32)
    carry = jnp.zeros((rows, 1), F32)
    for i in range(seq // chunk):
        sl = slice(i * chunk, (i + 1) * chunk)
        cs = jnp.dot(x_ref[0, :, sl], upper, precision=HI, preferred_element_type=F32) + carry
        o_ref[0, :, sl] = cs
        carry = cs[:, chunk - 1:chunk]


def _cumsum_rows(x):
    batch, rows, seq = x.shape
    chunk = 256
    assert seq % chunk == 0
    return pl.pallas_call(
        functools.partial(_cumsum_kernel, chunk=chunk),
        grid=(batch,),
        in_specs=[pl.BlockSpec((1, rows, seq), lambda b: (b, 0, 0))],
        out_specs=pl.BlockSpec((1, rows, seq), lambda b: (b, 0, 0)),
        out_shape=jax.ShapeDtypeStruct(x.shape, F32),
        compiler_params=_params("parallel"),
        name="cumsum",
    )(x)


def _page_copies(pt_ref, row, slot, srcs, bufs, sems, n_pages, page):
    copies = []
    for j in range(n_pages):
        idx = pt_ref[row * n_pages + j]
        for a, (src, buf) in enumerate(zip(srcs, bufs)):
            copies.append(pltpu.make_async_copy(src.at[idx], buf.at[slot, pl.ds(j * page, page)], sems.at[a, slot]))
    return copies


def _fetch_rows(pt_ref, srcs, bufs, sems, n_pages, page):
    b = pl.program_id(0)
    nb = pl.num_programs(0)
    slot = lax.rem(b, 2)

    @pl.when(b == 0)
    def _():
        for c in _page_copies(pt_ref, b, slot, srcs, bufs, sems, n_pages, page):
            c.start()

    @pl.when(b + 1 < nb)
    def _():
        for c in _page_copies(pt_ref, b + 1, 1 - slot, srcs, bufs, sems, n_pages, page):
            c.start()

    for c in _page_copies(pt_ref, b, slot, srcs, bufs, sems, n_pages, page):
        c.wait()
    return slot


def _block_diag_queries(q, n_kv):
    heads = q.shape[0]
    group = heads // n_kv
    qt = jnp.concatenate([q] * n_kv, axis=1)
    r = lax.broadcasted_iota(jnp.int32, qt.shape, 0)
    c = lax.broadcasted_iota(jnp.int32, qt.shape, 1)
    return jnp.where(c // HEAD_DIM == r // group, qt, jnp.zeros_like(qt))


def _decode_softmax_pv(s, s_self, v_all, v_new, n_kv):
    heads = s.shape[1]
    group = heads // n_kv
    m = jnp.maximum(jnp.max(s, axis=0, keepdims=True), s_self)
    p = jnp.exp(s - m)
    p_self = jnp.exp(s_self - m)
    l = jnp.sum(p, axis=0, keepdims=True) + p_self
    rows = []
    for h in range(heads):
        sl = slice((h // group) * HEAD_DIM, (h // group + 1) * HEAD_DIM)
        acc = jnp.sum(p[:, h:h + 1] * v_all[:, sl], axis=0, keepdims=True) + p_self[:, h:h + 1] * v_new[:, sl]
        rows.append(acc / l[:, h:h + 1])
    return jnp.concatenate(rows, axis=0)


def _moba_decode_kernel(pt_ref, q_ref, knew_ref, vnew_ref, slopes_ref, k_hbm, v_hbm, o_ref, kbuf, vbuf, sems,
                        *, n_kv, n_pages, page):
    slot = _fetch_rows(pt_ref, (k_hbm, v_hbm), (kbuf, vbuf), sems, n_pages, page)
    past = n_pages * page
    nblk = past // MOBA_BLOCK
    k_all = kbuf[slot]
    q = q_ref[0]
    heads = q.shape[0]
    w = _block_diag_queries(q, n_kv)

    means = jnp.concatenate(
        [jnp.sum(k_all[n * MOBA_BLOCK:(n + 1) * MOBA_BLOCK, :], axis=0, keepdims=True) for n in range(nblk)],
        axis=0) * (1.0 / MOBA_BLOCK)
    g = lax.dot_general(means, w.astype(F32), NT_DIMS, precision=HI, preferred_element_type=F32)
    blk = lax.broadcasted_iota(jnp.int32, g.shape, 0)
    cnt = jnp.zeros(g.shape, jnp.int32)
    for m in range(nblk):
        gm = g[m:m + 1, :]
        cnt = cnt + ((gm > g) | ((gm == g) & (blk > m))).astype(jnp.int32)
    sel = cnt < MOBA_TOPK

    s = lax.dot_general(k_all.astype(BF16), w, NT_DIMS, preferred_element_type=F32) * SCALE
    dist = (past - lax.broadcasted_iota(jnp.int32, (past, 1), 0)).astype(F32)
    s = s - slopes_ref[...] * dist
    s = jnp.concatenate(
        [jnp.where(sel[n:n + 1, :], s[n * MOBA_BLOCK:(n + 1) * MOBA_BLOCK, :], NEG_INF) for n in range(nblk)], axis=0)
    s_self = lax.dot_general(knew_ref[0].astype(BF16), w, NT_DIMS, preferred_element_type=F32) * SCALE
    o_ref[0] = _decode_softmax_pv(s, s_self, vbuf[slot], vnew_ref[0], n_kv).astype(o_ref.dtype)


def _fox_decode_kernel(pt_ref, q_ref, knew_ref, vnew_ref, fnew_ref, k_hbm, v_hbm, f_hbm, o_ref, kbuf, vbuf, fbuf, sems,
                       *, n_kv, n_pages, page):
    slot = _fetch_rows(pt_ref, (k_hbm, v_hbm, f_hbm), (kbuf, vbuf, fbuf), sems, n_pages, page)
    q = q_ref[0]
    w = _block_diag_queries(q, n_kv)

    r = lax.broadcasted_iota(jnp.int32, (page, page), 0)
    c = lax.broadcasted_iota(jnp.int32, (page, page), 1)
    lower = (r >= c).astype(F32)
    carry = jnp.zeros((1, q.shape[0]), F32)
    cums = []
    for j in range(n_pages):
        cs = jnp.dot(lower, fbuf[slot, j * page:(j + 1) * page, :], precision=HI, preferred_element_type=F32) + carry
        cums.append(cs)
        carry = cs[page - 1:page, :]
    c_past = jnp.concatenate(cums, axis=0)
    c_now = carry + fnew_ref[0]

    s = lax.dot_general(kbuf[slot].astype(BF16), w, NT_DIMS, preferred_element_type=F32) * SCALE
    s = s + c_now - c_past
    s_self = lax.dot_general(knew_ref[0].astype(BF16), w, NT_DIMS, preferred_element_type=F32) * SCALE
    o_ref[0] = _decode_softmax_pv(s, s_self, vbuf[slot], vnew_ref[0], n_kv).astype(o_ref.dtype)


def _decode_attention(body, page_table, q16, k_new, v_new, extra, extra_per_row, caches, n_kv):
    rows, n_pages = page_table.shape
    heads = q16.shape[1] // HEAD_DIM
    kvw = n_kv * HEAD_DIM
    page = caches[0].shape[1]
    past = n_pages * page
    assert past % MOBA_BLOCK == 0
    row_spec = lambda shape: pl.BlockSpec((1,) + shape, lambda b, pt: (b, 0, 0))
    in_specs = [row_spec((heads, HEAD_DIM)), row_spec((1, kvw)), row_spec((1, kvw))]
    if extra_per_row:
        in_specs.append(row_spec((1, heads)))
        extra = extra.reshape(rows, 1, heads)
    else:
        in_specs.append(pl.BlockSpec((1, heads), lambda b, pt: (0, 0)))
    in_specs += [pl.BlockSpec(memory_space=pl.ANY)] * len(caches)
    scratch = [pltpu.VMEM((2, past, c.shape[2]), F32) for c in caches]
    scratch.append(pltpu.SemaphoreType.DMA((len(caches), 2)))
    return pl.pallas_call(
        functools.partial(body, n_kv=n_kv, n_pages=n_pages, page=page),
        grid_spec=pltpu.PrefetchScalarGridSpec(
            num_scalar_prefetch=1,
            grid=(rows,),
            in_specs=in_specs,
            out_specs=pl.BlockSpec((1, heads, HEAD_DIM), lambda b, pt: (b, 0, 0)),
            scratch_shapes=scratch,
        ),
        out_shape=jax.ShapeDtypeStruct((rows, heads, HEAD_DIM), F32),
        compiler_params=_params("arbitrary"),
        name="decode_attention",
    )(page_table.reshape(-1), q16.reshape(rows, heads, HEAD_DIM), k_new.reshape(rows, 1, kvw),
      v_new.reshape(rows, 1, kvw), extra, *caches).reshape(rows, heads * HEAD_DIM)


def _merge_kernel(x_ref, oa_ref, ob_ref, z_ref, g_ref, wa_ref, wb_ref, wo_ref, y_ref):
    wa_w = oa_ref.shape[1]
    d = x_ref.shape[1]

    def branch(o_ref, z, w_ref):
        u = o_ref[...] * (z * jax.nn.sigmoid(z))
        return jnp.dot(u.astype(BF16), w_ref[...], preferred_element_type=F32)

    ya = branch(oa_ref, z_ref[:, :wa_w], wa_ref)
    yb = branch(ob_ref, z_ref[:, wa_w:], wb_ref)
    mixed = g_ref[:, :d] * ya + g_ref[:, d:] * yb
    y_ref[...] = x_ref[...] + jnp.dot(mixed.astype(BF16), wo_ref[...], preferred_element_type=F32)


def _merge(x, oa, ob, z, gates, wa, wb, wo):
    m, d = x.shape
    tm = _row_tile(m, 256)
    rows = lambda width: pl.BlockSpec((tm, width), lambda i: (i, 0))
    resident = lambda w: pl.BlockSpec(w.shape, lambda i: (0, 0), pipeline_mode=pl.Buffered(1))
    return pl.pallas_call(
        _merge_kernel,
        grid=(m // tm,),
        in_specs=[rows(d), rows(oa.shape[1]), rows(ob.shape[1]), rows(z.shape[1]), rows(gates.shape[1]),
                  resident(wa), resident(wb), resident(wo)],
        out_specs=rows(d),
        out_shape=jax.ShapeDtypeStruct((m, d), F32),
        compiler_params=_params("parallel"),
        name="merge",
    )(x, oa, ob, z, gates, wa, wb, wo)


def _project(x2d, g_norm, wts, n_kv_a, n_kv_b, n_heads_b):
    h = _rmsnorm(x2d, g_norm)
    kva, kvb = n_kv_a * HEAD_DIM, n_kv_b * HEAD_DIM
    (q16,) = _proj(_proj_q_kernel, h, wts["w_q"], wts["g_q"], [(BF16, wts["w_q"].shape[1])])
    ka, va, ka16, va16 = _proj(functools.partial(_proj_kv_kernel, kvw=kva, n_forget=0), h, wts["w_kva"], wts["p_kva"],
                               [(F32, kva), (F32, kva), (BF16, kva), (BF16, kva)])
    kb, vb, kb16, vb16, logf = _proj(functools.partial(_proj_kv_kernel, kvw=kvb, n_forget=n_heads_b), h, wts["w_kvb"],
                                     wts["p_kvb"], [(F32, kvb), (F32, kvb), (BF16, kvb), (BF16, kvb), (F32, n_heads_b)])
    (z,) = _proj(_proj_plain_kernel, h, wts["w_z"], None, [(F32, wts["w_z"].shape[1])])
    d = x2d.shape[1]
    (gates,) = _proj(_proj_gate_kernel, h, wts["w_g"], wts["b_g"], [(F32, 2 * d)], tn=d)
    return dict(q16=q16, ka=ka, va=va, ka16=ka16, va16=va16, kb=kb, vb=vb, kb16=kb16, vb16=vb16, logf=logf, z=z,
                gates=gates)


def _layer_weights(w_in, b_forget, b_merge, gq_a, gk_a, gq_b, gk_b, wa, wb, wo, dims):
    d, wa_w, wb_w, kva, kvb, hb = dims
    offs = [0]
    for wdt in (wa_w, kva, kva, wa_w, wb_w, kvb, kvb, wb_w, hb, d, d):
        offs.append(offs[-1] + wdt)
    col = lambda i, j=None: w_in[:, offs[i]:offs[i + 1 if j is None else j]]
    f_pad = LANES - hb
    return dict(
        w_q=jnp.concatenate([col(0), col(4)], axis=1).astype(BF16),
        g_q=jnp.concatenate([jnp.tile(gq_a, wa_w // HEAD_DIM), jnp.tile(gq_b, wb_w // HEAD_DIM)]).reshape(1, -1),
        w_kva=col(1, 3).astype(BF16),
        p_kva=jnp.concatenate([jnp.tile(gk_a, kva // HEAD_DIM), jnp.zeros((kva,), F32)]).reshape(1, -1),
        w_kvb=jnp.concatenate([col(5, 7), col(8), jnp.zeros((d, f_pad), F32)], axis=1).astype(BF16),
        p_kvb=jnp.concatenate([jnp.tile(gk_b, kvb // HEAD_DIM), jnp.zeros((kvb,), F32), b_forget,
                               jnp.zeros((f_pad,), F32)]).reshape(1, -1),
        w_z=jnp.concatenate([col(3), col(7)], axis=1).astype(BF16),
        w_g=col(9, 11).astype(BF16),
        b_g=b_merge.reshape(1, 2 * d),
        wa=wa.astype(BF16), wb=wb.astype(BF16), wo=wo.astype(BF16),
    )


def kernel(x_prompt, x_sample, cache_k_moba, cache_v_moba, cache_k_fox, cache_v_fox, cache_logf_fox, page_table,
           g_norm, w_in, b_forget, b_merge, gq_moba, gk_moba, gq_fox, gk_fox, w_branch_moba, w_branch_fox, w_out):
    depth = w_in.shape[0]
    batch, seq, d = x_prompt.shape
    rows, dec_seq, _ = x_sample.shape
    assert dec_seq == 1 and seq % MOBA_BLOCK == 0
    n_kv_a, n_kv_b = cache_k_moba.shape[3], cache_k_fox.shape[3]
    hb = b_forget.shape[1]
    wa_w, wb_w = w_branch_moba.shape[1], w_branch_fox.shape[1]
    ha = wa_w // HEAD_DIM
    assert wb_w == hb * HEAD_DIM
    kva, kvb = n_kv_a * HEAD_DIM, n_kv_b * HEAD_DIM
    pool, page = cache_k_moba.shape[1:3]
    slopes = 2.0 ** (-8.0 * jnp.arange(1, ha + 1, dtype=F32) / ha)

    xp = x_prompt.reshape(batch * seq, d)
    xs = x_sample.reshape(rows, d)
    outs_p, outs_s = [], []
    for l in range(depth):
        wts = _layer_weights(w_in[l], b_forget[l], b_merge[l], gq_moba[l], gk_moba[l], gq_fox[l], gk_fox[l],
                             w_branch_moba[l], w_branch_fox[l], w_out[l], (d, wa_w, wb_w, kva, kvb, hb))
        pp = _project(xp, g_norm[l], wts, n_kv_a, n_kv_b, hb)
        kmean = _kmean(pp["ka"], batch, seq)
        oa = _moba_prompt(pp["q16"], 0, pp["ka16"], pp["va16"], kmean, slopes, batch, seq, ha, n_kv_a)
        group_b = hb // n_kv_b
        c_row = _cumsum_rows(pp["logf"].reshape(batch, seq, hb).transpose(0, 2, 1))
        c_col = c_row.reshape(batch, n_kv_b, group_b, seq).transpose(1, 0, 3, 2).reshape(n_kv_b, batch * seq, group_b)
        ob = _fox_prompt(pp["q16"], wa_w, pp["kb16"], pp["vb16"], c_col,
                         c_row.reshape(batch, n_kv_b, group_b, seq), batch, seq, hb, n_kv_b)
        xp = _merge(xp, oa, ob, pp["z"], pp["gates"], wts["wa"], wts["wb"], wts["wo"])
        outs_p.append((pp["ka"], pp["va"], pp["kb"], pp["vb"], pp["logf"]))
        ps = _project(xs, g_norm[l], wts, n_kv_a, n_kv_b, hb)
        oa_s = _decode_attention(_moba_decode_kernel, page_table, ps["q16"][:, :wa_w], ps["ka"], ps["va"],
                                 slopes.reshape(1, ha), False,
                                 [cache_k_moba[l].reshape(pool, page, kva), cache_v_moba[l].reshape(pool, page, kva)],
                                 n_kv_a)
        ob_s = _decode_attention(_fox_decode_kernel, page_table, ps["q16"][:, wa_w:], ps["kb"], ps["vb"], ps["logf"], True,
                                 [cache_k_fox[l].reshape(pool, page, kvb), cache_v_fox[l].reshape(pool, page, kvb),
                                  cache_logf_fox[l]], n_kv_b)
        xs = _merge(xs, oa_s, ob_s, ps["z"], ps["gates"], wts["wa"], wts["wb"], wts["wo"])
        outs_s.append((ps["ka"], ps["va"], ps["kb"], ps["vb"], ps["logf"]))

    def stacked(outs, i, shape):
        return jnp.stack([o[i].reshape(shape) for o in outs])

    res = [xp.reshape(batch, seq, d), xs.reshape(rows, 1, d)]
    for outs, lead in ((outs_p, (batch, seq)), (outs_s, (rows, 1))):
        res += [stacked(outs, 0, lead + (n_kv_a, HEAD_DIM)), stacked(outs, 1, lead + (n_kv_a, HEAD_DIM)),
                stacked(outs, 2, lead + (n_kv_b, HEAD_DIM)), stacked(outs, 3, lead + (n_kv_b, HEAD_DIM)),
                stacked(outs, 4, lead + (hb,))]
    return tuple(res)
```

```python
import functools

import jax
import jax.numpy as jnp
from jax import lax
from jax.experimental import pallas as pl
from jax.experimental.pallas import tpu as pltpu

HEAD_DIM = 128
MOBA_BLOCK = 256
MOBA_TOPK = 3
NORM_EPS = 1e-6
NEG_INF = -1e30
M_INIT = 0.5 * NEG_INF
SCALE = HEAD_DIM ** -0.5
LOG2E = 1.4426950408889634
LANES = 128
SUBLANES = 8
VMEM_LIMIT = 52 * 1024 * 1024
ATTN_TILE = 512

F32 = jnp.float32
BF16 = jnp.bfloat16
HI = lax.Precision.HIGHEST
NT_DIMS = (((1,), (1,)), ((), ()))
TN_DIMS = (((0,), (0,)), ((), ()))


def _params(*sem):
    return pltpu.CompilerParams(dimension_semantics=sem, vmem_limit_bytes=VMEM_LIMIT)


def _row_tile(m, want):
    t = min(m, want)
    assert m % t == 0
    return t


def _rmsnorm_kernel(x_ref, g_ref, h_ref):
    x = x_ref[...]
    ms = jnp.mean(x * x, axis=-1, keepdims=True)
    h_ref[...] = (x * lax.rsqrt(ms + NORM_EPS) * g_ref[...]).astype(h_ref.dtype)


def _rmsnorm(x, g):
    m, d = x.shape
    tm = _row_tile(m, 512)
    return pl.pallas_call(
        _rmsnorm_kernel,
        grid=(m // tm,),
        in_specs=[pl.BlockSpec((tm, d), lambda i: (i, 0)), pl.BlockSpec((1, d), lambda i: (0, 0))],
        out_specs=pl.BlockSpec((tm, d), lambda i: (i, 0)),
        out_shape=jax.ShapeDtypeStruct((m, d), BF16),
        compiler_params=_params("parallel"),
        name="rmsnorm",
    )(x, g.reshape(1, d))


def _head_norm(a, g):
    ms = jnp.mean(a * a, axis=-1, keepdims=True)
    return a * lax.rsqrt(ms + NORM_EPS) * g


def _log_sigmoid(x):
    return jnp.minimum(x, 0.0) - jnp.log1p(jnp.exp(-jnp.abs(x)))


def _proj_q_kernel(h_ref, w_ref, g_ref, o_ref):
    acc = jnp.dot(h_ref[...], w_ref[...], preferred_element_type=F32)
    for j in range(acc.shape[1] // HEAD_DIM):
        sl = slice(j * HEAD_DIM, (j + 1) * HEAD_DIM)
        o_ref[:, sl] = _head_norm(acc[:, sl], g_ref[:, sl]).astype(o_ref.dtype)


def _proj_kv_kernel(h_ref, w_ref, g_ref, *o_refs, n_kv, n_forget):
    k32_ref, v32_ref, k16_ref, v16_ref = o_refs[:4]
    kvw = n_kv * HEAD_DIM
    acc = jnp.dot(h_ref[...], w_ref[...], preferred_element_type=F32)
    for j in range(n_kv):
        sl = slice(j * HEAD_DIM, (j + 1) * HEAD_DIM)
        kn = _head_norm(acc[:, sl], g_ref[:, sl])
        k32_ref[:, j, :] = kn
        k16_ref[:, sl] = kn.astype(BF16)
        v = acc[:, kvw + j * HEAD_DIM:kvw + (j + 1) * HEAD_DIM]
        v32_ref[:, j, :] = v
        v16_ref[:, sl] = v.astype(BF16)
    if n_forget:
        sl = slice(2 * kvw, 2 * kvw + n_forget)
        o_refs[4][...] = _log_sigmoid(acc[:, sl] + g_ref[:, sl])


def _proj_plain_kernel(h_ref, w_ref, o_ref):
    o_ref[...] = jnp.dot(h_ref[...], w_ref[...], preferred_element_type=F32).astype(o_ref.dtype)


def _proj_gate_kernel(h_ref, w_ref, b_ref, o_ref):
    acc = jnp.dot(h_ref[...], w_ref[...], preferred_element_type=F32)
    o_ref[...] = jax.nn.sigmoid(acc + b_ref[...]).astype(o_ref.dtype)


def _proj(body, h, w, params, outs, tn=None):
    m, d = h.shape
    n = w.shape[1]
    tn = n if tn is None else tn
    tm = _row_tile(m, 512)
    in_specs = [pl.BlockSpec((tm, d), lambda j, i: (i, 0)), pl.BlockSpec((d, tn), lambda j, i: (0, j))]
    args = [h, w]
    if params is not None:
        in_specs.append(pl.BlockSpec((1, tn), lambda j, i: (0, j)))
        args.append(params)
    single = n // tn == 1
    out_specs, out_shape = [], []
    for dt, trail in outs:
        if len(trail) == 2:
            assert single
            out_specs.append(pl.BlockSpec((tm,) + trail, lambda j, i: (i, 0, 0)))
        else:
            out_specs.append(pl.BlockSpec((tm, trail[0] if single else tn), lambda j, i: (i, j)))
        out_shape.append(jax.ShapeDtypeStruct((m,) + trail, dt))
    return pl.pallas_call(
        body,
        grid=(n // tn, m // tm),
        in_specs=in_specs,
        out_specs=out_specs,
        out_shape=out_shape,
        compiler_params=_params("parallel", "parallel"),
        name="proj",
    )(*args)


def _kmean_kernel(k_ref, o_ref, *, nblk, n_kv):
    o_ref[...] = jnp.zeros_like(o_ref)
    for j in range(n_kv):
        for n in range(nblk):
            blk = k_ref[n * MOBA_BLOCK:(n + 1) * MOBA_BLOCK, j, :]
            o_ref[0, j, n:n + 1, :] = jnp.sum(blk, axis=0, keepdims=True) * (1.0 / MOBA_BLOCK)


def _kmean(k32, batch, seq):
    n_kv = k32.shape[1]
    nblk = seq // MOBA_BLOCK
    assert nblk <= LANES
    return pl.pallas_call(
        functools.partial(_kmean_kernel, nblk=nblk, n_kv=n_kv),
        grid=(batch,),
        in_specs=[pl.BlockSpec((seq, n_kv, HEAD_DIM), lambda b: (b, 0, 0))],
        out_specs=pl.BlockSpec((1, n_kv, LANES, HEAD_DIM), lambda b: (b, 0, 0, 0)),
        out_shape=jax.ShapeDtypeStruct((batch, n_kv, LANES, HEAD_DIM), F32),
        compiler_params=_params("parallel"),
        name="kmean",
    )(k32)


def _softmax_block_t(h, parts, v, m_sc, l_sc, acc_sc):
    m_prev = m_sc[h]
    m_new = m_prev
    for u, rb in parts:
        m_new = jnp.maximum(m_new, jnp.max(u, axis=0, keepdims=True) + rb)
    alpha = jnp.exp2(m_prev - m_new)
    l_new = alpha * l_sc[h]
    ps = []
    for u, rb in parts:
        p = jnp.exp2(u - (m_new - rb))
        l_new = l_new + jnp.sum(p, axis=0, keepdims=True)
        ps.append(p.astype(v.dtype))
    p16 = ps[0] if len(ps) == 1 else jnp.concatenate(ps, axis=0)
    acc_sc[h] = alpha * acc_sc[h] + lax.dot_general(v, p16, TN_DIMS, preferred_element_type=F32)
    l_sc[h] = l_new
    m_sc[h] = m_new


def _init_softmax_state(m_sc, l_sc, acc_sc):
    m_sc[...] = jnp.full(m_sc.shape, M_INIT, F32)
    l_sc[...] = jnp.zeros(l_sc.shape, F32)
    acc_sc[...] = jnp.zeros(acc_sc.shape, F32)


def _write_heads(o_ref, l_sc, acc_sc, group):
    for h in range(group):
        o_ref[:, h * HEAD_DIM:(h + 1) * HEAD_DIM] = (acc_sc[h] / l_sc[h]).T.astype(o_ref.dtype)


def _moba_prompt_kernel(slopes_ref, q_ref, k_ref, v_ref, kmean_ref, o_ref, m_sc, l_sc, acc_sc, selb_sc, alibi_sc,
                        *, group, nblk):
    tq = q_ref.shape[0]
    per_tile = tq // MOBA_BLOCK
    kvh = pl.program_id(1)
    qi = pl.program_id(2)
    _init_softmax_state(m_sc, l_sc, acc_sc)

    qcol = lax.broadcasted_iota(jnp.int32, (1, tq), 1)
    own = qi * per_tile + qcol // MOBA_BLOCK
    krow = lax.broadcasted_iota(jnp.int32, (MOBA_BLOCK, tq), 0)
    rel = lax.broadcasted_iota(jnp.int32, (MOBA_BLOCK, tq), 1) - krow
    blk = lax.broadcasted_iota(jnp.int32, (nblk, tq), 0)
    km = kmean_ref[0, 0]
    for h in range(group):
        qh = q_ref[:, h * HEAD_DIM:(h + 1) * HEAD_DIM].astype(F32)
        g = lax.dot_general(km, qh, NT_DIMS, precision=HI, preferred_element_type=F32)[:nblk]
        cnt = jnp.zeros((nblk, tq), jnp.int32)
        for m in range(nblk):
            gm = g[m:m + 1, :]
            beats = (gm > g) | ((gm == g) & (blk > m))
            cnt = cnt + (beats & (own > m)).astype(jnp.int32)
        sel = (blk < own) & (cnt < MOBA_TOPK)
        selb_sc[h] = jnp.where(sel, 0.0, NEG_INF)
        alibi_sc[h] = (slopes_ref[kvh * group + h] * LOG2E) * rel.astype(F32)

    def logits(h, k):
        qh = q_ref[:, h * HEAD_DIM:(h + 1) * HEAD_DIM]
        return lax.dot_general(k, qh, NT_DIMS, preferred_element_type=F32) * (SCALE * LOG2E) - alibi_sc[h]

    def block_bias(h, kb):
        return (slopes_ref[kvh * group + h] * LOG2E) * ((qi * tq - kb * MOBA_BLOCK).astype(F32))

    def past_tile(ki, carry):
        off = pl.multiple_of(ki * tq, tq)
        v = v_ref[pl.ds(off, tq), :]
        ks = [k_ref[pl.ds(pl.multiple_of(off + c * MOBA_BLOCK, MOBA_BLOCK), MOBA_BLOCK), :] for c in range(per_tile)]
        for h in range(group):
            parts = []
            for c in range(per_tile):
                kb = ki * per_tile + c
                parts.append((logits(h, ks[c]), selb_sc[h, pl.ds(kb, 1), :] - block_bias(h, kb)))
            _softmax_block_t(h, parts, v, m_sc, l_sc, acc_sc)
        return carry

    lax.fori_loop(0, qi, past_tile, 0)

    off = pl.multiple_of(qi * tq, tq)
    v = v_ref[pl.ds(off, tq), :]
    for h in range(group):
        parts = []
        for c in range(per_tile):
            kb = qi * per_tile + c
            k = k_ref[pl.ds(pl.multiple_of(off + c * MOBA_BLOCK, MOBA_BLOCK), MOBA_BLOCK), :]
            u = jnp.where(rel >= c * MOBA_BLOCK, logits(h, k), NEG_INF)
            rb = jnp.where(own > kb, selb_sc[h, pl.ds(kb, 1), :], 0.0) - block_bias(h, kb)
            parts.append((u, rb))
        _softmax_block_t(h, parts, v, m_sc, l_sc, acc_sc)
    _write_heads(o_ref, l_sc, acc_sc, group)


def _moba_prompt(q16, q_col0, k16, v16, kmean, slopes, batch, seq, n_heads, n_kv):
    group = n_heads // n_kv
    tq = _row_tile(seq, ATTN_TILE)
    assert tq % MOBA_BLOCK == 0
    nq = seq // tq
    nblk = seq // MOBA_BLOCK
    gw = group * HEAD_DIM
    qb0 = q_col0 // gw
    return pl.pallas_call(
        functools.partial(_moba_prompt_kernel, group=group, nblk=nblk),
        grid=(batch, n_kv, nq),
        in_specs=[
            pl.BlockSpec(memory_space=pltpu.SMEM),
            pl.BlockSpec((tq, gw), lambda b, j, i: (b * nq + i, qb0 + j)),
            pl.BlockSpec((seq, HEAD_DIM), lambda b, j, i: (b, j)),
            pl.BlockSpec((seq, HEAD_DIM), lambda b, j, i: (b, j)),
            pl.BlockSpec((1, 1, LANES, HEAD_DIM), lambda b, j, i: (b, j, 0, 0)),
        ],
        out_specs=pl.BlockSpec((tq, gw), lambda b, j, i: (b * nq + i, j)),
        out_shape=jax.ShapeDtypeStruct((batch * seq, n_heads * HEAD_DIM), F32),
        scratch_shapes=[
            pltpu.VMEM((group, 1, tq), F32),
            pltpu.VMEM((group, 1, tq), F32),
            pltpu.VMEM((group, HEAD_DIM, tq), F32),
            pltpu.VMEM((group, nblk, tq), F32),
            pltpu.VMEM((group, MOBA_BLOCK, tq), F32),
        ],
        compiler_params=_params("parallel", "parallel", "arbitrary"),
        name="moba_prompt",
    )(slopes, q16, k16, v16, kmean)


def _fox_prompt_kernel(q_ref, k_ref, v_ref, ccol_ref, crow_ref, o_ref, m_sc, l_sc, acc_sc, ckb_sc, *, group):
    tq = q_ref.shape[0]
    qi = pl.program_id(2)
    _init_softmax_state(m_sc, l_sc, acc_sc)

    @pl.when(qi == 0)
    def _():
        for h in range(group):
            ckb_sc[h] = jnp.broadcast_to(ccol_ref[0, :, h:h + 1] * LOG2E, ckb_sc.shape[1:])

    def logits(h, off, k):
        qh = q_ref[:, h * HEAD_DIM:(h + 1) * HEAD_DIM]
        s = lax.dot_general(k, qh, NT_DIMS, preferred_element_type=F32) * (SCALE * LOG2E)
        return s - jnp.concatenate([ckb_sc[h, pl.ds(off, tq), :]] * (tq // LANES), axis=1)

    def cq(h):
        return crow_ref[0, 0, h:h + 1, pl.ds(pl.multiple_of(qi * tq, tq), tq)] * LOG2E

    def past_tile(ki, carry):
        off = pl.multiple_of(ki * tq, tq)
        k = k_ref[pl.ds(off, tq), :]
        v = v_ref[pl.ds(off, tq), :]
        for h in range(group):
            _softmax_block_t(h, [(logits(h, off, k), cq(h))], v, m_sc, l_sc, acc_sc)
        return carry

    lax.fori_loop(0, qi, past_tile, 0)

    krow = lax.broadcasted_iota(jnp.int32, (tq, tq), 0)
    qcol = lax.broadcasted_iota(jnp.int32, (tq, tq), 1)
    off = pl.multiple_of(qi * tq, tq)
    k = k_ref[pl.ds(off, tq), :]
    v = v_ref[pl.ds(off, tq), :]
    for h in range(group):
        u = jnp.where(qcol >= krow, logits(h, off, k), NEG_INF)
        _softmax_block_t(h, [(u, cq(h))], v, m_sc, l_sc, acc_sc)
    _write_heads(o_ref, l_sc, acc_sc, group)


def _fox_prompt(q16, q_col0, k16, v16, c_col, c_row, batch, seq, n_heads, n_kv):
    group = n_heads // n_kv
    tq = _row_tile(seq, ATTN_TILE)
    nq = seq // tq
    gw = group * HEAD_DIM
    qb0 = q_col0 // gw
    return pl.pallas_call(
        functools.partial(_fox_prompt_kernel, group=group),
        grid=(batch, n_kv, nq),
        in_specs=[
            pl.BlockSpec((tq, gw), lambda b, j, i: (b * nq + i, qb0 + j)),
            pl.BlockSpec((seq, HEAD_DIM), lambda b, j, i: (b, j)),
            pl.BlockSpec((seq, HEAD_DIM), lambda b, j, i: (b, j)),
            pl.BlockSpec((1, seq, group), lambda b, j, i: (j, b, 0)),
            pl.BlockSpec((1, 1, group, seq), lambda b, j, i: (b, j, 0, 0)),
        ],
        out_specs=pl.BlockSpec((tq, gw), lambda b, j, i: (b * nq + i, j)),
        out_shape=jax.ShapeDtypeStruct((batch * seq, n_heads * HEAD_DIM), F32),
        scratch_shapes=[
            pltpu.VMEM((group, 1, tq), F32),
            pltpu.VMEM((group, 1, tq), F32),
            pltpu.VMEM((group, HEAD_DIM, tq), F32),
            pltpu.VMEM((group, seq, LANES), F32),
        ],
        compiler_params=_params("arbitrary", "arbitrary", "arbitrary"),
        name="fox_prompt",
    )(q16, k16, v16, c_col, c_row)


def _cumsum_kernel(x_ref, o_ref, *, chunk):
    rows, seq = x_ref.shape[1:]
    r = lax.broadcasted_iota(jnp.int32, (chunk, chunk), 0)
    c = lax.broadcasted_iota(jnp.int32, (chunk, chunk), 1)
    upper = (r <= c).astype(F32)
    carry = jnp.zeros((rows, 1), F32)
    for i in range(seq // chunk):
        sl = slice(i * chunk, (i + 1) * chunk)
        cs = jnp.dot(x_ref[0, :, sl], upper, precision=HI, preferred_element_type=F32) + carry
        o_ref[0, :, sl] = cs
        carry = cs[:, chunk - 1:chunk]


def _cumsum_rows(x):
    batch, rows, seq = x.shape
    chunk = 256
    assert seq % chunk == 0
    return pl.pallas_call(
        functools.partial(_cumsum_kernel, chunk=chunk),
        grid=(batch,),
        in_specs=[pl.BlockSpec((1, rows, seq), lambda b: (b, 0, 0))],
        out_specs=pl.BlockSpec((1, rows, seq), lambda b: (b, 0, 0)),
        out_shape=jax.ShapeDtypeStruct(x.shape, F32),
        compiler_params=_params("parallel"),
        name="cumsum",
    )(x)


def _page_copies(pt_ref, row, slot, layer, srcs, bufs, sems, n_pages, page):
    copies = []
    for p in range(n_pages):
        idx = pt_ref[row * n_pages + p]
        rows = pl.ds(p * page, page)
        for a, (src, buf) in enumerate(zip(srcs, bufs)):
            if len(buf.shape) == 4:
                for j in range(buf.shape[1]):
                    copies.append(pltpu.make_async_copy(src.at[layer, idx, :, j, :], buf.at[slot, j, rows, :],
                                                        sems.at[a, slot]))
            else:
                copies.append(pltpu.make_async_copy(src.at[layer, idx], buf.at[slot, rows], sems.at[a, slot]))
    return copies


def _fetch_rows(pt_ref, layer, srcs, bufs, sems, n_pages, page):
    b = pl.program_id(0)
    nb = pl.num_programs(0)
    slot = lax.rem(b, 2)

    @pl.when(b == 0)
    def _():
        for c in _page_copies(pt_ref, b, slot, layer, srcs, bufs, sems, n_pages, page):
            c.start()

    @pl.when(b + 1 < nb)
    def _():
        for c in _page_copies(pt_ref, b + 1, 1 - slot, layer, srcs, bufs, sems, n_pages, page):
            c.start()

    for c in _page_copies(pt_ref, b, slot, layer, srcs, bufs, sems, n_pages, page):
        c.wait()
    return slot


def _append_new_row(buf, slot, past, new):
    first = lax.broadcasted_iota(jnp.int32, (SUBLANES, HEAD_DIM), 0) == 0
    for j in range(new.shape[0]):
        buf[slot, j, past:past + SUBLANES, :] = jnp.where(first, new[j:j + 1, :], 0.0)


def _tail_bias(heads):
    first = lax.broadcasted_iota(jnp.int32, (SUBLANES, heads), 0) == 0
    return jnp.where(first, 0.0, NEG_INF)


def _group_queries(q, j, group):
    r = lax.broadcasted_iota(jnp.int32, q.shape, 0)
    return jnp.where(r // group == j, q, jnp.zeros_like(q))


def _decode_scores(kb, q, n_kv):
    group = q.shape[0] // n_kv
    s = None
    for j in range(n_kv):
        sj = lax.dot_general(kb[j].astype(BF16), _group_queries(q, j, group), NT_DIMS, preferred_element_type=F32)
        s = sj if s is None else s + sj
    return s


def _decode_softmax_pv(s, vb, n_kv):
    heads = s.shape[1]
    group = heads // n_kv
    m = jnp.max(s, axis=0, keepdims=True)
    p = jnp.exp(s - m)
    l = jnp.sum(p, axis=0, keepdims=True)
    p16 = p.astype(BF16)
    o_t = jnp.concatenate(
        [lax.dot_general(vb[j].astype(BF16), p16, TN_DIMS, preferred_element_type=F32) for j in range(n_kv)],
        axis=0) / l
    o = o_t.T
    return jnp.concatenate(
        [o[h:h + 1, (h // group) * HEAD_DIM:(h // group + 1) * HEAD_DIM] for h in range(heads)], axis=0)


def _moba_decode_kernel(pt_ref, q_ref, knew_ref, vnew_ref, slopes_ref, k_hbm, v_hbm, o_ref, kbuf, vbuf, sems,
                        *, layer, n_kv, n_pages, page):
    slot = _fetch_rows(pt_ref, layer, (k_hbm, v_hbm), (kbuf, vbuf), sems, n_pages, page)
    past = n_pages * page
    nblk = past // MOBA_BLOCK
    _append_new_row(kbuf, slot, past, knew_ref[0])
    _append_new_row(vbuf, slot, past, vnew_ref[0])
    kb, vb = kbuf.at[slot], vbuf.at[slot]
    q = q_ref[0]
    heads = q.shape[0]
    group = heads // n_kv

    g = None
    for j in range(n_kv):
        means = jnp.concatenate(
            [jnp.sum(kb[j, n * MOBA_BLOCK:(n + 1) * MOBA_BLOCK, :], axis=0, keepdims=True) for n in range(nblk)],
            axis=0) * (1.0 / MOBA_BLOCK)
        gj = lax.dot_general(means, _group_queries(q, j, group).astype(F32), NT_DIMS, precision=HI,
                             preferred_element_type=F32)
        g = gj if g is None else g + gj
    blk = lax.broadcasted_iota(jnp.int32, g.shape, 0)
    cnt = jnp.zeros(g.shape, jnp.int32)
    for m in range(nblk):
        gm = g[m:m + 1, :]
        cnt = cnt + ((gm > g) | ((gm == g) & (blk > m))).astype(jnp.int32)
    sel = cnt < MOBA_TOPK

    dist = (past - lax.broadcasted_iota(jnp.int32, (MOBA_BLOCK, 1), 0)).astype(F32)
    slopes = slopes_ref[...]
    bias = [jnp.where(sel[n:n + 1, :], -slopes * (dist - n * MOBA_BLOCK), NEG_INF) for n in range(nblk)]
    s = _decode_scores(kb, q, n_kv) * SCALE + jnp.concatenate(bias + [_tail_bias(heads)], axis=0)
    o_ref[0] = _decode_softmax_pv(s, vb, n_kv).astype(o_ref.dtype)


def _fox_decode_kernel(pt_ref, q_ref, knew_ref, vnew_ref, fnew_ref, k_hbm, v_hbm, f_hbm, o_ref, kbuf, vbuf, fbuf, sems,
                       *, layer, n_kv, n_pages, page):
    slot = _fetch_rows(pt_ref, layer, (k_hbm, v_hbm, f_hbm), (kbuf, vbuf, fbuf), sems, n_pages, page)
    past = n_pages * page
    _append_new_row(kbuf, slot, past, knew_ref[0])
    _append_new_row(vbuf, slot, past, vnew_ref[0])
    kb, vb = kbuf.at[slot], vbuf.at[slot]
    q = q_ref[0]
    heads = q.shape[0]

    r = lax.broadcasted_iota(jnp.int32, (page, page), 0)
    c = lax.broadcasted_iota(jnp.int32, (page, page), 1)
    lower = (r >= c).astype(F32)
    carry = jnp.zeros((1, heads), F32)
    cums = []
    for j in range(n_pages):
        cs = jnp.dot(lower, fbuf[slot, j * page:(j + 1) * page, :], precision=HI, preferred_element_type=F32) + carry
        cums.append(cs)
        carry = cs[page - 1:page, :]
    c_now = carry + fnew_ref[0]
    bias = [c_now - cs for cs in cums]
    s = _decode_scores(kb, q, n_kv) * SCALE + jnp.concatenate(bias + [_tail_bias(heads)], axis=0)
    o_ref[0] = _decode_softmax_pv(s, vb, n_kv).astype(o_ref.dtype)


def _decode_attention(body, layer, page_table, q16, k_new, v_new, extra, extra_per_row, caches, n_kv):
    rows, n_pages = page_table.shape
    heads = q16.shape[1] // HEAD_DIM
    page = caches[0].shape[2]
    past = n_pages * page
    assert past % MOBA_BLOCK == 0
    row_spec = lambda shape: pl.BlockSpec((1,) + shape, lambda b, pt: (b, 0, 0))
    in_specs = [row_spec((heads, HEAD_DIM)), row_spec((n_kv, HEAD_DIM)), row_spec((n_kv, HEAD_DIM))]
    if extra_per_row:
        in_specs.append(row_spec((1, heads)))
        extra = extra.reshape(rows, 1, heads)
    else:
        in_specs.append(pl.BlockSpec((1, heads), lambda b, pt: (0, 0)))
    in_specs += [pl.BlockSpec(memory_space=pl.ANY)] * len(caches)
    scratch = [pltpu.VMEM((2, n_kv, past + SUBLANES, HEAD_DIM) if c.ndim == 5 else (2, past, c.shape[3]), F32)
               for c in caches]
    scratch.append(pltpu.SemaphoreType.DMA((len(caches), 2)))
    return pl.pallas_call(
        functools.partial(body, layer=layer, n_kv=n_kv, n_pages=n_pages, page=page),
        grid_spec=pltpu.PrefetchScalarGridSpec(
            num_scalar_prefetch=1,
            grid=(rows,),
            in_specs=in_specs,
            out_specs=pl.BlockSpec((1, heads, HEAD_DIM), lambda b, pt: (b, 0, 0)),
            scratch_shapes=scratch,
        ),
        out_shape=jax.ShapeDtypeStruct((rows, heads, HEAD_DIM), F32),
        compiler_params=_params("arbitrary"),
        name="decode_attention",
    )(page_table.reshape(-1), q16.reshape(rows, heads, HEAD_DIM), k_new, v_new, extra, *caches
      ).reshape(rows, heads * HEAD_DIM)


def _merge_kernel(x_ref, oa_ref, ob_ref, z_ref, g_ref, wa_ref, wb_ref, wo_ref, y_ref):
    wa_w = oa_ref.shape[1]
    d = x_ref.shape[1]

    def branch(o_ref, z, w_ref):
        u = o_ref[...] * (z * jax.nn.sigmoid(z))
        return jnp.dot(u.astype(BF16), w_ref[...], preferred_element_type=F32)

    ya = branch(oa_ref, z_ref[:, :wa_w], wa_ref)
    yb = branch(ob_ref, z_ref[:, wa_w:], wb_ref)
    mixed = g_ref[:, :d] * ya + g_ref[:, d:] * yb
    y_ref[...] = x_ref[...] + jnp.dot(mixed.astype(BF16), wo_ref[...], preferred_element_type=F32)


def _merge(x, oa, ob, z, gates, wa, wb, wo):
    m, d = x.shape
    tm = _row_tile(m, 256)
    rows = lambda width: pl.BlockSpec((tm, width), lambda i: (i, 0))
    resident = lambda w: pl.BlockSpec(w.shape, lambda i: (0, 0), pipeline_mode=pl.Buffered(1))
    return pl.pallas_call(
        _merge_kernel,
        grid=(m // tm,),
        in_specs=[rows(d), rows(oa.shape[1]), rows(ob.shape[1]), rows(z.shape[1]), rows(gates.shape[1]),
                  resident(wa), resident(wb), resident(wo)],
        out_specs=rows(d),
        out_shape=jax.ShapeDtypeStruct((m, d), F32),
        compiler_params=_params("parallel"),
        name="merge",
    )(x, oa, ob, z, gates, wa, wb, wo)


def _project(x2d, g_norm, wts, n_kv_a, n_kv_b, n_heads_b):
    h = _rmsnorm(x2d, g_norm)
    kva, kvb = n_kv_a * HEAD_DIM, n_kv_b * HEAD_DIM
    (q16,) = _proj(_proj_q_kernel, h, wts["w_q"], wts["g_q"], [(BF16, (wts["w_q"].shape[1],))])
    ka, va, ka16, va16 = _proj(functools.partial(_proj_kv_kernel, n_kv=n_kv_a, n_forget=0), h, wts["w_kva"],
                               wts["p_kva"], [(F32, (n_kv_a, HEAD_DIM)), (F32, (n_kv_a, HEAD_DIM)),
                                              (BF16, (kva,)), (BF16, (kva,))])
    kb, vb, kb16, vb16, logf = _proj(functools.partial(_proj_kv_kernel, n_kv=n_kv_b, n_forget=n_heads_b), h,
                                     wts["w_kvb"], wts["p_kvb"],
                                     [(F32, (n_kv_b, HEAD_DIM)), (F32, (n_kv_b, HEAD_DIM)), (BF16, (kvb,)),
                                      (BF16, (kvb,)), (F32, (n_heads_b,))])
    (z,) = _proj(_proj_plain_kernel, h, wts["w_z"], None, [(F32, (wts["w_z"].shape[1],))])
    d = x2d.shape[1]
    (gates,) = _proj(_proj_gate_kernel, h, wts["w_g"], wts["b_g"], [(F32, (2 * d,))], tn=d)
    return dict(q16=q16, ka=ka, va=va, ka16=ka16, va16=va16, kb=kb, vb=vb, kb16=kb16, vb16=vb16, logf=logf, z=z,
                gates=gates)


def _layer_weights(w_in, b_forget, b_merge, gq_a, gk_a, gq_b, gk_b, wa, wb, wo, dims):
    d, wa_w, wb_w, kva, kvb, hb = dims
    offs = [0]
    for wdt in (wa_w, kva, kva, wa_w, wb_w, kvb, kvb, wb_w, hb, d, d):
        offs.append(offs[-1] + wdt)
    col = lambda i, j=None: w_in[:, offs[i]:offs[i + 1 if j is None else j]]
    f_pad = LANES - hb
    return dict(
        w_q=jnp.concatenate([col(0), col(4)], axis=1).astype(BF16),
        g_q=jnp.concatenate([jnp.tile(gq_a, wa_w // HEAD_DIM), jnp.tile(gq_b, wb_w // HEAD_DIM)]).reshape(1, -1),
        w_kva=col(1, 3).astype(BF16),
        p_kva=jnp.concatenate([jnp.tile(gk_a, kva // HEAD_DIM), jnp.zeros((kva,), F32)]).reshape(1, -1),
        w_kvb=jnp.concatenate([col(5, 7), col(8), jnp.zeros((d, f_pad), F32)], axis=1).astype(BF16),
        p_kvb=jnp.concatenate([jnp.tile(gk_b, kvb // HEAD_DIM), jnp.zeros((kvb,), F32), b_forget,
                               jnp.zeros((f_pad,), F32)]).reshape(1, -1),
        w_z=jnp.concatenate([col(3), col(7)], axis=1).astype(BF16),
        w_g=col(9, 11).astype(BF16),
        b_g=b_merge.reshape(1, 2 * d),
        wa=wa.astype(BF16), wb=wb.astype(BF16), wo=wo.astype(BF16),
    )


def kernel(x_prompt, x_sample, cache_k_moba, cache_v_moba, cache_k_fox, cache_v_fox, cache_logf_fox, page_table,
           g_norm, w_in, b_forget, b_merge, gq_moba, gk_moba, gq_fox, gk_fox, w_branch_moba, w_branch_fox, w_out):
    depth = w_in.shape[0]
    batch, seq, d = x_prompt.shape
    rows, dec_seq, _ = x_sample.shape
    assert dec_seq == 1 and seq % MOBA_BLOCK == 0
    n_kv_a, n_kv_b = cache_k_moba.shape[3], cache_k_fox.shape[3]
    hb = b_forget.shape[1]
    wa_w, wb_w = w_branch_moba.shape[1], w_branch_fox.shape[1]
    ha = wa_w // HEAD_DIM
    assert wb_w == hb * HEAD_DIM
    kva, kvb = n_kv_a * HEAD_DIM, n_kv_b * HEAD_DIM
    slopes = 2.0 ** (-8.0 * jnp.arange(1, ha + 1, dtype=F32) / ha)

    xp = x_prompt.reshape(batch * seq, d)
    xs = x_sample.reshape(rows, d)
    outs_p, outs_s = [], []
    for l in range(depth):
        wts = _layer_weights(w_in[l], b_forget[l], b_merge[l], gq_moba[l], gk_moba[l], gq_fox[l], gk_fox[l],
                             w_branch_moba[l], w_branch_fox[l], w_out[l], (d, wa_w, wb_w, kva, kvb, hb))
        pp = _project(xp, g_norm[l], wts, n_kv_a, n_kv_b, hb)
        kmean = _kmean(pp["ka"], batch, seq)
        oa = _moba_prompt(pp["q16"], 0, pp["ka16"], pp["va16"], kmean, slopes, batch, seq, ha, n_kv_a)
        group_b = hb // n_kv_b
        c_row = _cumsum_rows(pp["logf"].reshape(batch, seq, hb).transpose(0, 2, 1))
        c_col = c_row.reshape(batch, n_kv_b, group_b, seq).transpose(1, 0, 3, 2).reshape(n_kv_b, batch * seq, group_b)
        ob = _fox_prompt(pp["q16"], wa_w, pp["kb16"], pp["vb16"], c_col,
                         c_row.reshape(batch, n_kv_b, group_b, seq), batch, seq, hb, n_kv_b)
        xp = _merge(xp, oa, ob, pp["z"], pp["gates"], wts["wa"], wts["wb"], wts["wo"])
        outs_p.append((pp["ka"], pp["va"], pp["kb"], pp["vb"], pp["logf"]))
        ps = _project(xs, g_norm[l], wts, n_kv_a, n_kv_b, hb)
        oa_s = _decode_attention(_moba_decode_kernel, l, page_table, ps["q16"][:, :wa_w], ps["ka"], ps["va"],
                                 slopes.reshape(1, ha), False, [cache_k_moba, cache_v_moba], n_kv_a)
        ob_s = _decode_attention(_fox_decode_kernel, l, page_table, ps["q16"][:, wa_w:], ps["kb"], ps["vb"],
                                 ps["logf"], True, [cache_k_fox, cache_v_fox, cache_logf_fox], n_kv_b)
        xs = _merge(xs, oa_s, ob_s, ps["z"], ps["gates"], wts["wa"], wts["wb"], wts["wo"])
        outs_s.append((ps["ka"], ps["va"], ps["kb"], ps["vb"], ps["logf"]))

    def stacked(outs, i, shape):
        return jnp.stack([o[i].reshape(shape) for o in outs])

    res = [xp.reshape(batch, seq, d), xs.reshape(rows, 1, d)]
    for outs, lead in ((outs_p, (batch, seq)), (outs_s, (rows, 1))):
        res += [stacked(outs, 0, lead + (n_kv_a, HEAD_DIM)), stacked(outs, 1, lead + (n_kv_a, HEAD_DIM)),
                stacked(outs, 2, lead + (n_kv_b, HEAD_DIM)), stacked(outs, 3, lead + (n_kv_b, HEAD_DIM)),
                stacked(outs, 4, lead + (hb,))]
    return tuple(res)
```

```python
import functools

import jax
import jax.numpy as jnp
from jax import lax
from jax.experimental import pallas as pl
from jax.experimental.pallas import tpu as pltpu

HEAD_DIM = 128
MOBA_BLOCK = 256
MOBA_TOPK = 3
NORM_EPS = 1e-6
NEG_INF = -1e30
M_INIT = 0.5 * NEG_INF
SCALE = HEAD_DIM ** -0.5
LOG2E = 1.4426950408889634
LANES = 128
SUBLANES = 8
VMEM_LIMIT = 52 * 1024 * 1024
ATTN_TILE = 512

F32 = jnp.float32
BF16 = jnp.bfloat16
HI = lax.Precision.HIGHEST
NT_DIMS = (((1,), (1,)), ((), ()))
TN_DIMS = (((0,), (0,)), ((), ()))


def _params(*sem):
    return pltpu.CompilerParams(dimension_semantics=sem, vmem_limit_bytes=VMEM_LIMIT)


def _row_tile(m, want):
    t = min(m, want)
    assert m % t == 0
    return t


def _rmsnorm_kernel(x_ref, g_ref, h_ref):
    x = x_ref[...]
    ms = jnp.mean(x * x, axis=-1, keepdims=True)
    h_ref[...] = (x * lax.rsqrt(ms + NORM_EPS) * g_ref[...]).astype(h_ref.dtype)


def _rmsnorm(x, g):
    m, d = x.shape
    tm = _row_tile(m, 512)
    return pl.pallas_call(
        _rmsnorm_kernel,
        grid=(m // tm,),
        in_specs=[pl.BlockSpec((tm, d), lambda i: (i, 0)), pl.BlockSpec((1, d), lambda i: (0, 0))],
        out_specs=pl.BlockSpec((tm, d), lambda i: (i, 0)),
        out_shape=jax.ShapeDtypeStruct((m, d), BF16),
        compiler_params=_params("parallel"),
        name="rmsnorm",
    )(x, g.reshape(1, d))


def _head_norm(a, g):
    ms = jnp.mean(a * a, axis=-1, keepdims=True)
    return a * lax.rsqrt(ms + NORM_EPS) * g


def _log_sigmoid(x):
    return jnp.minimum(x, 0.0) - jnp.log1p(jnp.exp(-jnp.abs(x)))


def _proj_q_kernel(h_ref, w_ref, g_ref, o_ref):
    acc = jnp.dot(h_ref[...], w_ref[...], preferred_element_type=F32)
    for j in range(acc.shape[1] // HEAD_DIM):
        sl = slice(j * HEAD_DIM, (j + 1) * HEAD_DIM)
        o_ref[:, sl] = _head_norm(acc[:, sl], g_ref[:, sl]).astype(o_ref.dtype)


def _proj_kv_kernel(h_ref, w_ref, g_ref, *o_refs, n_kv, n_forget):
    k32_ref, v32_ref, k16_ref, v16_ref = o_refs[:4]
    kvw = n_kv * HEAD_DIM
    acc = jnp.dot(h_ref[...], w_ref[...], preferred_element_type=F32)
    for j in range(n_kv):
        sl = slice(j * HEAD_DIM, (j + 1) * HEAD_DIM)
        kn = _head_norm(acc[:, sl], g_ref[:, sl])
        k32_ref[:, j, :] = kn
        k16_ref[:, sl] = kn.astype(BF16)
        v = acc[:, kvw + j * HEAD_DIM:kvw + (j + 1) * HEAD_DIM]
        v32_ref[:, j, :] = v
        v16_ref[:, sl] = v.astype(BF16)
    if n_forget:
        sl = slice(2 * kvw, 2 * kvw + n_forget)
        o_refs[4][...] = _log_sigmoid(acc[:, sl] + g_ref[:, sl])


def _proj_plain_kernel(h_ref, w_ref, o_ref):
    o_ref[...] = jnp.dot(h_ref[...], w_ref[...], preferred_element_type=F32).astype(o_ref.dtype)


def _proj_gate_kernel(h_ref, w_ref, b_ref, o_ref):
    acc = jnp.dot(h_ref[...], w_ref[...], preferred_element_type=F32)
    o_ref[...] = jax.nn.sigmoid(acc + b_ref[...]).astype(o_ref.dtype)


def _proj(body, h, w, params, outs, tn=None):
    m, d = h.shape
    n = w.shape[1]
    tn = n if tn is None else tn
    tm = _row_tile(m, 512)
    in_specs = [pl.BlockSpec((tm, d), lambda j, i: (i, 0)), pl.BlockSpec((d, tn), lambda j, i: (0, j))]
    args = [h, w]
    if params is not None:
        in_specs.append(pl.BlockSpec((1, tn), lambda j, i: (0, j)))
        args.append(params)
    single = n // tn == 1
    out_specs, out_shape = [], []
    for dt, trail in outs:
        if len(trail) == 2:
            assert single
            out_specs.append(pl.BlockSpec((tm,) + trail, lambda j, i: (i, 0, 0)))
        else:
            out_specs.append(pl.BlockSpec((tm, trail[0] if single else tn), lambda j, i: (i, j)))
        out_shape.append(jax.ShapeDtypeStruct((m,) + trail, dt))
    return pl.pallas_call(
        body,
        grid=(n // tn, m // tm),
        in_specs=in_specs,
        out_specs=out_specs,
        out_shape=out_shape,
        compiler_params=_params("parallel", "parallel"),
        name="proj",
    )(*args)


def _kmean_kernel(k_ref, o_ref, *, nblk, n_kv):
    o_ref[...] = jnp.zeros_like(o_ref)
    for j in range(n_kv):
        for n in range(nblk):
            blk = k_ref[n * MOBA_BLOCK:(n + 1) * MOBA_BLOCK, j, :]
            o_ref[0, j, n:n + 1, :] = jnp.sum(blk, axis=0, keepdims=True) * (1.0 / MOBA_BLOCK)


def _kmean(k32, batch, seq):
    n_kv = k32.shape[1]
    nblk = seq // MOBA_BLOCK
    assert nblk <= LANES
    return pl.pallas_call(
        functools.partial(_kmean_kernel, nblk=nblk, n_kv=n_kv),
        grid=(batch,),
        in_specs=[pl.BlockSpec((seq, n_kv, HEAD_DIM), lambda b: (b, 0, 0))],
        out_specs=pl.BlockSpec((1, n_kv, LANES, HEAD_DIM), lambda b: (b, 0, 0, 0)),
        out_shape=jax.ShapeDtypeStruct((batch, n_kv, LANES, HEAD_DIM), F32),
        compiler_params=_params("parallel"),
        name="kmean",
    )(k32)


def _qk_logits(k, q_ref, group):
    tq = q_ref.shape[0]
    q_all = jnp.concatenate([q_ref[:, h * HEAD_DIM:(h + 1) * HEAD_DIM] for h in range(group)], axis=0)
    s = lax.dot_general(k, q_all, NT_DIMS, preferred_element_type=F32)
    return [s[:, h * tq:(h + 1) * tq] for h in range(group)]


def _softmax_tile_t(head_parts, v, m_sc, l_sc, acc_sc):
    heads = range(len(head_parts))
    tq = head_parts[0][0][0].shape[1]
    m_prev = [m_sc[h] for h in heads]
    m_new = []
    for h in heads:
        m = m_prev[h]
        for u, rb in head_parts[h]:
            m = jnp.maximum(m, jnp.max(u, axis=0, keepdims=True) + rb)
        m_new.append(m)
    alphas, p16s = [], []
    for h in heads:
        alpha = jnp.exp2(m_prev[h] - m_new[h])
        l_new = alpha * l_sc[h]
        ps = []
        for u, rb in head_parts[h]:
            p = jnp.exp2(u - (m_new[h] - rb))
            l_new = l_new + jnp.sum(p, axis=0, keepdims=True)
            ps.append(p.astype(v.dtype))
        p16s.append(ps[0] if len(ps) == 1 else jnp.concatenate(ps, axis=0))
        alphas.append(alpha)
        l_sc[h] = l_new
        m_sc[h] = m_new[h]
    pv = lax.dot_general(v, jnp.concatenate(p16s, axis=1), TN_DIMS, preferred_element_type=F32)
    for h in heads:
        acc_sc[h] = alphas[h] * acc_sc[h] + pv[:, h * tq:(h + 1) * tq]


def _init_softmax_state(m_sc, l_sc, acc_sc):
    m_sc[...] = jnp.full(m_sc.shape, M_INIT, F32)
    l_sc[...] = jnp.zeros(l_sc.shape, F32)
    acc_sc[...] = jnp.zeros(acc_sc.shape, F32)


def _write_heads(o_ref, l_sc, acc_sc, group):
    for h in range(group):
        o_ref[:, h * HEAD_DIM:(h + 1) * HEAD_DIM] = (acc_sc[h] / l_sc[h]).T.astype(o_ref.dtype)


def _moba_prompt_kernel(slopes_ref, q_ref, k_ref, v_ref, kmean_ref, o_ref, m_sc, l_sc, acc_sc, selb_sc, alibi_sc,
                        *, group, nblk):
    tq = q_ref.shape[0]
    per_tile = tq // MOBA_BLOCK
    kvh = pl.program_id(1)
    qi = pl.program_id(2)
    _init_softmax_state(m_sc, l_sc, acc_sc)

    qcol = lax.broadcasted_iota(jnp.int32, (1, tq), 1)
    own = qi * per_tile + qcol // MOBA_BLOCK
    krow = lax.broadcasted_iota(jnp.int32, (MOBA_BLOCK, tq), 0)
    rel = lax.broadcasted_iota(jnp.int32, (MOBA_BLOCK, tq), 1) - krow
    blk = lax.broadcasted_iota(jnp.int32, (nblk, tq), 0)
    km = kmean_ref[0, 0]
    for h in range(group):
        qh = q_ref[:, h * HEAD_DIM:(h + 1) * HEAD_DIM].astype(F32)
        g = lax.dot_general(km, qh, NT_DIMS, precision=HI, preferred_element_type=F32)[:nblk]
        cnt = jnp.zeros((nblk, tq), jnp.int32)
        for m in range(nblk):
            gm = g[m:m + 1, :]
            beats = (gm > g) | ((gm == g) & (blk > m))
            cnt = cnt + (beats & (own > m)).astype(jnp.int32)
        sel = (blk < own) & (cnt < MOBA_TOPK)
        selb_sc[h] = jnp.where(sel, 0.0, NEG_INF)
        alibi_sc[h] = (slopes_ref[kvh * group + h] * LOG2E) * rel.astype(F32)

    def block_logits(ki):
        s = _qk_logits(k_ref[pl.ds(pl.multiple_of(ki * tq, tq), tq), :], q_ref, group)
        return [[s[h][c * MOBA_BLOCK:(c + 1) * MOBA_BLOCK] * (SCALE * LOG2E) - alibi_sc[h] for c in range(per_tile)]
                for h in range(group)]

    def block_bias(h, kb):
        return (slopes_ref[kvh * group + h] * LOG2E) * ((qi * tq - kb * MOBA_BLOCK).astype(F32))

    def past_tile(ki, carry):
        u = block_logits(ki)
        parts = [[(u[h][c], selb_sc[h, pl.ds(ki * per_tile + c, 1), :] - block_bias(h, ki * per_tile + c))
                  for c in range(per_tile)] for h in range(group)]
        _softmax_tile_t(parts, v_ref[pl.ds(pl.multiple_of(ki * tq, tq), tq), :], m_sc, l_sc, acc_sc)
        return carry

    lax.fori_loop(0, qi, past_tile, 0)

    u = block_logits(qi)
    parts = []
    for h in range(group):
        parts.append([])
        for c in range(per_tile):
            kb = qi * per_tile + c
            rb = jnp.where(own > kb, selb_sc[h, pl.ds(kb, 1), :], 0.0) - block_bias(h, kb)
            parts[h].append((jnp.where(rel >= c * MOBA_BLOCK, u[h][c], NEG_INF), rb))
    _softmax_tile_t(parts, v_ref[pl.ds(pl.multiple_of(qi * tq, tq), tq), :], m_sc, l_sc, acc_sc)
    _write_heads(o_ref, l_sc, acc_sc, group)


def _moba_prompt(q16, q_col0, k16, v16, kmean, slopes, batch, seq, n_heads, n_kv):
    group = n_heads // n_kv
    tq = _row_tile(seq, ATTN_TILE)
    assert tq % MOBA_BLOCK == 0
    nq = seq // tq
    nblk = seq // MOBA_BLOCK
    gw = group * HEAD_DIM
    qb0 = q_col0 // gw
    return pl.pallas_call(
        functools.partial(_moba_prompt_kernel, group=group, nblk=nblk),
        grid=(batch, n_kv, nq),
        in_specs=[
            pl.BlockSpec(memory_space=pltpu.SMEM),
            pl.BlockSpec((tq, gw), lambda b, j, i: (b * nq + i, qb0 + j)),
            pl.BlockSpec((seq, HEAD_DIM), lambda b, j, i: (b, j)),
            pl.BlockSpec((seq, HEAD_DIM), lambda b, j, i: (b, j)),
            pl.BlockSpec((1, 1, LANES, HEAD_DIM), lambda b, j, i: (b, j, 0, 0)),
        ],
        out_specs=pl.BlockSpec((tq, gw), lambda b, j, i: (b * nq + i, j)),
        out_shape=jax.ShapeDtypeStruct((batch * seq, n_heads * HEAD_DIM), F32),
        scratch_shapes=[
            pltpu.VMEM((group, 1, tq), F32),
            pltpu.VMEM((group, 1, tq), F32),
            pltpu.VMEM((group, HEAD_DIM, tq), F32),
            pltpu.VMEM((group, nblk, tq), F32),
            pltpu.VMEM((group, MOBA_BLOCK, tq), F32),
        ],
        compiler_params=_params("parallel", "parallel", "arbitrary"),
        name="moba_prompt",
    )(slopes, q16, k16, v16, kmean)


def _fox_prompt_kernel(q_ref, k_ref, v_ref, ccol_ref, crow_ref, o_ref, m_sc, l_sc, acc_sc, ckb_sc, *, group):
    tq = q_ref.shape[0]
    qi = pl.program_id(2)
    _init_softmax_state(m_sc, l_sc, acc_sc)

    @pl.when(qi == 0)
    def _():
        for h in range(group):
            ckb_sc[h] = jnp.broadcast_to(ccol_ref[0, :, h:h + 1] * LOG2E, ckb_sc.shape[1:])

    def tile_logits(ki):
        off = pl.multiple_of(ki * tq, tq)
        s = _qk_logits(k_ref[pl.ds(off, tq), :], q_ref, group)
        return [s[h] * (SCALE * LOG2E) - jnp.concatenate([ckb_sc[h, pl.ds(off, tq), :]] * (tq // LANES), axis=1)
                for h in range(group)]

    def cq(h):
        return crow_ref[0, 0, h:h + 1, pl.ds(pl.multiple_of(qi * tq, tq), tq)] * LOG2E

    def past_tile(ki, carry):
        u = tile_logits(ki)
        _softmax_tile_t([[(u[h], cq(h))] for h in range(group)], v_ref[pl.ds(pl.multiple_of(ki * tq, tq), tq), :],
                        m_sc, l_sc, acc_sc)
        return carry

    lax.fori_loop(0, qi, past_tile, 0)

    krow = lax.broadcasted_iota(jnp.int32, (tq, tq), 0)
    qcol = lax.broadcasted_iota(jnp.int32, (tq, tq), 1)
    u = tile_logits(qi)
    _softmax_tile_t([[(jnp.where(qcol >= krow, u[h], NEG_INF), cq(h))] for h in range(group)],
                    v_ref[pl.ds(pl.multiple_of(qi * tq, tq), tq), :], m_sc, l_sc, acc_sc)
    _write_heads(o_ref, l_sc, acc_sc, group)


def _fox_prompt(q16, q_col0, k16, v16, c_col, c_row, batch, seq, n_heads, n_kv):
    group = n_heads // n_kv
    tq = _row_tile(seq, ATTN_TILE)
    nq = seq // tq
    gw = group * HEAD_DIM
    qb0 = q_col0 // gw
    return pl.pallas_call(
        functools.partial(_fox_prompt_kernel, group=group),
        grid=(batch, n_kv, nq),
        in_specs=[
            pl.BlockSpec((tq, gw), lambda b, j, i: (b * nq + i, qb0 + j)),
            pl.BlockSpec((seq, HEAD_DIM), lambda b, j, i: (b, j)),
            pl.BlockSpec((seq, HEAD_DIM), lambda b, j, i: (b, j)),
            pl.BlockSpec((1, seq, group), lambda b, j, i: (j, b, 0)),
            pl.BlockSpec((1, 1, group, seq), lambda b, j, i: (b, j, 0, 0)),
        ],
        out_specs=pl.BlockSpec((tq, gw), lambda b, j, i: (b * nq + i, j)),
        out_shape=jax.ShapeDtypeStruct((batch * seq, n_heads * HEAD_DIM), F32),
        scratch_shapes=[
            pltpu.VMEM((group, 1, tq), F32),
            pltpu.VMEM((group, 1, tq), F32),
            pltpu.VMEM((group, HEAD_DIM, tq), F32),
            pltpu.VMEM((group, seq, LANES), F32),
        ],
        compiler_params=_params("arbitrary", "arbitrary", "arbitrary"),
        name="fox_prompt",
    )(q16, k16, v16, c_col, c_row)


def _cumsum_kernel(x_ref, o_ref, *, chunk):
    rows, seq = x_ref.shape[1:]
    r = lax.broadcasted_iota(jnp.int32, (chunk, chunk), 0)
    c = lax.broadcasted_iota(jnp.int32, (chunk, chunk), 1)
    upper = (r <= c).astype(F32)
    carry = jnp.zeros((rows, 1), F32)
    for i in range(seq // chunk):
        sl = slice(i * chunk, (i + 1) * chunk)
        cs = jnp.dot(x_ref[0, :, sl], upper, precision=HI, preferred_element_type=F32) + carry
        o_ref[0, :, sl] = cs
        carry = cs[:, chunk - 1:chunk]


def _cumsum_rows(x):
    batch, rows, seq = x.shape
    chunk = 256
    assert seq % chunk == 0
    return pl.pallas_call(
        functools.partial(_cumsum_kernel, chunk=chunk),
        grid=(batch,),
        in_specs=[pl.BlockSpec((1, rows, seq), lambda b: (b, 0, 0))],
        out_specs=pl.BlockSpec((1, rows, seq), lambda b: (b, 0, 0)),
        out_shape=jax.ShapeDtypeStruct(x.shape, F32),
        compiler_params=_params("parallel"),
        name="cumsum",
    )(x)


def _page_copies(pt_ref, row, slot, srcs, bufs, sems, n_pages, page):
    copies = []
    for p in range(n_pages):
        idx = pt_ref[row * n_pages + p]
        rows = pl.ds(p * page, page)
        for a, ((src, lead), buf) in enumerate(zip(srcs, bufs)):
            src_page = src.at[lead + (idx,)]
            if len(buf.shape) == 4:
                for j in range(buf.shape[1]):
                    copies.append(pltpu.make_async_copy(src_page.at[:, j, :], buf.at[slot, j, rows, :], sems.at[a, slot]))
            else:
                copies.append(pltpu.make_async_copy(src_page, buf.at[slot, rows], sems.at[a, slot]))
    return copies


def _fetch_rows(pt_ref, srcs, bufs, sems, n_pages, page):
    b = pl.program_id(0)
    nb = pl.num_programs(0)
    slot = lax.rem(b, 2)

    @pl.when(b == 0)
    def _():
        for c in _page_copies(pt_ref, b, slot, srcs, bufs, sems, n_pages, page):
            c.start()

    @pl.when(b + 1 < nb)
    def _():
        for c in _page_copies(pt_ref, b + 1, 1 - slot, srcs, bufs, sems, n_pages, page):
            c.start()

    for c in _page_copies(pt_ref, b, slot, srcs, bufs, sems, n_pages, page):
        c.wait()
    return slot


def _append_new_row(buf, slot, past, new):
    first = lax.broadcasted_iota(jnp.int32, (SUBLANES, HEAD_DIM), 0) == 0
    for j in range(new.shape[0]):
        buf[slot, j, past:past + SUBLANES, :] = jnp.where(first, new[j:j + 1, :], 0.0)


def _tail_bias(heads):
    first = lax.broadcasted_iota(jnp.int32, (SUBLANES, heads), 0) == 0
    return jnp.where(first, 0.0, NEG_INF)


def _group_queries(q, j, group):
    r = lax.broadcasted_iota(jnp.int32, q.shape, 0)
    return jnp.where(r // group == j, q, jnp.zeros_like(q))


def _decode_scores(kb, q, n_kv):
    group = q.shape[0] // n_kv
    s = None
    for j0 in range(0, n_kv, 2):
        js = range(j0, min(j0 + 2, n_kv))
        k2 = jnp.concatenate([kb[j].astype(BF16) for j in js], axis=1)
        q2 = jnp.concatenate([_group_queries(q, j, group) for j in js], axis=1)
        sj = lax.dot_general(k2, q2, NT_DIMS, preferred_element_type=F32)
        s = sj if s is None else s + sj
    return s


def _decode_softmax_pv(s, vb, n_kv):
    heads = s.shape[1]
    group = heads // n_kv
    m = jnp.max(s, axis=0, keepdims=True)
    p = jnp.exp(s - m)
    l = jnp.sum(p, axis=0, keepdims=True)
    p16 = p.astype(BF16)
    o_t = jnp.concatenate(
        [lax.dot_general(vb[j].astype(BF16), p16, TN_DIMS, preferred_element_type=F32) for j in range(n_kv)],
        axis=0) / l
    o = o_t.T
    return jnp.concatenate(
        [o[h:h + 1, (h // group) * HEAD_DIM:(h // group + 1) * HEAD_DIM] for h in range(heads)], axis=0)


def _moba_decode_kernel(pt_ref, q_ref, knew_ref, vnew_ref, alibi_ref, k_hbm, v_hbm, o_ref, kbuf, vbuf, sems,
                        *, layer, n_kv, n_pages, page):
    slot = _fetch_rows(pt_ref, ((k_hbm, (layer,)), (v_hbm, (layer,))), (kbuf, vbuf), sems, n_pages, page)
    past = n_pages * page
    nblk = past // MOBA_BLOCK
    _append_new_row(kbuf, slot, past, knew_ref[0])
    _append_new_row(vbuf, slot, past, vnew_ref[0])
    kb, vb = kbuf.at[slot], vbuf.at[slot]
    q = q_ref[0]
    heads = q.shape[0]
    group = heads // n_kv

    g = None
    for j in range(n_kv):
        means = jnp.concatenate(
            [jnp.sum(kb[j, n * MOBA_BLOCK:(n + 1) * MOBA_BLOCK, :], axis=0, keepdims=True) for n in range(nblk)],
            axis=0) * (1.0 / MOBA_BLOCK)
        gj = lax.dot_general(means, _group_queries(q, j, group).astype(F32), NT_DIMS, precision=HI,
                             preferred_element_type=F32)
        g = gj if g is None else g + gj
    blk = lax.broadcasted_iota(jnp.int32, g.shape, 0)
    cnt = jnp.zeros(g.shape, jnp.int32)
    for m in range(nblk):
        gm = g[m:m + 1, :]
        cnt = cnt + ((gm > g) | ((gm == g) & (blk > m))).astype(jnp.int32)
    sel = cnt < MOBA_TOPK
    keep = jnp.concatenate([jnp.broadcast_to(sel[n:n + 1, :], (MOBA_BLOCK, heads)) for n in range(nblk)]
                           + [jnp.ones((SUBLANES, heads), jnp.bool_)], axis=0)
    s = jnp.where(keep, _decode_scores(kb, q, n_kv) * SCALE + alibi_ref[...], NEG_INF)
    o_ref[0] = _decode_softmax_pv(s, vb, n_kv).astype(o_ref.dtype)


def _fox_decode_kernel(pt_ref, q_ref, knew_ref, vnew_ref, fnew_ref, k_hbm, v_hbm, c_hbm, o_ref, kbuf, vbuf, cbuf, sems,
                       *, layer, n_kv, n_pages, page):
    slot = _fetch_rows(pt_ref, ((k_hbm, (layer,)), (v_hbm, (layer,)), (c_hbm, ())), (kbuf, vbuf, cbuf), sems,
                       n_pages, page)
    past = n_pages * page
    _append_new_row(kbuf, slot, past, knew_ref[0])
    _append_new_row(vbuf, slot, past, vnew_ref[0])
    kb, vb = kbuf.at[slot], vbuf.at[slot]
    q = q_ref[0]
    heads = q.shape[0]

    carry = jnp.zeros((1, heads), F32)
    cums = []
    for j in range(n_pages):
        cs = cbuf[slot, j * page:(j + 1) * page, :] + carry
        cums.append(cs)
        carry = cs[page - 1:page, :]
    c_now = carry + fnew_ref[0]
    bias = [c_now - cs for cs in cums]
    s = _decode_scores(kb, q, n_kv) * SCALE + jnp.concatenate(bias + [_tail_bias(heads)], axis=0)
    o_ref[0] = _decode_softmax_pv(s, vb, n_kv).astype(o_ref.dtype)


def _decode_attention(body, layer, page_table, q16, k_new, v_new, extra, extra_per_row, caches, n_kv):
    rows, n_pages = page_table.shape
    heads = q16.shape[1] // HEAD_DIM
    page = caches[0].shape[2]
    past = n_pages * page
    assert past % MOBA_BLOCK == 0
    row_spec = lambda shape: pl.BlockSpec((1,) + shape, lambda b, pt: (b, 0, 0))
    in_specs = [row_spec((heads, HEAD_DIM)), row_spec((n_kv, HEAD_DIM)), row_spec((n_kv, HEAD_DIM))]
    if extra_per_row:
        in_specs.append(row_spec((1, heads)))
        extra = extra.reshape(rows, 1, heads)
    else:
        in_specs.append(pl.BlockSpec(extra.shape, lambda b, pt: (0, 0)))
    in_specs += [pl.BlockSpec(memory_space=pl.ANY)] * len(caches)
    scratch = [pltpu.VMEM((2, n_kv, past + SUBLANES, HEAD_DIM) if c.ndim == 5 else (2, past, c.shape[2]), F32)
               for c in caches]
    scratch.append(pltpu.SemaphoreType.DMA((len(caches), 2)))
    return pl.pallas_call(
        functools.partial(body, layer=layer, n_kv=n_kv, n_pages=n_pages, page=page),
        grid_spec=pltpu.PrefetchScalarGridSpec(
            num_scalar_prefetch=1,
            grid=(rows,),
            in_specs=in_specs,
            out_specs=pl.BlockSpec((1, heads, HEAD_DIM), lambda b, pt: (b, 0, 0)),
            scratch_shapes=scratch,
        ),
        out_shape=jax.ShapeDtypeStruct((rows, heads, HEAD_DIM), F32),
        compiler_params=_params("arbitrary"),
        name="decode_attention",
    )(page_table.reshape(-1), q16.reshape(rows, heads, HEAD_DIM), k_new, v_new, extra, *caches
      ).reshape(rows, heads * HEAD_DIM)


def _page_cumsum_kernel(x_ref, o_ref):
    page = x_ref.shape[1]
    r = lax.broadcasted_iota(jnp.int32, (page, page), 0)
    c = lax.broadcasted_iota(jnp.int32, (page, page), 1)
    o_ref[...] = jnp.dot(x_ref[...], (r <= c).astype(F32), precision=HI, preferred_element_type=F32)


def _page_cumsum(logf_cache):
    pool, page, heads = logf_cache.shape
    x = logf_cache.transpose(0, 2, 1).reshape(pool * heads, page)
    tm = _row_tile(pool * heads, 2048)
    out = pl.pallas_call(
        _page_cumsum_kernel,
        grid=(pool * heads // tm,),
        in_specs=[pl.BlockSpec((tm, page), lambda i: (i, 0))],
        out_specs=pl.BlockSpec((tm, page), lambda i: (i, 0)),
        out_shape=jax.ShapeDtypeStruct((pool * heads, page), F32),
        compiler_params=_params("parallel"),
        name="page_cumsum",
    )(x)
    return out.reshape(pool, heads, page).transpose(0, 2, 1)


def _merge_kernel(x_ref, oa_ref, ob_ref, z_ref, g_ref, wa_ref, wb_ref, wo_ref, y_ref):
    wa_w = oa_ref.shape[1]
    d = x_ref.shape[1]

    def branch(o_ref, z, w_ref):
        u = o_ref[...] * (z * jax.nn.sigmoid(z))
        return jnp.dot(u.astype(BF16), w_ref[...], preferred_element_type=F32)

    ya = branch(oa_ref, z_ref[:, :wa_w], wa_ref)
    yb = branch(ob_ref, z_ref[:, wa_w:], wb_ref)
    mixed = g_ref[:, :d] * ya + g_ref[:, d:] * yb
    y_ref[...] = x_ref[...] + jnp.dot(mixed.astype(BF16), wo_ref[...], preferred_element_type=F32)


def _merge(x, oa, ob, z, gates, wa, wb, wo):
    m, d = x.shape
    tm = _row_tile(m, 256)
    rows = lambda width: pl.BlockSpec((tm, width), lambda i: (i, 0))
    resident = lambda w: pl.BlockSpec(w.shape, lambda i: (0, 0), pipeline_mode=pl.Buffered(1))
    return pl.pallas_call(
        _merge_kernel,
        grid=(m // tm,),
        in_specs=[rows(d), rows(oa.shape[1]), rows(ob.shape[1]), rows(z.shape[1]), rows(gates.shape[1]),
                  resident(wa), resident(wb), resident(wo)],
        out_specs=rows(d),
        out_shape=jax.ShapeDtypeStruct((m, d), F32),
        compiler_params=_params("parallel"),
        name="merge",
    )(x, oa, ob, z, gates, wa, wb, wo)


def _project(x2d, g_norm, wts, n_kv_a, n_kv_b, n_heads_b):
    h = _rmsnorm(x2d, g_norm)
    kva, kvb = n_kv_a * HEAD_DIM, n_kv_b * HEAD_DIM
    (q16,) = _proj(_proj_q_kernel, h, wts["w_q"], wts["g_q"], [(BF16, (wts["w_q"].shape[1],))])
    ka, va, ka16, va16 = _proj(functools.partial(_proj_kv_kernel, n_kv=n_kv_a, n_forget=0), h, wts["w_kva"],
                               wts["p_kva"], [(F32, (n_kv_a, HEAD_DIM)), (F32, (n_kv_a, HEAD_DIM)),
                                              (BF16, (kva,)), (BF16, (kva,))])
    kb, vb, kb16, vb16, logf = _proj(functools.partial(_proj_kv_kernel, n_kv=n_kv_b, n_forget=n_heads_b), h,
                                     wts["w_kvb"], wts["p_kvb"],
                                     [(F32, (n_kv_b, HEAD_DIM)), (F32, (n_kv_b, HEAD_DIM)), (BF16, (kvb,)),
                                      (BF16, (kvb,)), (F32, (n_heads_b,))])
    (z,) = _proj(_proj_plain_kernel, h, wts["w_z"], None, [(F32, (wts["w_z"].shape[1],))])
    d = x2d.shape[1]
    (gates,) = _proj(_proj_gate_kernel, h, wts["w_g"], wts["b_g"], [(F32, (2 * d,))], tn=d)
    return dict(q16=q16, ka=ka, va=va, ka16=ka16, va16=va16, kb=kb, vb=vb, kb16=kb16, vb16=vb16, logf=logf, z=z,
                gates=gates)


def _layer_weights(w_in, b_forget, b_merge, gq_a, gk_a, gq_b, gk_b, wa, wb, wo, dims):
    d, wa_w, wb_w, kva, kvb, hb = dims
    offs = [0]
    for wdt in (wa_w, kva, kva, wa_w, wb_w, kvb, kvb, wb_w, hb, d, d):
        offs.append(offs[-1] + wdt)
    col = lambda i, j=None: w_in[:, offs[i]:offs[i + 1 if j is None else j]]
    f_pad = LANES - hb
    return dict(
        w_q=jnp.concatenate([col(0), col(4)], axis=1).astype(BF16),
        g_q=jnp.concatenate([jnp.tile(gq_a, wa_w // HEAD_DIM), jnp.tile(gq_b, wb_w // HEAD_DIM)]).reshape(1, -1),
        w_kva=col(1, 3).astype(BF16),
        p_kva=jnp.concatenate([jnp.tile(gk_a, kva // HEAD_DIM), jnp.zeros((kva,), F32)]).reshape(1, -1),
        w_kvb=jnp.concatenate([col(5, 7), col(8), jnp.zeros((d, f_pad), F32)], axis=1).astype(BF16),
        p_kvb=jnp.concatenate([jnp.tile(gk_b, kvb // HEAD_DIM), jnp.zeros((kvb,), F32), b_forget,
                               jnp.zeros((f_pad,), F32)]).reshape(1, -1),
        w_z=jnp.concatenate([col(3), col(7)], axis=1).astype(BF16),
        w_g=col(9, 11).astype(BF16),
        b_g=b_merge.reshape(1, 2 * d),
        wa=wa.astype(BF16), wb=wb.astype(BF16), wo=wo.astype(BF16),
    )


def kernel(x_prompt, x_sample, cache_k_moba, cache_v_moba, cache_k_fox, cache_v_fox, cache_logf_fox, page_table,
           g_norm, w_in, b_forget, b_merge, gq_moba, gk_moba, gq_fox, gk_fox, w_branch_moba, w_branch_fox, w_out):
    depth = w_in.shape[0]
    batch, seq, d = x_prompt.shape
    rows, dec_seq, _ = x_sample.shape
    assert dec_seq == 1 and seq % MOBA_BLOCK == 0
    n_kv_a, n_kv_b = cache_k_moba.shape[3], cache_k_fox.shape[3]
    hb = b_forget.shape[1]
    wa_w, wb_w = w_branch_moba.shape[1], w_branch_fox.shape[1]
    ha = wa_w // HEAD_DIM
    assert wb_w == hb * HEAD_DIM
    kva, kvb = n_kv_a * HEAD_DIM, n_kv_b * HEAD_DIM
    slopes = 2.0 ** (-8.0 * jnp.arange(1, ha + 1, dtype=F32) / ha)
    past = page_table.shape[1] * cache_k_moba.shape[2]
    pos = jnp.arange(past + SUBLANES)[:, None]
    alibi_decode = jnp.where(pos < past, -slopes[None, :] * (past - pos).astype(F32),
                             jnp.where(pos == past, 0.0, NEG_INF))

    xp = x_prompt.reshape(batch * seq, d)
    xs = x_sample.reshape(rows, d)
    outs_p, outs_s = [], []
    for l in range(depth):
        wts = _layer_weights(w_in[l], b_forget[l], b_merge[l], gq_moba[l], gk_moba[l], gq_fox[l], gk_fox[l],
                             w_branch_moba[l], w_branch_fox[l], w_out[l], (d, wa_w, wb_w, kva, kvb, hb))
        pp = _project(xp, g_norm[l], wts, n_kv_a, n_kv_b, hb)
        kmean = _kmean(pp["ka"], batch, seq)
        oa = _moba_prompt(pp["q16"], 0, pp["ka16"], pp["va16"], kmean, slopes, batch, seq, ha, n_kv_a)
        group_b = hb // n_kv_b
        c_row = _cumsum_rows(pp["logf"].reshape(batch, seq, hb).transpose(0, 2, 1))
        c_col = c_row.reshape(batch, n_kv_b, group_b, seq).transpose(1, 0, 3, 2).reshape(n_kv_b, batch * seq, group_b)
        ob = _fox_prompt(pp["q16"], wa_w, pp["kb16"], pp["vb16"], c_col,
                         c_row.reshape(batch, n_kv_b, group_b, seq), batch, seq, hb, n_kv_b)
        xp = _merge(xp, oa, ob, pp["z"], pp["gates"], wts["wa"], wts["wb"], wts["wo"])
        outs_p.append((pp["ka"], pp["va"], pp["kb"], pp["vb"], pp["logf"]))
        ps = _project(xs, g_norm[l], wts, n_kv_a, n_kv_b, hb)
        oa_s = _decode_attention(_moba_decode_kernel, l, page_table, ps["q16"][:, :wa_w], ps["ka"], ps["va"],
                                 alibi_decode, False, [cache_k_moba, cache_v_moba], n_kv_a)
        ob_s = _decode_attention(_fox_decode_kernel, l, page_table, ps["q16"][:, wa_w:], ps["kb"], ps["vb"],
                                 ps["logf"], True, [cache_k_fox, cache_v_fox, _page_cumsum(cache_logf_fox[l])], n_kv_b)
        xs = _merge(xs, oa_s, ob_s, ps["z"], ps["gates"], wts["wa"], wts["wb"], wts["wo"])
        outs_s.append((ps["ka"], ps["va"], ps["kb"], ps["vb"], ps["logf"]))

    def stacked(outs, i, shape):
        return jnp.stack([o[i].reshape(shape) for o in outs])

    res = [xp.reshape(batch, seq, d), xs.reshape(rows, 1, d)]
    for outs, lead in ((outs_p, (batch, seq)), (outs_s, (rows, 1))):
        res += [stacked(outs, 0, lead + (n_kv_a, HEAD_DIM)), stacked(outs, 1, lead + (n_kv_a, HEAD_DIM)),
                stacked(outs, 2, lead + (n_kv_b, HEAD_DIM)), stacked(outs, 3, lead + (n_kv_b, HEAD_DIM)),
                stacked(outs, 4, lead + (hb,))]
    return tuple(res)
```

```python
import functools

import jax
import jax.numpy as jnp
from jax import lax
from jax.experimental import pallas as pl
from jax.experimental.pallas import tpu as pltpu

HEAD_DIM = 128
MOBA_BLOCK = 256
MOBA_TOPK = 3
NORM_EPS = 1e-6
NEG_INF = -1e30
M_INIT = 0.5 * NEG_INF
SCALE = HEAD_DIM ** -0.5
LOG2E = 1.4426950408889634
LANES = 128
SUBLANES = 8
VMEM_LIMIT = 52 * 1024 * 1024
ATTN_TILE = 512

F32 = jnp.float32
BF16 = jnp.bfloat16
HI = lax.Precision.HIGHEST
NT_DIMS = (((1,), (1,)), ((), ()))
TN_DIMS = (((0,), (0,)), ((), ()))


def _params(*sem):
    return pltpu.CompilerParams(dimension_semantics=sem, vmem_limit_bytes=VMEM_LIMIT)


def _row_tile(m, want):
    t = min(m, want)
    assert m % t == 0
    return t


def _rmsnorm_kernel(x_ref, g_ref, h_ref):
    x = x_ref[...]
    ms = jnp.mean(x * x, axis=-1, keepdims=True)
    h_ref[...] = (x * lax.rsqrt(ms + NORM_EPS) * g_ref[...]).astype(h_ref.dtype)


def _rmsnorm(x, g):
    m, d = x.shape
    tm = _row_tile(m, 512)
    return pl.pallas_call(
        _rmsnorm_kernel,
        grid=(m // tm,),
        in_specs=[pl.BlockSpec((tm, d), lambda i: (i, 0)), pl.BlockSpec((1, d), lambda i: (0, 0))],
        out_specs=pl.BlockSpec((tm, d), lambda i: (i, 0)),
        out_shape=jax.ShapeDtypeStruct((m, d), BF16),
        compiler_params=_params("parallel"),
        name="rmsnorm",
    )(x, g.reshape(1, d))


def _head_norm(a, g):
    ms = jnp.mean(a * a, axis=-1, keepdims=True)
    return a * lax.rsqrt(ms + NORM_EPS) * g


def _log_sigmoid(x):
    return jnp.minimum(x, 0.0) - jnp.log1p(jnp.exp(-jnp.abs(x)))


def _proj_q_kernel(h_ref, w_ref, g_ref, o_ref):
    acc = jnp.dot(h_ref[...], w_ref[...], preferred_element_type=F32)
    for j in range(acc.shape[1] // HEAD_DIM):
        sl = slice(j * HEAD_DIM, (j + 1) * HEAD_DIM)
        o_ref[:, sl] = (_head_norm(acc[:, sl], g_ref[:, sl]) * (SCALE * LOG2E)).astype(o_ref.dtype)


def _proj_kv_kernel(h_ref, w_ref, g_ref, *refs, n_kv, n_forget):
    if n_forget:
        wf_ref, bf_ref, *refs = refs
    k32_ref, v32_ref, k16_ref, v16_ref = refs[:4]
    kvw = n_kv * HEAD_DIM
    h = h_ref[...]
    acc = jnp.dot(h, w_ref[...], preferred_element_type=F32)
    for j in range(n_kv):
        sl = slice(j * HEAD_DIM, (j + 1) * HEAD_DIM)
        kn = _head_norm(acc[:, sl], g_ref[:, sl])
        k32_ref[:, j, :] = kn
        k16_ref[:, sl] = kn.astype(BF16)
        v = acc[:, kvw + j * HEAD_DIM:kvw + (j + 1) * HEAD_DIM]
        v32_ref[:, j, :] = v
        v16_ref[:, sl] = v.astype(BF16)
    if n_forget:
        f = jnp.dot(h, wf_ref[...].astype(BF16), preferred_element_type=F32)
        refs[4][...] = _log_sigmoid(f[:, :n_forget] + bf_ref[:, :n_forget])


def _proj_plain_kernel(h_ref, w_ref, o_ref):
    o_ref[...] = jnp.dot(h_ref[...], w_ref[...], preferred_element_type=F32).astype(o_ref.dtype)


def _proj_gate_kernel(h_ref, w_ref, b_ref, o_ref):
    acc = jnp.dot(h_ref[...], w_ref[...], preferred_element_type=F32)
    o_ref[...] = jax.nn.sigmoid(acc + b_ref[...]).astype(o_ref.dtype)


def _with_cached_weights(body, h_ref, w_ref, *refs):
    *refs, w16_sc = refs

    @pl.when(pl.program_id(1) == 0)
    def _():
        w16_sc[...] = w_ref[...].astype(BF16)

    body(h_ref, w16_sc, *refs)


def _proj(body, h, w, cols, tn, extras, outs):
    m, d = h.shape
    first, stride, count = cols
    tm = _row_tile(m, 512)
    in_specs = [pl.BlockSpec((tm, d), lambda j, i: (i, 0)),
                pl.BlockSpec((d, tn), lambda j, i: (0, first + stride * j))]
    args = [h, w]
    for arr, width, base in extras:
        in_specs.append(pl.BlockSpec((arr.shape[0], width), lambda j, i, base=base: (0, base + j)))
        args.append(arr)
    out_specs, out_shape = [], []
    for dt, trail in outs:
        if len(trail) == 2:
            assert count == 1
            out_specs.append(pl.BlockSpec((tm,) + trail, lambda j, i: (i, 0, 0)))
        else:
            out_specs.append(pl.BlockSpec((tm, trail[0] // count), lambda j, i: (i, j)))
        out_shape.append(jax.ShapeDtypeStruct((m,) + trail, dt))
    cast = w.dtype != BF16
    return pl.pallas_call(
        functools.partial(_with_cached_weights, body) if cast else body,
        grid=(count, m // tm),
        in_specs=in_specs,
        out_specs=out_specs,
        out_shape=out_shape,
        scratch_shapes=[pltpu.VMEM((d, tn), BF16)] if cast else [],
        compiler_params=_params("parallel", "arbitrary" if cast else "parallel"),
        name="proj",
    )(*args)


def _kmean_kernel(k_ref, o_ref, *, nblk, n_kv):
    o_ref[...] = jnp.zeros_like(o_ref)
    for j in range(n_kv):
        for n in range(nblk):
            blk = k_ref[n * MOBA_BLOCK:(n + 1) * MOBA_BLOCK, j, :]
            o_ref[0, j, n:n + 1, :] = jnp.sum(blk, axis=0, keepdims=True) * (1.0 / MOBA_BLOCK)


def _kmean(k32, batch, seq):
    n_kv = k32.shape[1]
    nblk = seq // MOBA_BLOCK
    assert nblk <= LANES
    return pl.pallas_call(
        functools.partial(_kmean_kernel, nblk=nblk, n_kv=n_kv),
        grid=(batch,),
        in_specs=[pl.BlockSpec((seq, n_kv, HEAD_DIM), lambda b: (b, 0, 0))],
        out_specs=pl.BlockSpec((1, n_kv, LANES, HEAD_DIM), lambda b: (b, 0, 0, 0)),
        out_shape=jax.ShapeDtypeStruct((batch, n_kv, LANES, HEAD_DIM), F32),
        compiler_params=_params("parallel"),
        name="kmean",
    )(k32)


def _split3(x):
    hi = x.astype(BF16).astype(F32)
    r = x - hi
    mid = r.astype(BF16).astype(F32)
    return hi, mid, (r - mid).astype(BF16).astype(F32)


def _store_wide_queries(qw_sc, q_ref, coef_rows):
    tq = q_ref.shape[0]
    for h, coef in enumerate(coef_rows):
        qw_sc[h * tq:(h + 1) * tq, :HEAD_DIM] = q_ref[:, h * HEAD_DIM:(h + 1) * HEAD_DIM]
        qw_sc[h * tq:(h + 1) * tq, HEAD_DIM:] = jnp.broadcast_to(coef.astype(BF16), (tq, LANES))


def _qk_logits(k, kx, qw_sc):
    return lax.dot_general(jnp.concatenate([k, kx], axis=1), qw_sc[...], NT_DIMS, preferred_element_type=F32)


def _pipelined_tiles(qi, s_refs, logits_into, process):
    s0, s1 = s_refs
    logits_into(s0, 0)

    def pair(kp, carry):
        t = 2 * kp
        logits_into(s1, t + 1)
        process(s0, t, False)
        logits_into(s0, t + 2)
        process(s1, t + 1, False)
        return carry

    lax.fori_loop(0, qi // 2, pair, 0)

    @pl.when(qi % 2 == 0)
    def _():
        process(s0, qi, True)

    @pl.when(qi % 2 == 1)
    def _():
        logits_into(s1, qi)
        process(s0, qi - 1, False)
        process(s1, qi, True)


def _softmax_tile_t(head_parts, v, m_sc, l_sc, acc_sc):
    heads = range(len(head_parts))
    tq = head_parts[0][0][0].shape[1]
    m_prev = [m_sc[h] for h in heads]
    m_new = []
    for h in heads:
        m = m_prev[h]
        for u, rb in head_parts[h]:
            m = jnp.maximum(m, jnp.max(u, axis=0, keepdims=True) + rb)
        m_new.append(m)
    alphas, p16s = [], []
    for h in heads:
        alpha = jnp.exp2(m_prev[h] - m_new[h])
        l_new = alpha * l_sc[h]
        ps = []
        for u, rb in head_parts[h]:
            p = jnp.exp2(u - (m_new[h] - rb))
            l_new = l_new + jnp.sum(p, axis=0, keepdims=True)
            ps.append(p.astype(v.dtype))
        p16s.append(ps[0] if len(ps) == 1 else jnp.concatenate(ps, axis=0))
        alphas.append(alpha)
        l_sc[h] = l_new
        m_sc[h] = m_new[h]
    pv = lax.dot_general(v, jnp.concatenate(p16s, axis=1), TN_DIMS, preferred_element_type=F32)
    for h in heads:
        acc_sc[h] = alphas[h] * acc_sc[h] + pv[:, h * tq:(h + 1) * tq]


def _init_softmax_state(m_sc, l_sc, acc_sc):
    m_sc[...] = jnp.full(m_sc.shape, M_INIT, F32)
    l_sc[...] = jnp.zeros(l_sc.shape, F32)
    acc_sc[...] = jnp.zeros(acc_sc.shape, F32)


def _write_heads(o_ref, l_sc, acc_sc, group):
    for h in range(group):
        o_ref[:, h * HEAD_DIM:(h + 1) * HEAD_DIM] = (acc_sc[h] / l_sc[h]).T.astype(o_ref.dtype)


def _moba_prompt_kernel(slopes_ref, q_ref, k_ref, v_ref, kmean_ref, o_ref, m_sc, l_sc, acc_sc, selb_sc, qw_sc, kx_sc,
                        s0_sc, s1_sc, *, group, nblk):
    tq = q_ref.shape[0]
    per_tile = tq // MOBA_BLOCK
    kvh = pl.program_id(1)
    qi = pl.program_id(2)
    _init_softmax_state(m_sc, l_sc, acc_sc)

    qcol = lax.broadcasted_iota(jnp.int32, (1, tq), 1)
    own = qi * per_tile + qcol // MOBA_BLOCK
    blk = lax.broadcasted_iota(jnp.int32, (nblk, tq), 0)
    km = kmean_ref[0, 0]
    for h in range(group):
        qh = q_ref[:, h * HEAD_DIM:(h + 1) * HEAD_DIM].astype(F32)
        g = lax.dot_general(km, qh, NT_DIMS, precision=HI, preferred_element_type=F32)[:nblk]
        cnt = jnp.zeros((nblk, tq), jnp.int32)
        for m in range(nblk):
            gm = g[m:m + 1, :]
            beats = (gm > g) | ((gm == g) & (blk > m))
            cnt = cnt + (beats & (own > m)).astype(jnp.int32)
        sel = (blk < own) & (cnt < MOBA_TOPK)
        selb_sc[h] = jnp.where(sel, 0.0, NEG_INF)

    slope2 = [slopes_ref[kvh * group + h] * LOG2E for h in range(group)]
    lane = lax.broadcasted_iota(jnp.int32, (1, LANES), 1)
    coefs = []
    for h in range(group):
        hi, mid, lo = _split3(jnp.zeros((1, LANES), F32) + slope2[h])
        coefs.append(jnp.where(lane < 6, jnp.where(lane % 3 == 0, hi, jnp.where(lane % 3 == 1, mid, lo)), 0.0))
    _store_wide_queries(qw_sc, q_ref, coefs)
    koff = lax.broadcasted_iota(jnp.int32, (tq, LANES), 0)
    klane = lax.broadcasted_iota(jnp.int32, (tq, LANES), 1)
    kx_sc[...] = jnp.where(klane < 3, (koff // MOBA_BLOCK) * MOBA_BLOCK,
                           jnp.where(klane < 6, koff % MOBA_BLOCK, 0)).astype(F32).astype(BF16)

    def logits_into(s_ref, ki):
        s_ref[...] = _qk_logits(k_ref[pl.ds(pl.multiple_of(ki * tq, tq), tq), :], kx_sc[...], qw_sc)

    def process(s_ref, ki, diagonal):
        parts = []
        for h in range(group):
            query_bias = -slope2[h] * (((qi - ki) * tq).astype(F32) + qcol.astype(F32))
            parts.append([])
            for c in range(per_tile):
                kb = ki * per_tile + c
                u = s_ref[c * MOBA_BLOCK:(c + 1) * MOBA_BLOCK, h * tq:(h + 1) * tq]
                selrow = selb_sc[h, pl.ds(kb, 1), :]
                if diagonal:
                    rel = (lax.broadcasted_iota(jnp.int32, (MOBA_BLOCK, tq), 1)
                           - lax.broadcasted_iota(jnp.int32, (MOBA_BLOCK, tq), 0))
                    u = jnp.where(rel >= c * MOBA_BLOCK, u, NEG_INF)
                    selrow = jnp.where(own > kb, selrow, 0.0)
                parts[h].append((u, selrow + query_bias))
        _softmax_tile_t(parts, v_ref[pl.ds(pl.multiple_of(ki * tq, tq), tq), :], m_sc, l_sc, acc_sc)

    _pipelined_tiles(qi, (s0_sc, s1_sc), logits_into, process)
    _write_heads(o_ref, l_sc, acc_sc, group)


def _moba_prompt(q16, q_col0, k16, v16, kmean, slopes, batch, seq, n_heads, n_kv):
    group = n_heads // n_kv
    tq = _row_tile(seq, ATTN_TILE)
    assert tq % MOBA_BLOCK == 0
    nq = seq // tq
    nblk = seq // MOBA_BLOCK
    gw = group * HEAD_DIM
    qb0 = q_col0 // gw
    return pl.pallas_call(
        functools.partial(_moba_prompt_kernel, group=group, nblk=nblk),
        grid=(batch, n_kv, nq),
        in_specs=[
            pl.BlockSpec(memory_space=pltpu.SMEM),
            pl.BlockSpec((tq, gw), lambda b, j, i: (b * nq + i, qb0 + j)),
            pl.BlockSpec((seq, HEAD_DIM), lambda b, j, i: (b, j)),
            pl.BlockSpec((seq, HEAD_DIM), lambda b, j, i: (b, j)),
            pl.BlockSpec((1, 1, LANES, HEAD_DIM), lambda b, j, i: (b, j, 0, 0)),
        ],
        out_specs=pl.BlockSpec((tq, gw), lambda b, j, i: (b * nq + i, j)),
        out_shape=jax.ShapeDtypeStruct((batch * seq, n_heads * HEAD_DIM), F32),
        scratch_shapes=[
            pltpu.VMEM((group, 1, tq), F32),
            pltpu.VMEM((group, 1, tq), F32),
            pltpu.VMEM((group, HEAD_DIM, tq), F32),
            pltpu.VMEM((group, nblk, tq), F32),
            pltpu.VMEM((group * tq, 2 * HEAD_DIM), BF16),
            pltpu.VMEM((tq, LANES), BF16),
            pltpu.VMEM((tq, group * tq), F32),
            pltpu.VMEM((tq, group * tq), F32),
        ],
        compiler_params=_params("parallel", "parallel", "arbitrary"),
        name="moba_prompt",
    )(slopes, q16, k16, v16, kmean)


def _fox_prompt_kernel(q_ref, k_ref, v_ref, ccol_ref, crow_ref, o_ref, m_sc, l_sc, acc_sc, qw_sc, kx_sc, s0_sc, s1_sc,
                       *, group):
    tq = q_ref.shape[0]
    seq = k_ref.shape[0]
    qi = pl.program_id(2)
    _init_softmax_state(m_sc, l_sc, acc_sc)
    lane = lax.broadcasted_iota(jnp.int32, (1, LANES), 1)
    _store_wide_queries(qw_sc, q_ref, [jnp.where(lane // 3 == h, -1.0, 0.0) for h in range(group)])

    @pl.when(qi == 0)
    def _():
        place = (lax.broadcasted_iota(jnp.int32, (group, LANES), 1) // 3
                 == lax.broadcasted_iota(jnp.int32, (group, LANES), 0)).astype(F32)
        x = jnp.dot(ccol_ref[0] * LOG2E, place, precision=HI, preferred_element_type=F32)
        hi, mid, lo = _split3(x)
        l3 = lax.broadcasted_iota(jnp.int32, (seq, LANES), 1) % 3
        kx_sc[...] = jnp.where(l3 == 0, hi, jnp.where(l3 == 1, mid, lo)).astype(BF16)

    def logits_into(s_ref, ki):
        off = pl.multiple_of(ki * tq, tq)
        s_ref[...] = _qk_logits(k_ref[pl.ds(off, tq), :], kx_sc[pl.ds(off, tq), :], qw_sc)

    def cq(h):
        return crow_ref[0, 0, h:h + 1, pl.ds(pl.multiple_of(qi * tq, tq), tq)] * LOG2E

    def process(s_ref, ki, diagonal):
        u = [s_ref[:, h * tq:(h + 1) * tq] for h in range(group)]
        if diagonal:
            krow = lax.broadcasted_iota(jnp.int32, (tq, tq), 0)
            qcol = lax.broadcasted_iota(jnp.int32, (tq, tq), 1)
            u = [jnp.where(qcol >= krow, uh, NEG_INF) for uh in u]
        _softmax_tile_t([[(u[h], cq(h))] for h in range(group)], v_ref[pl.ds(pl.multiple_of(ki * tq, tq), tq), :],
                        m_sc, l_sc, acc_sc)

    _pipelined_tiles(qi, (s0_sc, s1_sc), logits_into, process)
    _write_heads(o_ref, l_sc, acc_sc, group)


def _fox_prompt(q16, q_col0, k16, v16, c_col, c_row, batch, seq, n_heads, n_kv):
    group = n_heads // n_kv
    tq = _row_tile(seq, ATTN_TILE)
    nq = seq // tq
    gw = group * HEAD_DIM
    qb0 = q_col0 // gw
    return pl.pallas_call(
        functools.partial(_fox_prompt_kernel, group=group),
        grid=(batch, n_kv, nq),
        in_specs=[
            pl.BlockSpec((tq, gw), lambda b, j, i: (b * nq + i, qb0 + j)),
            pl.BlockSpec((seq, HEAD_DIM), lambda b, j, i: (b, j)),
            pl.BlockSpec((seq, HEAD_DIM), lambda b, j, i: (b, j)),
            pl.BlockSpec((1, seq, group), lambda b, j, i: (j, b, 0)),
            pl.BlockSpec((1, 1, group, seq), lambda b, j, i: (b, j, 0, 0)),
        ],
        out_specs=pl.BlockSpec((tq, gw), lambda b, j, i: (b * nq + i, j)),
        out_shape=jax.ShapeDtypeStruct((batch * seq, n_heads * HEAD_DIM), F32),
        scratch_shapes=[
            pltpu.VMEM((group, 1, tq), F32),
            pltpu.VMEM((group, 1, tq), F32),
            pltpu.VMEM((group, HEAD_DIM, tq), F32),
            pltpu.VMEM((group * tq, 2 * HEAD_DIM), BF16),
            pltpu.VMEM((seq, LANES), BF16),
            pltpu.VMEM((tq, group * tq), F32),
            pltpu.VMEM((tq, group * tq), F32),
        ],
        compiler_params=_params("arbitrary", "arbitrary", "arbitrary"),
        name="fox_prompt",
    )(q16, k16, v16, c_col, c_row)


def _cumsum_kernel(x_ref, o_ref, *, chunk):
    rows, seq = x_ref.shape[1:]
    r = lax.broadcasted_iota(jnp.int32, (chunk, chunk), 0)
    c = lax.broadcasted_iota(jnp.int32, (chunk, chunk), 1)
    upper = (r <= c).astype(F32)
    carry = jnp.zeros((rows, 1), F32)
    for i in range(seq // chunk):
        sl = slice(i * chunk, (i + 1) * chunk)
        cs = jnp.dot(x_ref[0, :, sl], upper, precision=HI, preferred_element_type=F32) + carry
        o_ref[0, :, sl] = cs
        carry = cs[:, chunk - 1:chunk]


def _cumsum_rows(x):
    batch, rows, seq = x.shape
    chunk = 256
    assert seq % chunk == 0
    return pl.pallas_call(
        functools.partial(_cumsum_kernel, chunk=chunk),
        grid=(batch,),
        in_specs=[pl.BlockSpec((1, rows, seq), lambda b: (b, 0, 0))],
        out_specs=pl.BlockSpec((1, rows, seq), lambda b: (b, 0, 0)),
        out_shape=jax.ShapeDtypeStruct(x.shape, F32),
        compiler_params=_params("parallel"),
        name="cumsum",
    )(x)


def _page_copies(pt_ref, row, slot, srcs, bufs, sems, n_pages, page):
    copies = []
    for p in range(n_pages):
        idx = pt_ref[row * n_pages + p]
        rows = pl.ds(p * page, page)
        for a, ((src, lead), buf) in enumerate(zip(srcs, bufs)):
            src_page = src.at[lead + (idx,)]
            if len(buf.shape) == 4:
                for j in range(buf.shape[1]):
                    copies.append(pltpu.make_async_copy(src_page.at[:, j, :], buf.at[slot, j, rows, :], sems.at[a, slot]))
            else:
                copies.append(pltpu.make_async_copy(src_page, buf.at[slot, rows], sems.at[a, slot]))
    return copies


def _fetch_rows(pt_ref, srcs, bufs, sems, n_pages, page):
    b = pl.program_id(0)
    nb = pl.num_programs(0)
    slot = lax.rem(b, 2)

    @pl.when(b == 0)
    def _():
        for c in _page_copies(pt_ref, b, slot, srcs, bufs, sems, n_pages, page):
            c.start()

    @pl.when(b + 1 < nb)
    def _():
        for c in _page_copies(pt_ref, b + 1, 1 - slot, srcs, bufs, sems, n_pages, page):
            c.start()

    for c in _page_copies(pt_ref, b, slot, srcs, bufs, sems, n_pages, page):
        c.wait()
    return slot


def _append_new_row(buf, slot, past, new):
    first = lax.broadcasted_iota(jnp.int32, (SUBLANES, HEAD_DIM), 0) == 0
    for j in range(new.shape[0]):
        buf[slot, j, past:past + SUBLANES, :] = jnp.where(first, new[j:j + 1, :], 0.0)


def _tail_bias(heads):
    first = lax.broadcasted_iota(jnp.int32, (SUBLANES, heads), 0) == 0
    return jnp.where(first, 0.0, NEG_INF)


def _group_queries(q, j, group):
    r = lax.broadcasted_iota(jnp.int32, q.shape, 0)
    return jnp.where(r // group == j, q, jnp.zeros_like(q))


def _decode_scores(kb, q, n_kv):
    group = q.shape[0] // n_kv
    s = None
    for j0 in range(0, n_kv, 2):
        js = range(j0, min(j0 + 2, n_kv))
        k2 = jnp.concatenate([kb[j].astype(BF16) for j in js], axis=1)
        q2 = jnp.concatenate([_group_queries(q, j, group) for j in js], axis=1)
        sj = lax.dot_general(k2, q2, NT_DIMS, preferred_element_type=F32)
        s = sj if s is None else s + sj
    return s


def _decode_softmax_pv(s, vb, n_kv):
    heads = s.shape[1]
    group = heads // n_kv
    m = jnp.max(s, axis=0, keepdims=True)
    p = jnp.exp2(s - m)
    l = jnp.sum(p, axis=0, keepdims=True)
    p16 = p.astype(BF16)
    o_t = jnp.concatenate(
        [lax.dot_general(vb[j].astype(BF16), p16, TN_DIMS, preferred_element_type=F32) for j in range(n_kv)],
        axis=0) / l
    o = o_t.T
    return jnp.concatenate(
        [o[h:h + 1, (h // group) * HEAD_DIM:(h // group + 1) * HEAD_DIM] for h in range(heads)], axis=0)


def _moba_decode_kernel(pt_ref, q_ref, knew_ref, vnew_ref, alibi_ref, k_hbm, v_hbm, o_ref, kbuf, vbuf, sems,
                        *, layer, n_kv, n_pages, page):
    slot = _fetch_rows(pt_ref, ((k_hbm, (layer,)), (v_hbm, (layer,))), (kbuf, vbuf), sems, n_pages, page)
    past = n_pages * page
    nblk = past // MOBA_BLOCK
    _append_new_row(kbuf, slot, past, knew_ref[0])
    _append_new_row(vbuf, slot, past, vnew_ref[0])
    kb, vb = kbuf.at[slot], vbuf.at[slot]
    q = q_ref[0]
    heads = q.shape[0]
    group = heads // n_kv

    g = None
    for j in range(n_kv):
        means = jnp.concatenate(
            [jnp.sum(kb[j, n * MOBA_BLOCK:(n + 1) * MOBA_BLOCK, :], axis=0, keepdims=True) for n in range(nblk)],
            axis=0) * (1.0 / MOBA_BLOCK)
        gj = lax.dot_general(means, _group_queries(q, j, group).astype(F32), NT_DIMS, precision=HI,
                             preferred_element_type=F32)
        g = gj if g is None else g + gj
    blk = lax.broadcasted_iota(jnp.int32, g.shape, 0)
    cnt = jnp.zeros(g.shape, jnp.int32)
    for m in range(nblk):
        gm = g[m:m + 1, :]
        cnt = cnt + ((gm > g) | ((gm == g) & (blk > m))).astype(jnp.int32)
    sel = cnt < MOBA_TOPK
    keep = jnp.concatenate([jnp.broadcast_to(sel[n:n + 1, :], (MOBA_BLOCK, heads)) for n in range(nblk)]
                           + [jnp.ones((SUBLANES, heads), jnp.bool_)], axis=0)
    s = jnp.where(keep, _decode_scores(kb, q, n_kv) + alibi_ref[...], NEG_INF)
    o_ref[0] = _decode_softmax_pv(s, vb, n_kv).astype(o_ref.dtype)


def _fox_decode_kernel(pt_ref, q_ref, knew_ref, vnew_ref, fnew_ref, k_hbm, v_hbm, c_hbm, o_ref, kbuf, vbuf, cbuf, sems,
                       *, layer, n_kv, n_pages, page):
    slot = _fetch_rows(pt_ref, ((k_hbm, (layer,)), (v_hbm, (layer,)), (c_hbm, ())), (kbuf, vbuf, cbuf), sems,
                       n_pages, page)
    past = n_pages * page
    _append_new_row(kbuf, slot, past, knew_ref[0])
    _append_new_row(vbuf, slot, past, vnew_ref[0])
    kb, vb = kbuf.at[slot], vbuf.at[slot]
    q = q_ref[0]
    heads = q.shape[0]

    carry = jnp.zeros((1, heads), F32)
    cums = []
    for j in range(n_pages):
        cs = cbuf[slot, j * page:(j + 1) * page, :] + carry
        cums.append(cs)
        carry = cs[page - 1:page, :]
    c_now = carry + fnew_ref[0]
    bias = [(c_now - cs) * LOG2E for cs in cums]
    s = _decode_scores(kb, q, n_kv) + jnp.concatenate(bias + [_tail_bias(heads)], axis=0)
    o_ref[0] = _decode_softmax_pv(s, vb, n_kv).astype(o_ref.dtype)


def _decode_attention(body, layer, page_table, q16, k_new, v_new, extra, extra_per_row, caches, n_kv):
    rows, n_pages = page_table.shape
    heads = q16.shape[1] // HEAD_DIM
    page = caches[0].shape[2]
    past = n_pages * page
    assert past % MOBA_BLOCK == 0
    row_spec = lambda shape: pl.BlockSpec((1,) + shape, lambda b, pt: (b, 0, 0))
    in_specs = [row_spec((heads, HEAD_DIM)), row_spec((n_kv, HEAD_DIM)), row_spec((n_kv, HEAD_DIM))]
    if extra_per_row:
        in_specs.append(row_spec((1, heads)))
        extra = extra.reshape(rows, 1, heads)
    else:
        in_specs.append(pl.BlockSpec(extra.shape, lambda b, pt: (0, 0)))
    in_specs += [pl.BlockSpec(memory_space=pl.ANY)] * len(caches)
    scratch = [pltpu.VMEM((2, n_kv, past + SUBLANES, HEAD_DIM) if c.ndim == 5 else (2, past, c.shape[2]), F32)
               for c in caches]
    scratch.append(pltpu.SemaphoreType.DMA((len(caches), 2)))
    return pl.pallas_call(
        functools.partial(body, layer=layer, n_kv=n_kv, n_pages=n_pages, page=page),
        grid_spec=pltpu.PrefetchScalarGridSpec(
            num_scalar_prefetch=1,
            grid=(rows,),
            in_specs=in_specs,
            out_specs=pl.BlockSpec((1, heads, HEAD_DIM), lambda b, pt: (b, 0, 0)),
            scratch_shapes=scratch,
        ),
        out_shape=jax.ShapeDtypeStruct((rows, heads, HEAD_DIM), F32),
        compiler_params=_params("arbitrary"),
        name="decode_attention",
    )(page_table.reshape(-1), q16.reshape(rows, heads, HEAD_DIM), k_new, v_new, extra, *caches
      ).reshape(rows, heads * HEAD_DIM)


def _page_cumsum_kernel(x_ref, o_ref):
    page = x_ref.shape[1]
    r = lax.broadcasted_iota(jnp.int32, (page, page), 0)
    c = lax.broadcasted_iota(jnp.int32, (page, page), 1)
    o_ref[...] = jnp.dot(x_ref[...], (r <= c).astype(F32), precision=HI, preferred_element_type=F32)


def _page_cumsum(logf_cache):
    pool, page, heads = logf_cache.shape
    x = logf_cache.transpose(0, 2, 1).reshape(pool * heads, page)
    tm = _row_tile(pool * heads, 2048)
    out = pl.pallas_call(
        _page_cumsum_kernel,
        grid=(pool * heads // tm,),
        in_specs=[pl.BlockSpec((tm, page), lambda i: (i, 0))],
        out_specs=pl.BlockSpec((tm, page), lambda i: (i, 0)),
        out_shape=jax.ShapeDtypeStruct((pool * heads, page), F32),
        compiler_params=_params("parallel"),
        name="page_cumsum",
    )(x)
    return out.reshape(pool, heads, page).transpose(0, 2, 1)


def _merge_kernel(x_ref, oa_ref, ob_ref, z_ref, g_ref, wa_ref, wb_ref, wo_ref, y_ref):
    wa_w = oa_ref.shape[1]
    d = x_ref.shape[1]

    def branch(o_ref, z, w_ref):
        u = o_ref[...] * (z * jax.nn.sigmoid(z))
        return jnp.dot(u.astype(BF16), w_ref[...], preferred_element_type=F32)

    ya = branch(oa_ref, z_ref[:, :wa_w], wa_ref)
    yb = branch(ob_ref, z_ref[:, wa_w:], wb_ref)
    mixed = g_ref[:, :d] * ya + g_ref[:, d:] * yb
    y_ref[...] = x_ref[...] + jnp.dot(mixed.astype(BF16), wo_ref[...], preferred_element_type=F32)


def _merge(x, oa, ob, z, gates, wa, wb, wo):
    m, d = x.shape
    tm = _row_tile(m, 256)
    rows = lambda width: pl.BlockSpec((tm, width), lambda i: (i, 0))
    resident = lambda w: pl.BlockSpec(w.shape, lambda i: (0, 0), pipeline_mode=pl.Buffered(1))
    return pl.pallas_call(
        _merge_kernel,
        grid=(m // tm,),
        in_specs=[rows(d), rows(oa.shape[1]), rows(ob.shape[1]), rows(z.shape[1]), rows(gates.shape[1]),
                  resident(wa), resident(wb), resident(wo)],
        out_specs=rows(d),
        out_shape=jax.ShapeDtypeStruct((m, d), F32),
        compiler_params=_params("parallel"),
        name="merge",
    )(x, oa, ob, z, gates, wa, wb, wo)


def _project(x2d, g_norm, w_in, wts, dims):
    d, wa_w, wb_w, kva, kvb, hb = dims
    n_kv_a, n_kv_b = kva // HEAD_DIM, kvb // HEAD_DIM
    h = _rmsnorm(x2d, g_norm)
    tile = wa_w
    assert wa_w == wb_w == 2 * kva == 2 * kvb and tile % LANES == 0
    f_block = (2 * wa_w + 2 * wb_w + 2 * kva + 2 * kvb) // LANES
    (q16,) = _proj(_proj_q_kernel, h, w_in, (0, 3, 2), tile, [(wts["g_q"], tile, 0)], [(BF16, (wa_w + wb_w,))])
    ka, va, ka16, va16 = _proj(functools.partial(_proj_kv_kernel, n_kv=n_kv_a, n_forget=0), h, w_in, (1, 0, 1), tile,
                               [(wts["g_ka"], tile, 0)],
                               [(F32, (n_kv_a, HEAD_DIM)), (F32, (n_kv_a, HEAD_DIM)), (BF16, (kva,)), (BF16, (kva,))])
    kb, vb, kb16, vb16, logf = _proj(functools.partial(_proj_kv_kernel, n_kv=n_kv_b, n_forget=hb), h, w_in, (4, 0, 1),
                                     tile, [(wts["g_kb"], tile, 0), (w_in, LANES, f_block), (wts["b_f"], LANES, 0)],
                                     [(F32, (n_kv_b, HEAD_DIM)), (F32, (n_kv_b, HEAD_DIM)), (BF16, (kvb,)),
                                      (BF16, (kvb,)), (F32, (hb,))])
    (z,) = _proj(_proj_plain_kernel, h, w_in, (2, 3, 2), tile, [], [(F32, (wa_w + wb_w,))])
    (gates,) = _proj(_proj_gate_kernel, h, wts["w_g"], (0, 1, 2), d, [(wts["b_g"], d, 0)], [(F32, (2 * d,))])
    return dict(q16=q16, ka=ka, va=va, ka16=ka16, va16=va16, kb=kb, vb=vb, kb16=kb16, vb16=vb16, logf=logf, z=z,
                gates=gates)


def _layer_weights(w_in, b_forget, b_merge, gq_a, gk_a, gq_b, gk_b, wa, wb, wo, dims):
    d, wa_w, wb_w, kva, kvb, hb = dims
    gate0 = 2 * wa_w + 2 * wb_w + 2 * kva + 2 * kvb + hb
    return dict(
        g_q=jnp.concatenate([jnp.tile(gq_a, wa_w // HEAD_DIM), jnp.tile(gq_b, wb_w // HEAD_DIM)]).reshape(1, -1),
        g_ka=jnp.concatenate([jnp.tile(gk_a, kva // HEAD_DIM), jnp.zeros((kva,), F32)]).reshape(1, -1),
        g_kb=jnp.concatenate([jnp.tile(gk_b, kvb // HEAD_DIM), jnp.zeros((kvb,), F32)]).reshape(1, -1),
        b_f=jnp.concatenate([b_forget, jnp.zeros((LANES - hb,), F32)]).reshape(1, -1),
        w_g=w_in[:, gate0:gate0 + 2 * d].astype(BF16),
        b_g=b_merge.reshape(1, 2 * d),
        wa=wa.astype(BF16), wb=wb.astype(BF16), wo=wo.astype(BF16),
    )


def kernel(x_prompt, x_sample, cache_k_moba, cache_v_moba, cache_k_fox, cache_v_fox, cache_logf_fox, page_table,
           g_norm, w_in, b_forget, b_merge, gq_moba, gk_moba, gq_fox, gk_fox, w_branch_moba, w_branch_fox, w_out):
    depth = w_in.shape[0]
    batch, seq, d = x_prompt.shape
    rows, dec_seq, _ = x_sample.shape
    assert dec_seq == 1 and seq % MOBA_BLOCK == 0
    n_kv_a, n_kv_b = cache_k_moba.shape[3], cache_k_fox.shape[3]
    hb = b_forget.shape[1]
    wa_w, wb_w = w_branch_moba.shape[1], w_branch_fox.shape[1]
    ha = wa_w // HEAD_DIM
    assert wb_w == hb * HEAD_DIM
    kva, kvb = n_kv_a * HEAD_DIM, n_kv_b * HEAD_DIM
    slopes = 2.0 ** (-8.0 * jnp.arange(1, ha + 1, dtype=F32) / ha)
    past = page_table.shape[1] * cache_k_moba.shape[2]
    pos = jnp.arange(past + SUBLANES)[:, None]
    alibi_decode = jnp.where(pos < past, (-LOG2E) * slopes[None, :] * (past - pos).astype(F32),
                             jnp.where(pos == past, 0.0, NEG_INF))

    xp = x_prompt.reshape(batch * seq, d)
    xs = x_sample.reshape(rows, d)
    outs_p, outs_s = [], []
    for l in range(depth):
        dims = (d, wa_w, wb_w, kva, kvb, hb)
        wts = _layer_weights(w_in[l], b_forget[l], b_merge[l], gq_moba[l], gk_moba[l], gq_fox[l], gk_fox[l],
                             w_branch_moba[l], w_branch_fox[l], w_out[l], dims)
        pp = _project(xp, g_norm[l], w_in[l], wts, dims)
        kmean = _kmean(pp["ka"], batch, seq)
        oa = _moba_prompt(pp["q16"], 0, pp["ka16"], pp["va16"], kmean, slopes, batch, seq, ha, n_kv_a)
        group_b = hb // n_kv_b
        c_row = _cumsum_rows(pp["logf"].reshape(batch, seq, hb).transpose(0, 2, 1))
        c_col = c_row.reshape(batch, n_kv_b, group_b, seq).transpose(1, 0, 3, 2).reshape(n_kv_b, batch * seq, group_b)
        ob = _fox_prompt(pp["q16"], wa_w, pp["kb16"], pp["vb16"], c_col,
                         c_row.reshape(batch, n_kv_b, group_b, seq), batch, seq, hb, n_kv_b)
        xp = _merge(xp, oa, ob, pp["z"], pp["gates"], wts["wa"], wts["wb"], wts["wo"])
        outs_p.append((pp["ka"], pp["va"], pp["kb"], pp["vb"], pp["logf"]))
        ps = _project(xs, g_norm[l], w_in[l], wts, dims)
        oa_s = _decode_attention(_moba_decode_kernel, l, page_table, ps["q16"][:, :wa_w], ps["ka"], ps["va"],
                                 alibi_decode, False, [cache_k_moba, cache_v_moba], n_kv_a)
        ob_s = _decode_attention(_fox_decode_kernel, l, page_table, ps["q16"][:, wa_w:], ps["kb"], ps["vb"],
                                 ps["logf"], True, [cache_k_fox, cache_v_fox, _page_cumsum(cache_logf_fox[l])], n_kv_b)
        xs = _merge(xs, oa_s, ob_s, ps["z"], ps["gates"], wts["wa"], wts["wb"], wts["wo"])
        outs_s.append((ps["ka"], ps["va"], ps["kb"], ps["vb"], ps["logf"]))

    def stacked(outs, i, shape):
        return jnp.stack([o[i].reshape(shape) for o in outs])

    res = [xp.reshape(batch, seq, d), xs.reshape(rows, 1, d)]
    for outs, lead in ((outs_p, (batch, seq)), (outs_s, (rows, 1))):
        res += [stacked(outs, 0, lead + (n_kv_a, HEAD_DIM)), stacked(outs, 1, lead + (n_kv_a, HEAD_DIM)),
                stacked(outs, 2, lead + (n_kv_b, HEAD_DIM)), stacked(outs, 3, lead + (n_kv_b, HEAD_DIM)),
                stacked(outs, 4, lead + (hb,))]
    return tuple(res)
```

```python
import functools

import jax
import jax.numpy as jnp
from jax import lax
from jax.experimental import pallas as pl
from jax.experimental.pallas import tpu as pltpu

HEAD_DIM = 128
MOBA_BLOCK = 256
MOBA_TOPK = 3
NORM_EPS = 1e-6
NEG_INF = -1e30
M_INIT = 0.5 * NEG_INF
SCALE = HEAD_DIM ** -0.5
LOG2E = 1.4426950408889634
LANES = 128
SUBLANES = 8
VMEM_LIMIT = 52 * 1024 * 1024
ATTN_TILE = 512

F32 = jnp.float32
BF16 = jnp.bfloat16
HI = lax.Precision.HIGHEST
NT_DIMS = (((1,), (1,)), ((), ()))
TN_DIMS = (((0,), (0,)), ((), ()))


def _params(*sem):
    return pltpu.CompilerParams(dimension_semantics=sem, vmem_limit_bytes=VMEM_LIMIT)


def _row_tile(m, want):
    t = min(m, want)
    assert m % t == 0
    return t


def _rmsnorm_kernel(x_ref, g_ref, h_ref):
    x = x_ref[...]
    ms = jnp.mean(x * x, axis=-1, keepdims=True)
    h_ref[...] = (x * lax.rsqrt(ms + NORM_EPS) * g_ref[...]).astype(h_ref.dtype)


def _rmsnorm(x, g):
    m, d = x.shape
    tm = _row_tile(m, 512)
    return pl.pallas_call(
        _rmsnorm_kernel,
        grid=(m // tm,),
        in_specs=[pl.BlockSpec((tm, d), lambda i: (i, 0)), pl.BlockSpec((1, d), lambda i: (0, 0))],
        out_specs=pl.BlockSpec((tm, d), lambda i: (i, 0)),
        out_shape=jax.ShapeDtypeStruct((m, d), BF16),
        compiler_params=_params("parallel"),
        name="rmsnorm",
    )(x, g.reshape(1, d))


def _head_norm(a, g):
    ms = jnp.mean(a * a, axis=-1, keepdims=True)
    return a * lax.rsqrt(ms + NORM_EPS) * g


def _log_sigmoid(x):
    return jnp.minimum(x, 0.0) - jnp.log1p(jnp.exp(-jnp.abs(x)))


def _proj_q_kernel(h_ref, w_ref, g_ref, o_ref):
    acc = lax.dot_general(h_ref[...], w_ref[...], NT_DIMS, preferred_element_type=F32)
    for j in range(acc.shape[1] // HEAD_DIM):
        sl = slice(j * HEAD_DIM, (j + 1) * HEAD_DIM)
        o_ref[:, sl] = (_head_norm(acc[:, sl], g_ref[:, sl]) * (SCALE * LOG2E)).astype(o_ref.dtype)


def _proj_kv_kernel(h_ref, w_ref, g_ref, *refs, n_kv, n_forget):
    if n_forget:
        wf_ref, bf_ref, *refs = refs
    k32_ref, v32_ref, k16_ref, v16_ref = refs[:4]
    kvw = n_kv * HEAD_DIM
    h = h_ref[...]
    acc = lax.dot_general(h, w_ref[...], NT_DIMS, preferred_element_type=F32)
    for j in range(n_kv):
        sl = slice(j * HEAD_DIM, (j + 1) * HEAD_DIM)
        kn = _head_norm(acc[:, sl], g_ref[:, sl])
        k32_ref[:, j, :] = kn
        k16_ref[:, sl] = kn.astype(BF16)
        v = acc[:, kvw + j * HEAD_DIM:kvw + (j + 1) * HEAD_DIM]
        v32_ref[:, j, :] = v
        v16_ref[:, sl] = v.astype(BF16)
    if n_forget:
        f = lax.dot_general(h, wf_ref[...].astype(BF16), NT_DIMS, preferred_element_type=F32)
        refs[4][...] = _log_sigmoid(f[:, :n_forget] + bf_ref[:, :n_forget])


def _proj_plain_kernel(h_ref, w_ref, o_ref):
    o_ref[...] = lax.dot_general(h_ref[...], w_ref[...], NT_DIMS, preferred_element_type=F32).astype(o_ref.dtype)


def _proj_gate_kernel(h_ref, w_ref, b_ref, o_ref):
    acc = lax.dot_general(h_ref[...], w_ref[...], NT_DIMS, preferred_element_type=F32)
    o_ref[...] = jax.nn.sigmoid(acc + b_ref[...]).astype(o_ref.dtype)


def _with_cached_weights(body, h_ref, w_ref, *refs):
    *refs, w16_sc = refs

    @pl.when(pl.program_id(1) == 0)
    def _():
        w16_sc[...] = w_ref[...].astype(BF16)

    body(h_ref, w16_sc, *refs)


def _weight_rows(wt, n_rows, first, stride=0):
    return pl.BlockSpec((pl.Element(n_rows), pl.Element(wt.shape[1])),
                        lambda j, i: (pl.multiple_of(first + stride * j, SUBLANES), 0))


def _proj(body, h, wt, rows, tn, extras, outs, forget_row=None):
    m, d = h.shape
    first, stride, count = rows
    assert first % SUBLANES == 0 and stride % SUBLANES == 0
    tm = _row_tile(m, 1024)
    in_specs = [pl.BlockSpec((tm, d), lambda j, i: (i, 0)), _weight_rows(wt, tn, first, stride)]
    args = [h, wt]
    for arr, width, base in extras[:1]:
        in_specs.append(pl.BlockSpec((arr.shape[0], width), lambda j, i, base=base: (0, base + j)))
        args.append(arr)
    if forget_row is not None:
        in_specs.append(_weight_rows(wt, LANES, forget_row))
        args.append(wt)
    for arr, width, base in extras[1:]:
        in_specs.append(pl.BlockSpec((arr.shape[0], width), lambda j, i, base=base: (0, base + j)))
        args.append(arr)
    out_specs, out_shape = [], []
    for dt, trail in outs:
        if len(trail) == 2:
            assert count == 1
            out_specs.append(pl.BlockSpec((tm,) + trail, lambda j, i: (i, 0, 0)))
        else:
            out_specs.append(pl.BlockSpec((tm, trail[0] // count), lambda j, i: (i, j)))
        out_shape.append(jax.ShapeDtypeStruct((m,) + trail, dt))
    return pl.pallas_call(
        functools.partial(_with_cached_weights, body),
        grid=(count, m // tm),
        in_specs=in_specs,
        out_specs=out_specs,
        out_shape=out_shape,
        scratch_shapes=[pltpu.VMEM((tn, d), BF16)],
        compiler_params=_params("parallel", "arbitrary"),
        name="proj",
    )(*args)


def _kmean_kernel(k_ref, o_ref, *, nblk, n_kv):
    o_ref[...] = jnp.zeros_like(o_ref)
    for j in range(n_kv):
        for n in range(nblk):
            blk = k_ref[n * MOBA_BLOCK:(n + 1) * MOBA_BLOCK, j, :]
            o_ref[0, j, n:n + 1, :] = jnp.sum(blk, axis=0, keepdims=True) * (1.0 / MOBA_BLOCK)


def _kmean(k32, batch, seq):
    n_kv = k32.shape[1]
    nblk = seq // MOBA_BLOCK
    assert nblk <= LANES
    return pl.pallas_call(
        functools.partial(_kmean_kernel, nblk=nblk, n_kv=n_kv),
        grid=(batch,),
        in_specs=[pl.BlockSpec((seq, n_kv, HEAD_DIM), lambda b: (b, 0, 0))],
        out_specs=pl.BlockSpec((1, n_kv, LANES, HEAD_DIM), lambda b: (b, 0, 0, 0)),
        out_shape=jax.ShapeDtypeStruct((batch, n_kv, LANES, HEAD_DIM), F32),
        compiler_params=_params("parallel"),
        name="kmean",
    )(k32)


def _split3(x):
    hi = x.astype(BF16).astype(F32)
    r = x - hi
    mid = r.astype(BF16).astype(F32)
    return hi, mid, (r - mid).astype(BF16).astype(F32)


def _store_wide_queries(qw_sc, q_ref, coef_rows):
    tq = q_ref.shape[0]
    for h, coef in enumerate(coef_rows):
        qw_sc[h * tq:(h + 1) * tq, :HEAD_DIM] = q_ref[:, h * HEAD_DIM:(h + 1) * HEAD_DIM]
        qw_sc[h * tq:(h + 1) * tq, HEAD_DIM:] = jnp.broadcast_to(coef.astype(BF16), (tq, LANES))


def _qk_logits(k, kx, qw_sc):
    return lax.dot_general(jnp.concatenate([k, kx], axis=1), qw_sc[...], NT_DIMS, preferred_element_type=F32)


def _pipelined_tiles(qi, s_refs, logits_into, process):
    s0, s1 = s_refs
    logits_into(s0, 0)

    def pair(kp, carry):
        t = 2 * kp
        logits_into(s1, t + 1)
        process(s0, t, False)
        logits_into(s0, t + 2)
        process(s1, t + 1, False)
        return carry

    lax.fori_loop(0, qi // 2, pair, 0)

    @pl.when(qi % 2 == 0)
    def _():
        process(s0, qi, True)

    @pl.when(qi % 2 == 1)
    def _():
        logits_into(s1, qi)
        process(s0, qi - 1, False)
        process(s1, qi, True)


def _softmax_tile_t(head_parts, v, m_sc, l_sc, acc_sc):
    heads = range(len(head_parts))
    tq = head_parts[0][0][0].shape[1]
    m_prev = [m_sc[h] for h in heads]
    m_new = []
    for h in heads:
        m = m_prev[h]
        for u, rb in head_parts[h]:
            m = jnp.maximum(m, jnp.max(u, axis=0, keepdims=True) + rb)
        m_new.append(m)
    alphas, p16s = [], []
    for h in heads:
        alpha = jnp.exp2(m_prev[h] - m_new[h])
        l_new = alpha * l_sc[h]
        ps = []
        for u, rb in head_parts[h]:
            p = jnp.exp2(u - (m_new[h] - rb))
            l_new = l_new + jnp.sum(p, axis=0, keepdims=True)
            ps.append(p.astype(v.dtype))
        p16s.append(ps[0] if len(ps) == 1 else jnp.concatenate(ps, axis=0))
        alphas.append(alpha)
        l_sc[h] = l_new
        m_sc[h] = m_new[h]
    pv = lax.dot_general(v, jnp.concatenate(p16s, axis=1), TN_DIMS, preferred_element_type=F32)
    for h in heads:
        acc_sc[h] = alphas[h] * acc_sc[h] + pv[:, h * tq:(h + 1) * tq]


def _init_softmax_state(m_sc, l_sc, acc_sc):
    m_sc[...] = jnp.full(m_sc.shape, M_INIT, F32)
    l_sc[...] = jnp.zeros(l_sc.shape, F32)
    acc_sc[...] = jnp.zeros(acc_sc.shape, F32)


def _write_heads(o_ref, l_sc, acc_sc, group):
    for h in range(group):
        o_ref[:, h * HEAD_DIM:(h + 1) * HEAD_DIM] = (acc_sc[h] / l_sc[h]).T.astype(o_ref.dtype)


def _moba_prompt_kernel(slopes_ref, q_ref, k_ref, v_ref, kmean_ref, o_ref, m_sc, l_sc, acc_sc, selb_sc, qw_sc, kx_sc,
                        s0_sc, s1_sc, *, group, nblk):
    tq = q_ref.shape[0]
    per_tile = tq // MOBA_BLOCK
    kvh = pl.program_id(1)
    qi = pl.program_id(2)
    _init_softmax_state(m_sc, l_sc, acc_sc)

    qcol = lax.broadcasted_iota(jnp.int32, (1, tq), 1)
    own = qi * per_tile + qcol // MOBA_BLOCK
    blk = lax.broadcasted_iota(jnp.int32, (nblk, tq), 0)
    km = kmean_ref[0, 0]
    for h in range(group):
        qh = q_ref[:, h * HEAD_DIM:(h + 1) * HEAD_DIM].astype(F32)
        g = lax.dot_general(km, qh, NT_DIMS, precision=HI, preferred_element_type=F32)[:nblk]
        cnt = jnp.zeros((nblk, tq), jnp.int32)
        for m in range(nblk):
            gm = g[m:m + 1, :]
            beats = (gm > g) | ((gm == g) & (blk > m))
            cnt = cnt + (beats & (own > m)).astype(jnp.int32)
        sel = (blk < own) & (cnt < MOBA_TOPK)
        selb_sc[h] = jnp.where(sel, 0.0, NEG_INF)

    slope2 = [slopes_ref[kvh * group + h] * LOG2E for h in range(group)]
    lane = lax.broadcasted_iota(jnp.int32, (1, LANES), 1)
    coefs = []
    for h in range(group):
        hi, mid, lo = _split3(jnp.zeros((1, LANES), F32) + slope2[h])
        coefs.append(jnp.where(lane < 6, jnp.where(lane % 3 == 0, hi, jnp.where(lane % 3 == 1, mid, lo)), 0.0))
    _store_wide_queries(qw_sc, q_ref, coefs)
    koff = lax.broadcasted_iota(jnp.int32, (tq, LANES), 0)
    klane = lax.broadcasted_iota(jnp.int32, (tq, LANES), 1)
    kx_sc[...] = jnp.where(klane < 3, (koff // MOBA_BLOCK) * MOBA_BLOCK,
                           jnp.where(klane < 6, koff % MOBA_BLOCK, 0)).astype(F32).astype(BF16)

    def logits_into(s_ref, ki):
        s_ref[...] = _qk_logits(k_ref[pl.ds(pl.multiple_of(ki * tq, tq), tq), :], kx_sc[...], qw_sc)

    def process(s_ref, ki, diagonal):
        parts = []
        for h in range(group):
            query_bias = -slope2[h] * (((qi - ki) * tq).astype(F32) + qcol.astype(F32))
            parts.append([])
            for c in range(per_tile):
                kb = ki * per_tile + c
                u = s_ref[c * MOBA_BLOCK:(c + 1) * MOBA_BLOCK, h * tq:(h + 1) * tq]
                selrow = selb_sc[h, pl.ds(kb, 1), :]
                if diagonal:
                    rel = (lax.broadcasted_iota(jnp.int32, (MOBA_BLOCK, tq), 1)
                           - lax.broadcasted_iota(jnp.int32, (MOBA_BLOCK, tq), 0))
                    u = jnp.where(rel >= c * MOBA_BLOCK, u, NEG_INF)
                    selrow = jnp.where(own > kb, selrow, 0.0)
                parts[h].append((u, selrow + query_bias))
        _softmax_tile_t(parts, v_ref[pl.ds(pl.multiple_of(ki * tq, tq), tq), :], m_sc, l_sc, acc_sc)

    _pipelined_tiles(qi, (s0_sc, s1_sc), logits_into, process)
    _write_heads(o_ref, l_sc, acc_sc, group)


def _moba_prompt(q16, q_col0, k16, v16, kmean, slopes, batch, seq, n_heads, n_kv):
    group = n_heads // n_kv
    tq = _row_tile(seq, ATTN_TILE)
    assert tq % MOBA_BLOCK == 0
    nq = seq // tq
    nblk = seq // MOBA_BLOCK
    gw = group * HEAD_DIM
    qb0 = q_col0 // gw
    return pl.pallas_call(
        functools.partial(_moba_prompt_kernel, group=group, nblk=nblk),
        grid=(batch, n_kv, nq),
        in_specs=[
            pl.BlockSpec(memory_space=pltpu.SMEM),
            pl.BlockSpec((tq, gw), lambda b, j, i: (b * nq + i, qb0 + j)),
            pl.BlockSpec((seq, HEAD_DIM), lambda b, j, i: (b, j)),
            pl.BlockSpec((seq, HEAD_DIM), lambda b, j, i: (b, j)),
            pl.BlockSpec((1, 1, LANES, HEAD_DIM), lambda b, j, i: (b, j, 0, 0)),
        ],
        out_specs=pl.BlockSpec((tq, gw), lambda b, j, i: (b * nq + i, j)),
        out_shape=jax.ShapeDtypeStruct((batch * seq, n_heads * HEAD_DIM), F32),
        scratch_shapes=[
            pltpu.VMEM((group, 1, tq), F32),
            pltpu.VMEM((group, 1, tq), F32),
            pltpu.VMEM((group, HEAD_DIM, tq), F32),
            pltpu.VMEM((group, nblk, tq), F32),
            pltpu.VMEM((group * tq, 2 * HEAD_DIM), BF16),
            pltpu.VMEM((tq, LANES), BF16),
            pltpu.VMEM((tq, group * tq), F32),
            pltpu.VMEM((tq, group * tq), F32),
        ],
        compiler_params=_params("parallel", "parallel", "arbitrary"),
        name="moba_prompt",
    )(slopes, q16, k16, v16, kmean)


def _fox_prompt_kernel(q_ref, k_ref, v_ref, ccol_ref, crow_ref, o_ref, m_sc, l_sc, acc_sc, qw_sc, kx_sc, s0_sc, s1_sc,
                       *, group):
    tq = q_ref.shape[0]
    seq = k_ref.shape[0]
    qi = pl.program_id(2)
    _init_softmax_state(m_sc, l_sc, acc_sc)
    lane = lax.broadcasted_iota(jnp.int32, (1, LANES), 1)
    _store_wide_queries(qw_sc, q_ref, [jnp.where(lane // 3 == h, -1.0, 0.0) for h in range(group)])

    @pl.when(qi == 0)
    def _():
        place = (lax.broadcasted_iota(jnp.int32, (group, LANES), 1) // 3
                 == lax.broadcasted_iota(jnp.int32, (group, LANES), 0)).astype(F32)
        x = jnp.dot(ccol_ref[0] * LOG2E, place, precision=HI, preferred_element_type=F32)
        hi, mid, lo = _split3(x)
        l3 = lax.broadcasted_iota(jnp.int32, (seq, LANES), 1) % 3
        kx_sc[...] = jnp.where(l3 == 0, hi, jnp.where(l3 == 1, mid, lo)).astype(BF16)

    def logits_into(s_ref, ki):
        off = pl.multiple_of(ki * tq, tq)
        s_ref[...] = _qk_logits(k_ref[pl.ds(off, tq), :], kx_sc[pl.ds(off, tq), :], qw_sc)

    def cq(h):
        return crow_ref[0, 0, h:h + 1, pl.ds(pl.multiple_of(qi * tq, tq), tq)] * LOG2E

    def process(s_ref, ki, diagonal):
        u = [s_ref[:, h * tq:(h + 1) * tq] for h in range(group)]
        if diagonal:
            krow = lax.broadcasted_iota(jnp.int32, (tq, tq), 0)
            qcol = lax.broadcasted_iota(jnp.int32, (tq, tq), 1)
            u = [jnp.where(qcol >= krow, uh, NEG_INF) for uh in u]
        _softmax_tile_t([[(u[h], cq(h))] for h in range(group)], v_ref[pl.ds(pl.multiple_of(ki * tq, tq), tq), :],
                        m_sc, l_sc, acc_sc)

    _pipelined_tiles(qi, (s0_sc, s1_sc), logits_into, process)
    _write_heads(o_ref, l_sc, acc_sc, group)


def _fox_prompt(q16, q_col0, k16, v16, c_col, c_row, batch, seq, n_heads, n_kv):
    group = n_heads // n_kv
    tq = _row_tile(seq, ATTN_TILE)
    nq = seq // tq
    gw = group * HEAD_DIM
    qb0 = q_col0 // gw
    return pl.pallas_call(
        functools.partial(_fox_prompt_kernel, group=group),
        grid=(batch, n_kv, nq),
        in_specs=[
            pl.BlockSpec((tq, gw), lambda b, j, i: (b * nq + i, qb0 + j)),
            pl.BlockSpec((seq, HEAD_DIM), lambda b, j, i: (b, j)),
            pl.BlockSpec((seq, HEAD_DIM), lambda b, j, i: (b, j)),
            pl.BlockSpec((1, seq, group), lambda b, j, i: (j, b, 0)),
            pl.BlockSpec((1, 1, group, seq), lambda b, j, i: (b, j, 0, 0)),
        ],
        out_specs=pl.BlockSpec((tq, gw), lambda b, j, i: (b * nq + i, j)),
        out_shape=jax.ShapeDtypeStruct((batch * seq, n_heads * HEAD_DIM), F32),
        scratch_shapes=[
            pltpu.VMEM((group, 1, tq), F32),
            pltpu.VMEM((group, 1, tq), F32),
            pltpu.VMEM((group, HEAD_DIM, tq), F32),
            pltpu.VMEM((group * tq, 2 * HEAD_DIM), BF16),
            pltpu.VMEM((seq, LANES), BF16),
            pltpu.VMEM((tq, group * tq), F32),
            pltpu.VMEM((tq, group * tq), F32),
        ],
        compiler_params=_params("arbitrary", "arbitrary", "arbitrary"),
        name="fox_prompt",
    )(q16, k16, v16, c_col, c_row)


def _cumsum_kernel(x_ref, o_ref, *, chunk):
    rows, seq = x_ref.shape[1:]
    r = lax.broadcasted_iota(jnp.int32, (chunk, chunk), 0)
    c = lax.broadcasted_iota(jnp.int32, (chunk, chunk), 1)
    upper = (r <= c).astype(F32)
    carry = jnp.zeros((rows, 1), F32)
    for i in range(seq // chunk):
        sl = slice(i * chunk, (i + 1) * chunk)
        cs = jnp.dot(x_ref[0, :, sl], upper, precision=HI, preferred_element_type=F32) + carry
        o_ref[0, :, sl] = cs
        carry = cs[:, chunk - 1:chunk]


def _cumsum_rows(x):
    batch, rows, seq = x.shape
    chunk = 256
    assert seq % chunk == 0
    return pl.pallas_call(
        functools.partial(_cumsum_kernel, chunk=chunk),
        grid=(batch,),
        in_specs=[pl.BlockSpec((1, rows, seq), lambda b: (b, 0, 0))],
        out_specs=pl.BlockSpec((1, rows, seq), lambda b: (b, 0, 0)),
        out_shape=jax.ShapeDtypeStruct(x.shape, F32),
        compiler_params=_params("parallel"),
        name="cumsum",
    )(x)


def _page_copies(pt_ref, row, slot, srcs, bufs, sems, n_pages, page):
    copies = []
    for p in range(n_pages):
        idx = pt_ref[row * n_pages + p]
        rows = pl.ds(p * page, page)
        for a, ((src, lead), buf) in enumerate(zip(srcs, bufs)):
            src_page = src.at[lead + (idx,)]
            if len(buf.shape) == 4:
                for j in range(buf.shape[1]):
                    copies.append(pltpu.make_async_copy(src_page.at[:, j, :], buf.at[slot, j, rows, :], sems.at[a, slot]))
            else:
                copies.append(pltpu.make_async_copy(src_page, buf.at[slot, rows], sems.at[a, slot]))
    return copies


def _fetch_rows(pt_ref, srcs, bufs, sems, n_pages, page):
    b = pl.program_id(0)
    nb = pl.num_programs(0)
    slot = lax.rem(b, 2)

    @pl.when(b == 0)
    def _():
        for c in _page_copies(pt_ref, b, slot, srcs, bufs, sems, n_pages, page):
            c.start()

    @pl.when(b + 1 < nb)
    def _():
        for c in _page_copies(pt_ref, b + 1, 1 - slot, srcs, bufs, sems, n_pages, page):
            c.start()

    for c in _page_copies(pt_ref, b, slot, srcs, bufs, sems, n_pages, page):
        c.wait()
    return slot


def _append_new_row(buf, slot, past, new):
    first = lax.broadcasted_iota(jnp.int32, (SUBLANES, HEAD_DIM), 0) == 0
    for j in range(new.shape[0]):
        buf[slot, j, past:past + SUBLANES, :] = jnp.where(first, new[j:j + 1, :], 0.0)


def _tail_bias(heads):
    first = lax.broadcasted_iota(jnp.int32, (SUBLANES, heads), 0) == 0
    return jnp.where(first, 0.0, NEG_INF)


def _group_queries(q, j, group):
    r = lax.broadcasted_iota(jnp.int32, q.shape, 0)
    return jnp.where(r // group == j, q, jnp.zeros_like(q))


def _decode_scores(kb, q, n_kv):
    group = q.shape[0] // n_kv
    s = None
    for j0 in range(0, n_kv, 2):
        js = range(j0, min(j0 + 2, n_kv))
        k2 = jnp.concatenate([kb[j].astype(BF16) for j in js], axis=1)
        q2 = jnp.concatenate([_group_queries(q, j, group) for j in js], axis=1)
        sj = lax.dot_general(k2, q2, NT_DIMS, preferred_element_type=F32)
        s = sj if s is None else s + sj
    return s


def _decode_softmax_pv(s, vb, n_kv):
    heads = s.shape[1]
    group = heads // n_kv
    m = jnp.max(s, axis=0, keepdims=True)
    p = jnp.exp2(s - m)
    l = jnp.sum(p, axis=0, keepdims=True)
    p16 = p.astype(BF16)
    o_t = jnp.concatenate(
        [lax.dot_general(vb[j].astype(BF16), p16, TN_DIMS, preferred_element_type=F32) for j in range(n_kv)],
        axis=0) / l
    o = o_t.T
    return jnp.concatenate(
        [o[h:h + 1, (h // group) * HEAD_DIM:(h // group + 1) * HEAD_DIM] for h in range(heads)], axis=0)


def _moba_decode_kernel(pt_ref, q_ref, knew_ref, vnew_ref, alibi_ref, k_hbm, v_hbm, o_ref, kbuf, vbuf, sems,
                        *, layer, n_kv, n_pages, page):
    slot = _fetch_rows(pt_ref, ((k_hbm, (layer,)), (v_hbm, (layer,))), (kbuf, vbuf), sems, n_pages, page)
    past = n_pages * page
    nblk = past // MOBA_BLOCK
    _append_new_row(kbuf, slot, past, knew_ref[0])
    _append_new_row(vbuf, slot, past, vnew_ref[0])
    kb, vb = kbuf.at[slot], vbuf.at[slot]
    q = q_ref[0]
    heads = q.shape[0]
    group = heads // n_kv

    g = None
    for j in range(n_kv):
        means = jnp.concatenate(
            [jnp.sum(kb[j, n * MOBA_BLOCK:(n + 1) * MOBA_BLOCK, :], axis=0, keepdims=True) for n in range(nblk)],
            axis=0) * (1.0 / MOBA_BLOCK)
        gj = lax.dot_general(means, _group_queries(q, j, group).astype(F32), NT_DIMS, precision=HI,
                             preferred_element_type=F32)
        g = gj if g is None else g + gj
    blk = lax.broadcasted_iota(jnp.int32, g.shape, 0)
    cnt = jnp.zeros(g.shape, jnp.int32)
    for m in range(nblk):
        gm = g[m:m + 1, :]
        cnt = cnt + ((gm > g) | ((gm == g) & (blk > m))).astype(jnp.int32)
    sel = cnt < MOBA_TOPK
    keep = jnp.concatenate([jnp.broadcast_to(sel[n:n + 1, :], (MOBA_BLOCK, heads)) for n in range(nblk)]
                           + [jnp.ones((SUBLANES, heads), jnp.bool_)], axis=0)
    s = jnp.where(keep, _decode_scores(kb, q, n_kv) + alibi_ref[...], NEG_INF)
    o_ref[0] = _decode_softmax_pv(s, vb, n_kv).astype(o_ref.dtype)


def _fox_decode_kernel(pt_ref, q_ref, knew_ref, vnew_ref, fnew_ref, k_hbm, v_hbm, c_hbm, o_ref, kbuf, vbuf, cbuf, sems,
                       *, layer, n_kv, n_pages, page):
    slot = _fetch_rows(pt_ref, ((k_hbm, (layer,)), (v_hbm, (layer,)), (c_hbm, ())), (kbuf, vbuf, cbuf), sems,
                       n_pages, page)
    past = n_pages * page
    _append_new_row(kbuf, slot, past, knew_ref[0])
    _append_new_row(vbuf, slot, past, vnew_ref[0])
    kb, vb = kbuf.at[slot], vbuf.at[slot]
    q = q_ref[0]
    heads = q.shape[0]

    carry = jnp.zeros((1, heads), F32)
    cums = []
    for j in range(n_pages):
        cs = cbuf[slot, j * page:(j + 1) * page, :] + carry
        cums.append(cs)
        carry = cs[page - 1:page, :]
    c_now = carry + fnew_ref[0]
    bias = [(c_now - cs) * LOG2E for cs in cums]
    s = _decode_scores(kb, q, n_kv) + jnp.concatenate(bias + [_tail_bias(heads)], axis=0)
    o_ref[0] = _decode_softmax_pv(s, vb, n_kv).astype(o_ref.dtype)


def _decode_attention(body, layer, page_table, q16, k_new, v_new, extra, extra_per_row, caches, n_kv):
    rows, n_pages = page_table.shape
    heads = q16.shape[1] // HEAD_DIM
    page = caches[0].shape[2]
    past = n_pages * page
    assert past % MOBA_BLOCK == 0
    row_spec = lambda shape: pl.BlockSpec((1,) + shape, lambda b, pt: (b, 0, 0))
    in_specs = [row_spec((heads, HEAD_DIM)), row_spec((n_kv, HEAD_DIM)), row_spec((n_kv, HEAD_DIM))]
    if extra_per_row:
        in_specs.append(row_spec((1, heads)))
        extra = extra.reshape(rows, 1, heads)
    else:
        in_specs.append(pl.BlockSpec(extra.shape, lambda b, pt: (0, 0)))
    in_specs += [pl.BlockSpec(memory_space=pl.ANY)] * len(caches)
    scratch = [pltpu.VMEM((2, n_kv, past + SUBLANES, HEAD_DIM) if c.ndim == 5 else (2, past, c.shape[2]), F32)
               for c in caches]
    scratch.append(pltpu.SemaphoreType.DMA((len(caches), 2)))
    return pl.pallas_call(
        functools.partial(body, layer=layer, n_kv=n_kv, n_pages=n_pages, page=page),
        grid_spec=pltpu.PrefetchScalarGridSpec(
            num_scalar_prefetch=1,
            grid=(rows,),
            in_specs=in_specs,
            out_specs=pl.BlockSpec((1, heads, HEAD_DIM), lambda b, pt: (b, 0, 0)),
            scratch_shapes=scratch,
        ),
        out_shape=jax.ShapeDtypeStruct((rows, heads, HEAD_DIM), F32),
        compiler_params=_params("arbitrary"),
        name="decode_attention",
    )(page_table.reshape(-1), q16.reshape(rows, heads, HEAD_DIM), k_new, v_new, extra, *caches
      ).reshape(rows, heads * HEAD_DIM)


def _page_cumsum_kernel(x_ref, o_ref):
    page = x_ref.shape[1]
    r = lax.broadcasted_iota(jnp.int32, (page, page), 0)
    c = lax.broadcasted_iota(jnp.int32, (page, page), 1)
    o_ref[...] = jnp.dot(x_ref[...], (r <= c).astype(F32), precision=HI, preferred_element_type=F32)


def _page_cumsum(logf_cache):
    pool, page, heads = logf_cache.shape
    x = logf_cache.transpose(0, 2, 1).reshape(pool * heads, page)
    tm = _row_tile(pool * heads, 2048)
    out = pl.pallas_call(
        _page_cumsum_kernel,
        grid=(pool * heads // tm,),
        in_specs=[pl.BlockSpec((tm, page), lambda i: (i, 0))],
        out_specs=pl.BlockSpec((tm, page), lambda i: (i, 0)),
        out_shape=jax.ShapeDtypeStruct((pool * heads, page), F32),
        compiler_params=_params("parallel"),
        name="page_cumsum",
    )(x)
    return out.reshape(pool, heads, page).transpose(0, 2, 1)


def _merge_kernel(x_ref, oa_ref, ob_ref, z_ref, g_ref, wa_ref, wb_ref, wo_ref, y_ref):
    wa_w = oa_ref.shape[1]
    d = x_ref.shape[1]

    def branch(o_ref, z, w_ref):
        u = o_ref[...] * (z * jax.nn.sigmoid(z))
        return jnp.dot(u.astype(BF16), w_ref[...], preferred_element_type=F32)

    ya = branch(oa_ref, z_ref[:, :wa_w], wa_ref)
    yb = branch(ob_ref, z_ref[:, wa_w:], wb_ref)
    mixed = g_ref[:, :d] * ya + g_ref[:, d:] * yb
    y_ref[...] = x_ref[...] + jnp.dot(mixed.astype(BF16), wo_ref[...], preferred_element_type=F32)


def _merge(x, oa, ob, z, gates, wa, wb, wo):
    m, d = x.shape
    tm = _row_tile(m, 256)
    rows = lambda width: pl.BlockSpec((tm, width), lambda i: (i, 0))
    resident = lambda w: pl.BlockSpec(w.shape, lambda i: (0, 0), pipeline_mode=pl.Buffered(1))
    return pl.pallas_call(
        _merge_kernel,
        grid=(m // tm,),
        in_specs=[rows(d), rows(oa.shape[1]), rows(ob.shape[1]), rows(z.shape[1]), rows(gates.shape[1]),
                  resident(wa), resident(wb), resident(wo)],
        out_specs=rows(d),
        out_shape=jax.ShapeDtypeStruct((m, d), F32),
        compiler_params=_params("parallel"),
        name="merge",
    )(x, oa, ob, z, gates, wa, wb, wo)


def _project(x2d, g_norm, wt, wts, dims):
    d, wa_w, wb_w, kva, kvb, hb = dims
    n_kv_a, n_kv_b = kva // HEAD_DIM, kvb // HEAD_DIM
    h = _rmsnorm(x2d, g_norm)
    tile = wa_w
    assert wa_w == wb_w == 2 * kva == 2 * kvb
    branch = 2 * wa_w + 2 * kva
    f_row = 2 * branch
    (q16,) = _proj(_proj_q_kernel, h, wt, (0, branch, 2), tile, [(wts["g_q"], tile, 0)], [(BF16, (wa_w + wb_w,))])
    ka, va, ka16, va16 = _proj(functools.partial(_proj_kv_kernel, n_kv=n_kv_a, n_forget=0), h, wt, (wa_w, 0, 1), tile,
                               [(wts["g_ka"], tile, 0)],
                               [(F32, (n_kv_a, HEAD_DIM)), (F32, (n_kv_a, HEAD_DIM)), (BF16, (kva,)), (BF16, (kva,))])
    kb, vb, kb16, vb16, logf = _proj(functools.partial(_proj_kv_kernel, n_kv=n_kv_b, n_forget=hb), h, wt,
                                     (branch + wb_w, 0, 1), tile, [(wts["g_kb"], tile, 0), (wts["b_f"], LANES, 0)],
                                     [(F32, (n_kv_b, HEAD_DIM)), (F32, (n_kv_b, HEAD_DIM)), (BF16, (kvb,)),
                                      (BF16, (kvb,)), (F32, (hb,))], forget_row=f_row)
    (z,) = _proj(_proj_plain_kernel, h, wt, (wa_w + 2 * kva, branch, 2), tile, [], [(F32, (wa_w + wb_w,))])
    gate_tile = min(tile, 2 * d)
    (gates,) = _proj(_proj_gate_kernel, h, wt, (f_row + hb, gate_tile, 2 * d // gate_tile), gate_tile,
                     [(wts["b_g"], gate_tile, 0)], [(F32, (2 * d,))])
    return dict(q16=q16, ka=ka, va=va, ka16=ka16, va16=va16, kb=kb, vb=vb, kb16=kb16, vb16=vb16, logf=logf, z=z,
                gates=gates)


def _layer_weights(b_forget, b_merge, gq_a, gk_a, gq_b, gk_b, wa, wb, wo, dims):
    d, wa_w, wb_w, kva, kvb, hb = dims
    return dict(
        g_q=jnp.concatenate([jnp.tile(gq_a, wa_w // HEAD_DIM), jnp.tile(gq_b, wb_w // HEAD_DIM)]).reshape(1, -1),
        g_ka=jnp.concatenate([jnp.tile(gk_a, kva // HEAD_DIM), jnp.zeros((kva,), F32)]).reshape(1, -1),
        g_kb=jnp.concatenate([jnp.tile(gk_b, kvb // HEAD_DIM), jnp.zeros((kvb,), F32)]).reshape(1, -1),
        b_f=jnp.concatenate([b_forget, jnp.zeros((LANES - hb,), F32)]).reshape(1, -1),
        b_g=b_merge.reshape(1, 2 * d),
        wa=wa.astype(BF16), wb=wb.astype(BF16), wo=wo.astype(BF16),
    )


def kernel(x_prompt, x_sample, cache_k_moba, cache_v_moba, cache_k_fox, cache_v_fox, cache_logf_fox, page_table,
           g_norm, w_in, b_forget, b_merge, gq_moba, gk_moba, gq_fox, gk_fox, w_branch_moba, w_branch_fox, w_out):
    depth = w_in.shape[0]
    batch, seq, d = x_prompt.shape
    rows, dec_seq, _ = x_sample.shape
    assert dec_seq == 1 and seq % MOBA_BLOCK == 0
    n_kv_a, n_kv_b = cache_k_moba.shape[3], cache_k_fox.shape[3]
    hb = b_forget.shape[1]
    wa_w, wb_w = w_branch_moba.shape[1], w_branch_fox.shape[1]
    ha = wa_w // HEAD_DIM
    assert wb_w == hb * HEAD_DIM
    kva, kvb = n_kv_a * HEAD_DIM, n_kv_b * HEAD_DIM
    slopes = 2.0 ** (-8.0 * jnp.arange(1, ha + 1, dtype=F32) / ha)
    past = page_table.shape[1] * cache_k_moba.shape[2]
    pos = jnp.arange(past + SUBLANES)[:, None]
    alibi_decode = jnp.where(pos < past, (-LOG2E) * slopes[None, :] * (past - pos).astype(F32),
                             jnp.where(pos == past, 0.0, NEG_INF))

    xp = x_prompt.reshape(batch * seq, d)
    xs = x_sample.reshape(rows, d)
    outs_p, outs_s = [], []
    for l in range(depth):
        dims = (d, wa_w, wb_w, kva, kvb, hb)
        wt = w_in[l].T
        wts = _layer_weights(b_forget[l], b_merge[l], gq_moba[l], gk_moba[l], gq_fox[l], gk_fox[l],
                             w_branch_moba[l], w_branch_fox[l], w_out[l], dims)
        pp = _project(xp, g_norm[l], wt, wts, dims)
        kmean = _kmean(pp["ka"], batch, seq)
        oa = _moba_prompt(pp["q16"], 0, pp["ka16"], pp["va16"], kmean, slopes, batch, seq, ha, n_kv_a)
        group_b = hb // n_kv_b
        c_row = _cumsum_rows(pp["logf"].reshape(batch, seq, hb).transpose(0, 2, 1))
        c_col = c_row.reshape(batch, n_kv_b, group_b, seq).transpose(1, 0, 3, 2).reshape(n_kv_b, batch * seq, group_b)
        ob = _fox_prompt(pp["q16"], wa_w, pp["kb16"], pp["vb16"], c_col,
                         c_row.reshape(batch, n_kv_b, group_b, seq), batch, seq, hb, n_kv_b)
        xp = _merge(xp, oa, ob, pp["z"], pp["gates"], wts["wa"], wts["wb"], wts["wo"])
        outs_p.append((pp["ka"], pp["va"], pp["kb"], pp["vb"], pp["logf"]))
        ps = _project(xs, g_norm[l], wt, wts, dims)
        oa_s = _decode_attention(_moba_decode_kernel, l, page_table, ps["q16"][:, :wa_w], ps["ka"], ps["va"],
                                 alibi_decode, False, [cache_k_moba, cache_v_moba], n_kv_a)
        ob_s = _decode_attention(_fox_decode_kernel, l, page_table, ps["q16"][:, wa_w:], ps["kb"], ps["vb"],
                                 ps["logf"], True, [cache_k_fox, cache_v_fox, _page_cumsum(cache_logf_fox[l])], n_kv_b)
        xs = _merge(xs, oa_s, ob_s, ps["z"], ps["gates"], wts["wa"], wts["wb"], wts["wo"])
        outs_s.append((ps["ka"], ps["va"], ps["kb"], ps["vb"], ps["logf"]))

    def stacked(outs, i, shape):
        return jnp.stack([o[i].reshape(shape) for o in outs])

    res = [xp.reshape(batch, seq, d), xs.reshape(rows, 1, d)]
    for outs, lead in ((outs_p, (batch, seq)), (outs_s, (rows, 1))):
        res += [stacked(outs, 0, lead + (n_kv_a, HEAD_DIM)), stacked(outs, 1, lead + (n_kv_a, HEAD_DIM)),
                stacked(outs, 2, lead + (n_kv_b, HEAD_DIM)), stacked(outs, 3, lead + (n_kv_b, HEAD_DIM)),
                stacked(outs, 4, lead + (hb,))]
    return tuple(res)
```

```python
import functools

import jax
import jax.numpy as jnp
from jax import lax
from jax.experimental import pallas as pl
from jax.experimental.pallas import tpu as pltpu

HEAD_DIM = 128
MOBA_BLOCK = 256
MOBA_TOPK = 3
NORM_EPS = 1e-6
NEG_INF = -1e30
M_INIT = 0.5 * NEG_INF
SCALE = HEAD_DIM ** -0.5
LOG2E = 1.4426950408889634
LANES = 128
SUBLANES = 8
VMEM_LIMIT = 52 * 1024 * 1024
ATTN_TILE = 512

F32 = jnp.float32
BF16 = jnp.bfloat16
HI = lax.Precision.HIGHEST
NT_DIMS = (((1,), (1,)), ((), ()))
TN_DIMS = (((0,), (0,)), ((), ()))


def _params(*sem):
    return pltpu.CompilerParams(dimension_semantics=sem, vmem_limit_bytes=VMEM_LIMIT)


def _row_tile(m, want):
    t = min(m, want)
    assert m % t == 0
    return t


def _head_norm(a, g):
    ms = jnp.mean(a * a, axis=-1, keepdims=True)
    return a * lax.rsqrt(ms + NORM_EPS) * g


def _log_sigmoid(x):
    return jnp.minimum(x, 0.0) - jnp.log1p(jnp.exp(-jnp.abs(x)))


def _proj_q_kernel(h_ref, w_ref, g_ref, o_ref):
    acc = lax.dot_general(h_ref[...], w_ref[...], NT_DIMS, preferred_element_type=F32)
    for j in range(acc.shape[1] // HEAD_DIM):
        sl = slice(j * HEAD_DIM, (j + 1) * HEAD_DIM)
        o_ref[:, sl] = (_head_norm(acc[:, sl], g_ref[:, sl]) * (SCALE * LOG2E)).astype(o_ref.dtype)


def _proj_kv_kernel(h_ref, w_ref, g_ref, *refs, n_kv, n_forget):
    if n_forget:
        wf_ref, bf_ref, *refs = refs
    k32_hbm, v32_hbm, k16_ref, v16_ref, *refs = refs
    if n_forget:
        logf_ref, *refs = refs
    kbuf, vbuf, sems = refs
    kvw = n_kv * HEAD_DIM
    tm = h_ref.shape[0]
    i = pl.program_id(1)
    last = pl.num_programs(1) - 1
    slot = lax.rem(i, 2)

    def writebacks(step, slot):
        rows = pl.ds(pl.multiple_of(step * tm, tm), tm)
        return [pltpu.make_async_copy(buf.at[slot, :, j * HEAD_DIM:(j + 1) * HEAD_DIM], dst.at[rows, j, :],
                                      sems.at[a, slot])
                for a, (buf, dst) in enumerate(((kbuf, k32_hbm), (vbuf, v32_hbm))) for j in range(n_kv)]

    @pl.when(i >= 2)
    def _():
        for c in writebacks(i - 2, slot):
            c.wait()

    h = h_ref[...]
    acc = lax.dot_general(h, w_ref[...], NT_DIMS, preferred_element_type=F32)
    for j in range(n_kv):
        sl = slice(j * HEAD_DIM, (j + 1) * HEAD_DIM)
        kn = _head_norm(acc[:, sl], g_ref[:, sl])
        kbuf[slot, :, sl] = kn
        k16_ref[:, sl] = kn.astype(BF16)
        v = acc[:, kvw + j * HEAD_DIM:kvw + (j + 1) * HEAD_DIM]
        vbuf[slot, :, sl] = v
        v16_ref[:, sl] = v.astype(BF16)
    if n_forget:
        f = lax.dot_general(h, wf_ref[...].astype(BF16), NT_DIMS, preferred_element_type=F32)
        logf_ref[...] = _log_sigmoid(f[:, :n_forget] + bf_ref[:, :n_forget])
    for c in writebacks(i, slot):
        c.start()

    @pl.when((i == last) & (i >= 1))
    def _():
        for c in writebacks(i - 1, 1 - slot):
            c.wait()

    @pl.when(i == last)
    def _():
        for c in writebacks(i, slot):
            c.wait()


def _proj_plain_kernel(h_ref, w_ref, o_ref):
    o_ref[...] = lax.dot_general(h_ref[...], w_ref[...], NT_DIMS, preferred_element_type=F32).astype(o_ref.dtype)


def _proj_gate_kernel(h_ref, w_ref, b_ref, o_ref):
    acc = lax.dot_general(h_ref[...], w_ref[...], NT_DIMS, preferred_element_type=F32)
    o_ref[...] = jax.nn.sigmoid(acc + b_ref[...]).astype(o_ref.dtype)


def _with_cached_weights(body, h_ref, w_ref, *refs):
    *refs, w16_sc = refs

    @pl.when(pl.program_id(1) == 0)
    def _():
        w16_sc[...] = w_ref[...].astype(BF16)

    body(h_ref, w16_sc, *refs)


def _with_rmsnorm(body, n_inputs, x_ref, w_ref, gn_ref, *refs):
    ins, (h_ref, *rest) = refs[:n_inputs], refs[n_inputs:]
    x = x_ref[...]
    ms = jnp.mean(x * x, axis=-1, keepdims=True)
    h_ref[...] = (x * lax.rsqrt(ms + NORM_EPS) * gn_ref[...]).astype(h_ref.dtype)
    body(h_ref, w_ref, *ins, *rest)


def _weight_rows(wt, n_rows, first, stride=0):
    return pl.BlockSpec((pl.Element(n_rows), pl.Element(wt.shape[1])),
                        lambda j, i: (pl.multiple_of(first + stride * j, SUBLANES), 0))


def _proj(body, h, wt, rows, tn, extras, outs, forget_row=None, norm_gain=None):
    m, d = h.shape
    first, stride, count = rows
    assert first % SUBLANES == 0 and stride % SUBLANES == 0
    tm = _row_tile(m, 1024 if norm_gain is None else 512)
    in_specs = [pl.BlockSpec((tm, d), lambda j, i: (i, 0)), _weight_rows(wt, tn, first, stride)]
    args = [h, wt]
    if norm_gain is not None:
        assert count == 1
        in_specs.append(pl.BlockSpec((1, d), lambda j, i: (0, 0)))
        args.append(norm_gain)
        outs = [(BF16, (d,))] + list(outs)
        body = functools.partial(_with_rmsnorm, body, len(extras) + (forget_row is not None))
    for arr, width, base in extras[:1]:
        in_specs.append(pl.BlockSpec((arr.shape[0], width), lambda j, i, base=base: (0, base + j)))
        args.append(arr)
    if forget_row is not None:
        in_specs.append(_weight_rows(wt, LANES, forget_row))
        args.append(wt)
    for arr, width, base in extras[1:]:
        in_specs.append(pl.BlockSpec((arr.shape[0], width), lambda j, i, base=base: (0, base + j)))
        args.append(arr)
    out_specs, out_shape, staging = [], [], []
    for dt, trail in outs:
        if len(trail) == 2:
            assert count == 1
            out_specs.append(pl.BlockSpec(memory_space=pl.ANY))
            staging.append(pltpu.VMEM((2, tm, trail[0] * trail[1]), dt))
        else:
            out_specs.append(pl.BlockSpec((tm, trail[0] // count), lambda j, i: (i, j)))
        out_shape.append(jax.ShapeDtypeStruct((m,) + trail, dt))
    if staging:
        staging.append(pltpu.SemaphoreType.DMA((len(staging), 2)))
    return pl.pallas_call(
        functools.partial(_with_cached_weights, body),
        grid=(count, m // tm),
        in_specs=in_specs,
        out_specs=out_specs,
        out_shape=out_shape,
        scratch_shapes=staging + [pltpu.VMEM((tn, d), BF16)],
        compiler_params=_params("parallel", "arbitrary"),
        name="proj",
    )(*args)


def _kmean_kernel(k_ref, o_ref, *, nblk, n_kv):
    o_ref[...] = jnp.zeros_like(o_ref)
    for j in range(n_kv):
        for n in range(nblk):
            blk = k_ref[n * MOBA_BLOCK:(n + 1) * MOBA_BLOCK, j, :]
            o_ref[0, j, n:n + 1, :] = jnp.sum(blk, axis=0, keepdims=True) * (1.0 / MOBA_BLOCK)


def _kmean(k32, batch, seq):
    n_kv = k32.shape[1]
    nblk = seq // MOBA_BLOCK
    assert nblk <= LANES
    return pl.pallas_call(
        functools.partial(_kmean_kernel, nblk=nblk, n_kv=n_kv),
        grid=(batch,),
        in_specs=[pl.BlockSpec((seq, n_kv, HEAD_DIM), lambda b: (b, 0, 0))],
        out_specs=pl.BlockSpec((1, n_kv, LANES, HEAD_DIM), lambda b: (b, 0, 0, 0)),
        out_shape=jax.ShapeDtypeStruct((batch, n_kv, LANES, HEAD_DIM), F32),
        compiler_params=_params("parallel"),
        name="kmean",
    )(k32)


def _split3(x):
    hi = x.astype(BF16).astype(F32)
    r = x - hi
    mid = r.astype(BF16).astype(F32)
    return hi, mid, (r - mid).astype(BF16).astype(F32)


def _store_wide_queries(qw_sc, q_ref, coef_rows):
    tq = q_ref.shape[0]
    for h, coef in enumerate(coef_rows):
        qw_sc[h * tq:(h + 1) * tq, :HEAD_DIM] = q_ref[:, h * HEAD_DIM:(h + 1) * HEAD_DIM]
        qw_sc[h * tq:(h + 1) * tq, HEAD_DIM:] = jnp.broadcast_to(coef.astype(BF16), (tq, LANES))


def _qk_logits(k, kx, qw_sc):
    return lax.dot_general(jnp.concatenate([k, kx], axis=1), qw_sc[...], NT_DIMS, preferred_element_type=F32)


def _pipelined_tiles(qi, s_refs, logits_into, process):
    s0, s1 = s_refs
    logits_into(s0, 0)

    def pair(kp, carry):
        t = 2 * kp
        logits_into(s1, t + 1)
        process(s0, t, False)
        logits_into(s0, t + 2)
        process(s1, t + 1, False)
        return carry

    lax.fori_loop(0, qi // 2, pair, 0)

    @pl.when(qi % 2 == 0)
    def _():
        process(s0, qi, True)

    @pl.when(qi % 2 == 1)
    def _():
        logits_into(s1, qi)
        process(s0, qi - 1, False)
        process(s1, qi, True)


def _softmax_tile_t(head_parts, v, m_sc, l_sc, acc_sc):
    heads = range(len(head_parts))
    tq = head_parts[0][0][0].shape[1]
    m_prev = [m_sc[h] for h in heads]
    m_new = []
    for h in heads:
        m = m_prev[h]
        for u, rb in head_parts[h]:
            m = jnp.maximum(m, jnp.max(u, axis=0, keepdims=True) + rb)
        m_new.append(m)
    alphas, p16s = [], []
    for h in heads:
        alpha = jnp.exp2(m_prev[h] - m_new[h])
        l_new = alpha * l_sc[h]
        ps = []
        for u, rb in head_parts[h]:
            p = jnp.exp2(u - (m_new[h] - rb))
            l_new = l_new + jnp.sum(p, axis=0, keepdims=True)
            ps.append(p.astype(v.dtype))
        p16s.append(ps[0] if len(ps) == 1 else jnp.concatenate(ps, axis=0))
        alphas.append(alpha)
        l_sc[h] = l_new
        m_sc[h] = m_new[h]
    pv = lax.dot_general(v, jnp.concatenate(p16s, axis=1), TN_DIMS, preferred_element_type=F32)
    for h in heads:
        acc_sc[h] = alphas[h] * acc_sc[h] + pv[:, h * tq:(h + 1) * tq]


def _init_softmax_state(m_sc, l_sc, acc_sc):
    m_sc[...] = jnp.full(m_sc.shape, M_INIT, F32)
    l_sc[...] = jnp.zeros(l_sc.shape, F32)
    acc_sc[...] = jnp.zeros(acc_sc.shape, F32)


def _write_heads(o_ref, l_sc, acc_sc, group):
    for h in range(group):
        o_ref[:, h * HEAD_DIM:(h + 1) * HEAD_DIM] = (acc_sc[h] / l_sc[h]).T.astype(o_ref.dtype)


def _moba_prompt_kernel(slopes_ref, q_ref, k_ref, v_ref, kmean_ref, o_ref, m_sc, l_sc, acc_sc, selb_sc, qw_sc, kx_sc,
                        s0_sc, s1_sc, *, group, nblk):
    tq = q_ref.shape[0]
    per_tile = tq // MOBA_BLOCK
    kvh = pl.program_id(1)
    qi = pl.program_id(2)
    _init_softmax_state(m_sc, l_sc, acc_sc)

    qcol = lax.broadcasted_iota(jnp.int32, (1, tq), 1)
    own = qi * per_tile + qcol // MOBA_BLOCK
    blk = lax.broadcasted_iota(jnp.int32, (nblk, tq), 0)
    km = kmean_ref[0, 0]
    for h in range(group):
        qh = q_ref[:, h * HEAD_DIM:(h + 1) * HEAD_DIM].astype(F32)
        g = lax.dot_general(km, qh, NT_DIMS, precision=HI, preferred_element_type=F32)[:nblk]
        selb_sc[h] = g

        def rank(m, cnt, h=h, g=g):
            gm = selb_sc[h, pl.ds(m, 1), :]
            beats = (gm > g) | ((gm == g) & (blk > m))
            return cnt + (beats & (own > m)).astype(jnp.int32)

        cnt = lax.fori_loop(0, (qi + 1) * per_tile - 1, rank, jnp.zeros((nblk, tq), jnp.int32))
        sel = (blk < own) & (cnt < MOBA_TOPK)
        selb_sc[h] = jnp.where(sel, 0.0, NEG_INF)

    slope2 = [slopes_ref[kvh * group + h] * LOG2E for h in range(group)]
    lane = lax.broadcasted_iota(jnp.int32, (1, LANES), 1)
    coefs = []
    for h in range(group):
        hi, mid, lo = _split3(jnp.zeros((1, LANES), F32) + slope2[h])
        coefs.append(jnp.where(lane < 6, jnp.where(lane % 3 == 0, hi, jnp.where(lane % 3 == 1, mid, lo)), 0.0))
    _store_wide_queries(qw_sc, q_ref, coefs)
    koff = lax.broadcasted_iota(jnp.int32, (tq, LANES), 0)
    klane = lax.broadcasted_iota(jnp.int32, (tq, LANES), 1)
    kx_sc[...] = jnp.where(klane < 3, (koff // MOBA_BLOCK) * MOBA_BLOCK,
                           jnp.where(klane < 6, koff % MOBA_BLOCK, 0)).astype(F32).astype(BF16)

    def logits_into(s_ref, ki):
        s_ref[...] = _qk_logits(k_ref[pl.ds(pl.multiple_of(ki * tq, tq), tq), :], kx_sc[...], qw_sc)

    def process(s_ref, ki, diagonal):
        parts = []
        for h in range(group):
            query_bias = -slope2[h] * (((qi - ki) * tq).astype(F32) + qcol.astype(F32))
            parts.append([])
            for c in range(per_tile):
                kb = ki * per_tile + c
                u = s_ref[c * MOBA_BLOCK:(c + 1) * MOBA_BLOCK, h * tq:(h + 1) * tq]
                selrow = selb_sc[h, pl.ds(kb, 1), :]
                if diagonal:
                    rel = (lax.broadcasted_iota(jnp.int32, (MOBA_BLOCK, tq), 1)
                           - lax.broadcasted_iota(jnp.int32, (MOBA_BLOCK, tq), 0))
                    u = jnp.where(rel >= c * MOBA_BLOCK, u, NEG_INF)
                    selrow = jnp.where(own > kb, selrow, 0.0)
                parts[h].append((u, selrow + query_bias))
        _softmax_tile_t(parts, v_ref[pl.ds(pl.multiple_of(ki * tq, tq), tq), :], m_sc, l_sc, acc_sc)

    _pipelined_tiles(qi, (s0_sc, s1_sc), logits_into, process)
    _write_heads(o_ref, l_sc, acc_sc, group)


def _moba_prompt(q16, q_col0, k16, v16, kmean, slopes, batch, seq, n_heads, n_kv):
    group = n_heads // n_kv
    tq = _row_tile(seq, ATTN_TILE)
    assert tq % MOBA_BLOCK == 0
    nq = seq // tq
    nblk = seq // MOBA_BLOCK
    gw = group * HEAD_DIM
    qb0 = q_col0 // gw
    return pl.pallas_call(
        functools.partial(_moba_prompt_kernel, group=group, nblk=nblk),
        grid=(batch, n_kv, nq),
        in_specs=[
            pl.BlockSpec(memory_space=pltpu.SMEM),
            pl.BlockSpec((tq, gw), lambda b, j, i: (b * nq + i, qb0 + j)),
            pl.BlockSpec((seq, HEAD_DIM), lambda b, j, i: (b, j)),
            pl.BlockSpec((seq, HEAD_DIM), lambda b, j, i: (b, j)),
            pl.BlockSpec((1, 1, LANES, HEAD_DIM), lambda b, j, i: (b, j, 0, 0)),
        ],
        out_specs=pl.BlockSpec((tq, gw), lambda b, j, i: (b * nq + i, j)),
        out_shape=jax.ShapeDtypeStruct((batch * seq, n_heads * HEAD_DIM), F32),
        scratch_shapes=[
            pltpu.VMEM((group, 1, tq), F32),
            pltpu.VMEM((group, 1, tq), F32),
            pltpu.VMEM((group, HEAD_DIM, tq), F32),
            pltpu.VMEM((group, nblk, tq), F32),
            pltpu.VMEM((group * tq, 2 * HEAD_DIM), BF16),
            pltpu.VMEM((tq, LANES), BF16),
            pltpu.VMEM((tq, group * tq), F32),
            pltpu.VMEM((tq, group * tq), F32),
        ],
        compiler_params=_params("parallel", "parallel", "arbitrary"),
        name="moba_prompt",
    )(slopes, q16, k16, v16, kmean)


def _fox_prompt_kernel(q_ref, k_ref, v_ref, ccol_ref, crow_ref, o_ref, m_sc, l_sc, acc_sc, qw_sc, kx_sc, s0_sc, s1_sc,
                       *, group):
    tq = q_ref.shape[0]
    seq = k_ref.shape[0]
    qi = pl.program_id(2)
    _init_softmax_state(m_sc, l_sc, acc_sc)
    lane = lax.broadcasted_iota(jnp.int32, (1, LANES), 1)
    _store_wide_queries(qw_sc, q_ref, [jnp.where(lane // 3 == h, -1.0, 0.0) for h in range(group)])

    @pl.when(qi == 0)
    def _():
        place = (lax.broadcasted_iota(jnp.int32, (group, LANES), 1) // 3
                 == lax.broadcasted_iota(jnp.int32, (group, LANES), 0)).astype(F32)
        x = jnp.dot(ccol_ref[0] * LOG2E, place, precision=HI, preferred_element_type=F32)
        hi, mid, lo = _split3(x)
        l3 = lax.broadcasted_iota(jnp.int32, (seq, LANES), 1) % 3
        kx_sc[...] = jnp.where(l3 == 0, hi, jnp.where(l3 == 1, mid, lo)).astype(BF16)

    def logits_into(s_ref, ki):
        off = pl.multiple_of(ki * tq, tq)
        s_ref[...] = _qk_logits(k_ref[pl.ds(off, tq), :], kx_sc[pl.ds(off, tq), :], qw_sc)

    def cq(h):
        return crow_ref[0, 0, h:h + 1, pl.ds(pl.multiple_of(qi * tq, tq), tq)] * LOG2E

    def process(s_ref, ki, diagonal):
        u = [s_ref[:, h * tq:(h + 1) * tq] for h in range(group)]
        if diagonal:
            krow = lax.broadcasted_iota(jnp.int32, (tq, tq), 0)
            qcol = lax.broadcasted_iota(jnp.int32, (tq, tq), 1)
            u = [jnp.where(qcol >= krow, uh, NEG_INF) for uh in u]
        _softmax_tile_t([[(u[h], cq(h))] for h in range(group)], v_ref[pl.ds(pl.multiple_of(ki * tq, tq), tq), :],
                        m_sc, l_sc, acc_sc)

    _pipelined_tiles(qi, (s0_sc, s1_sc), logits_into, process)
    _write_heads(o_ref, l_sc, acc_sc, group)


def _fox_prompt(q16, q_col0, k16, v16, c_col, c_row, batch, seq, n_heads, n_kv):
    group = n_heads // n_kv
    tq = _row_tile(seq, ATTN_TILE)
    nq = seq // tq
    gw = group * HEAD_DIM
    qb0 = q_col0 // gw
    return pl.pallas_call(
        functools.partial(_fox_prompt_kernel, group=group),
        grid=(batch, n_kv, nq),
        in_specs=[
            pl.BlockSpec((tq, gw), lambda b, j, i: (b * nq + i, qb0 + j)),
            pl.BlockSpec((seq, HEAD_DIM), lambda b, j, i: (b, j)),
            pl.BlockSpec((seq, HEAD_DIM), lambda b, j, i: (b, j)),
            pl.BlockSpec((1, seq, group), lambda b, j, i: (j, b, 0)),
            pl.BlockSpec((1, 1, group, seq), lambda b, j, i: (b, j, 0, 0)),
        ],
        out_specs=pl.BlockSpec((tq, gw), lambda b, j, i: (b * nq + i, j)),
        out_shape=jax.ShapeDtypeStruct((batch * seq, n_heads * HEAD_DIM), F32),
        scratch_shapes=[
            pltpu.VMEM((group, 1, tq), F32),
            pltpu.VMEM((group, 1, tq), F32),
            pltpu.VMEM((group, HEAD_DIM, tq), F32),
            pltpu.VMEM((group * tq, 2 * HEAD_DIM), BF16),
            pltpu.VMEM((seq, LANES), BF16),
            pltpu.VMEM((tq, group * tq), F32),
            pltpu.VMEM((tq, group * tq), F32),
        ],
        compiler_params=_params("arbitrary", "arbitrary", "arbitrary"),
        name="fox_prompt",
    )(q16, k16, v16, c_col, c_row)


def _cumsum_kernel(x_ref, o_ref, *, chunk):
    rows, seq = x_ref.shape[1:]
    r = lax.broadcasted_iota(jnp.int32, (chunk, chunk), 0)
    c = lax.broadcasted_iota(jnp.int32, (chunk, chunk), 1)
    upper = (r <= c).astype(F32)
    carry = jnp.zeros((rows, 1), F32)
    for i in range(seq // chunk):
        sl = slice(i * chunk, (i + 1) * chunk)
        cs = jnp.dot(x_ref[0, :, sl], upper, precision=HI, preferred_element_type=F32) + carry
        o_ref[0, :, sl] = cs
        carry = cs[:, chunk - 1:chunk]


def _cumsum_rows(x):
    batch, rows, seq = x.shape
    chunk = 256
    assert seq % chunk == 0
    return pl.pallas_call(
        functools.partial(_cumsum_kernel, chunk=chunk),
        grid=(batch,),
        in_specs=[pl.BlockSpec((1, rows, seq), lambda b: (b, 0, 0))],
        out_specs=pl.BlockSpec((1, rows, seq), lambda b: (b, 0, 0)),
        out_shape=jax.ShapeDtypeStruct(x.shape, F32),
        compiler_params=_params("parallel"),
        name="cumsum",
    )(x)


def _page_copies(pt_ref, row, slot, srcs, bufs, sems, n_pages, page):
    copies = []
    for p in range(n_pages):
        idx = pt_ref[row * n_pages + p]
        rows = pl.ds(p * page, page)
        for a, ((src, lead), buf) in enumerate(zip(srcs, bufs)):
            src_page = src.at[lead + (idx,)]
            if len(buf.shape) == 4:
                for j in range(buf.shape[1]):
                    copies.append(pltpu.make_async_copy(src_page.at[:, j, :], buf.at[slot, j, rows, :], sems.at[a, slot]))
            else:
                copies.append(pltpu.make_async_copy(src_page, buf.at[slot, rows], sems.at[a, slot]))
    return copies


def _fetch_rows(pt_ref, srcs, bufs, sems, n_pages, page):
    b = pl.program_id(0)
    nb = pl.num_programs(0)
    slot = lax.rem(b, 2)

    @pl.when(b == 0)
    def _():
        for c in _page_copies(pt_ref, b, slot, srcs, bufs, sems, n_pages, page):
            c.start()

    @pl.when(b + 1 < nb)
    def _():
        for c in _page_copies(pt_ref, b + 1, 1 - slot, srcs, bufs, sems, n_pages, page):
            c.start()

    for c in _page_copies(pt_ref, b, slot, srcs, bufs, sems, n_pages, page):
        c.wait()
    return slot


def _append_new_row(buf, slot, past, new):
    first = lax.broadcasted_iota(jnp.int32, (SUBLANES, HEAD_DIM), 0) == 0
    for j in range(new.shape[0]):
        buf[slot, j, past:past + SUBLANES, :] = jnp.where(first, new[j:j + 1, :], 0.0)


def _tail_bias(heads):
    first = lax.broadcasted_iota(jnp.int32, (SUBLANES, heads), 0) == 0
    return jnp.where(first, 0.0, NEG_INF)


def _group_queries(q, j, group):
    r = lax.broadcasted_iota(jnp.int32, q.shape, 0)
    return jnp.where(r // group == j, q, jnp.zeros_like(q))


def _decode_scores(kb, q, n_kv):
    group = q.shape[0] // n_kv
    s = None
    for j0 in range(0, n_kv, 2):
        js = range(j0, min(j0 + 2, n_kv))
        k2 = jnp.concatenate([kb[j].astype(BF16) for j in js], axis=1)
        q2 = jnp.concatenate([_group_queries(q, j, group) for j in js], axis=1)
        sj = lax.dot_general(k2, q2, NT_DIMS, preferred_element_type=F32)
        s = sj if s is None else s + sj
    return s


def _decode_softmax_pv(s, vb, n_kv):
    heads = s.shape[1]
    group = heads // n_kv
    m = jnp.max(s, axis=0, keepdims=True)
    p = jnp.exp2(s - m)
    l = jnp.sum(p, axis=0, keepdims=True)
    p16 = p.astype(BF16)
    o_t = jnp.concatenate(
        [lax.dot_general(vb[j].astype(BF16), p16, TN_DIMS, preferred_element_type=F32) for j in range(n_kv)],
        axis=0) / l
    o = o_t.T
    return jnp.concatenate(
        [o[h:h + 1, (h // group) * HEAD_DIM:(h // group + 1) * HEAD_DIM] for h in range(heads)], axis=0)


def _moba_decode_kernel(pt_ref, q_ref, knew_ref, vnew_ref, alibi_ref, k_hbm, v_hbm, o_ref, kbuf, vbuf, sems,
                        *, layer, n_kv, n_pages, page):
    slot = _fetch_rows(pt_ref, ((k_hbm, (layer,)), (v_hbm, (layer,))), (kbuf, vbuf), sems, n_pages, page)
    past = n_pages * page
    nblk = past // MOBA_BLOCK
    _append_new_row(kbuf, slot, past, knew_ref[0])
    _append_new_row(vbuf, slot, past, vnew_ref[0])
    kb, vb = kbuf.at[slot], vbuf.at[slot]
    q = q_ref[0]
    heads = q.shape[0]
    group = heads // n_kv

    g = None
    for j in range(n_kv):
        means = jnp.concatenate(
            [jnp.sum(kb[j, n * MOBA_BLOCK:(n + 1) * MOBA_BLOCK, :], axis=0, keepdims=True) for n in range(nblk)],
            axis=0) * (1.0 / MOBA_BLOCK)
        gj = lax.dot_general(means, _group_queries(q, j, group).astype(F32), NT_DIMS, precision=HI,
                             preferred_element_type=F32)
        g = gj if g is None else g + gj
    blk = lax.broadcasted_iota(jnp.int32, g.shape, 0)
    cnt = jnp.zeros(g.shape, jnp.int32)
    for m in range(nblk):
        gm = g[m:m + 1, :]
        cnt = cnt + ((gm > g) | ((gm == g) & (blk > m))).astype(jnp.int32)
    sel = cnt < MOBA_TOPK
    keep = jnp.concatenate([jnp.broadcast_to(sel[n:n + 1, :], (MOBA_BLOCK, heads)) for n in range(nblk)]
                           + [jnp.ones((SUBLANES, heads), jnp.bool_)], axis=0)
    s = jnp.where(keep, _decode_scores(kb, q, n_kv) + alibi_ref[...], NEG_INF)
    o_ref[0] = _decode_softmax_pv(s, vb, n_kv).astype(o_ref.dtype)


def _fox_decode_kernel(pt_ref, q_ref, knew_ref, vnew_ref, fnew_ref, k_hbm, v_hbm, c_hbm, o_ref, kbuf, vbuf, cbuf, sems,
                       *, layer, n_kv, n_pages, page):
    slot = _fetch_rows(pt_ref, ((k_hbm, (layer,)), (v_hbm, (layer,)), (c_hbm, ())), (kbuf, vbuf, cbuf), sems,
                       n_pages, page)
    past = n_pages * page
    _append_new_row(kbuf, slot, past, knew_ref[0])
    _append_new_row(vbuf, slot, past, vnew_ref[0])
    kb, vb = kbuf.at[slot], vbuf.at[slot]
    q = q_ref[0]
    heads = q.shape[0]

    carry = jnp.zeros((1, heads), F32)
    cums = []
    for j in range(n_pages):
        cs = cbuf[slot, j * page:(j + 1) * page, :] + carry
        cums.append(cs)
        carry = cs[page - 1:page, :]
    c_now = carry + fnew_ref[0]
    bias = [(c_now - cs) * LOG2E for cs in cums]
    s = _decode_scores(kb, q, n_kv) + jnp.concatenate(bias + [_tail_bias(heads)], axis=0)
    o_ref[0] = _decode_softmax_pv(s, vb, n_kv).astype(o_ref.dtype)


def _decode_attention(body, layer, page_table, q16, k_new, v_new, extra, extra_per_row, caches, n_kv):
    rows, n_pages = page_table.shape
    heads = q16.shape[1] // HEAD_DIM
    page = caches[0].shape[2]
    past = n_pages * page
    assert past % MOBA_BLOCK == 0
    row_spec = lambda shape: pl.BlockSpec((1,) + shape, lambda b, pt: (b, 0, 0))
    in_specs = [row_spec((heads, HEAD_DIM)), row_spec((n_kv, HEAD_DIM)), row_spec((n_kv, HEAD_DIM))]
    if extra_per_row:
        in_specs.append(row_spec((1, heads)))
        extra = extra.reshape(rows, 1, heads)
    else:
        in_specs.append(pl.BlockSpec(extra.shape, lambda b, pt: (0, 0)))
    in_specs += [pl.BlockSpec(memory_space=pl.ANY)] * len(caches)
    scratch = [pltpu.VMEM((2, n_kv, past + SUBLANES, HEAD_DIM) if c.ndim == 5 else (2, past, c.shape[2]), F32)
               for c in caches]
    scratch.append(pltpu.SemaphoreType.DMA((len(caches), 2)))
    return pl.pallas_call(
        functools.partial(body, layer=layer, n_kv=n_kv, n_pages=n_pages, page=page),
        grid_spec=pltpu.PrefetchScalarGridSpec(
            num_scalar_prefetch=1,
            grid=(rows,),
            in_specs=in_specs,
            out_specs=pl.BlockSpec((1, heads, HEAD_DIM), lambda b, pt: (b, 0, 0)),
            scratch_shapes=scratch,
        ),
        out_shape=jax.ShapeDtypeStruct((rows, heads, HEAD_DIM), F32),
        compiler_params=_params("arbitrary"),
        name="decode_attention",
    )(page_table.reshape(-1), q16.reshape(rows, heads, HEAD_DIM), k_new, v_new, extra, *caches
      ).reshape(rows, heads * HEAD_DIM)


def _page_cumsum_kernel(x_ref, o_ref):
    page = x_ref.shape[1]
    r = lax.broadcasted_iota(jnp.int32, (page, page), 0)
    c = lax.broadcasted_iota(jnp.int32, (page, page), 1)
    o_ref[...] = jnp.dot(x_ref[...], (r <= c).astype(F32), precision=HI, preferred_element_type=F32)


def _page_cumsum(logf_cache):
    pool, page, heads = logf_cache.shape
    x = logf_cache.transpose(0, 2, 1).reshape(pool * heads, page)
    tm = _row_tile(pool * heads, 2048)
    out = pl.pallas_call(
        _page_cumsum_kernel,
        grid=(pool * heads // tm,),
        in_specs=[pl.BlockSpec((tm, page), lambda i: (i, 0))],
        out_specs=pl.BlockSpec((tm, page), lambda i: (i, 0)),
        out_shape=jax.ShapeDtypeStruct((pool * heads, page), F32),
        compiler_params=_params("parallel"),
        name="page_cumsum",
    )(x)
    return out.reshape(pool, heads, page).transpose(0, 2, 1)


def _merge_kernel(x_ref, oa_ref, ob_ref, z_ref, g_ref, wa_ref, wb_ref, wo_ref, y_ref):
    wa_w = oa_ref.shape[1]
    d = x_ref.shape[1]

    def branch(o_ref, z, w_ref):
        u = o_ref[...] * (z * jax.nn.sigmoid(z))
        return jnp.dot(u.astype(BF16), w_ref[...], preferred_element_type=F32)

    ya = branch(oa_ref, z_ref[:, :wa_w], wa_ref)
    yb = branch(ob_ref, z_ref[:, wa_w:], wb_ref)
    mixed = g_ref[:, :d] * ya + g_ref[:, d:] * yb
    y_ref[...] = x_ref[...] + jnp.dot(mixed.astype(BF16), wo_ref[...], preferred_element_type=F32)


def _merge(x, oa, ob, z, gates, wa, wb, wo):
    m, d = x.shape
    tm = _row_tile(m, 256)
    rows = lambda width: pl.BlockSpec((tm, width), lambda i: (i, 0))
    resident = lambda w: pl.BlockSpec(w.shape, lambda i: (0, 0), pipeline_mode=pl.Buffered(1))
    return pl.pallas_call(
        _merge_kernel,
        grid=(m // tm,),
        in_specs=[rows(d), rows(oa.shape[1]), rows(ob.shape[1]), rows(z.shape[1]), rows(gates.shape[1]),
                  resident(wa), resident(wb), resident(wo)],
        out_specs=rows(d),
        out_shape=jax.ShapeDtypeStruct((m, d), F32),
        compiler_params=_params("parallel"),
        name="merge",
    )(x, oa, ob, z, gates, wa, wb, wo)


def _project(x2d, g_norm, wt, wts, dims):
    d, wa_w, wb_w, kva, kvb, hb = dims
    n_kv_a, n_kv_b = kva // HEAD_DIM, kvb // HEAD_DIM
    tile = wa_w
    assert wa_w == wb_w == 2 * kva == 2 * kvb
    branch = 2 * wa_w + 2 * kva
    f_row = 2 * branch
    h, ka, va, ka16, va16 = _proj(functools.partial(_proj_kv_kernel, n_kv=n_kv_a, n_forget=0), x2d, wt, (wa_w, 0, 1),
                                  tile, [(wts["g_ka"], tile, 0)],
                                  [(F32, (n_kv_a, HEAD_DIM)), (F32, (n_kv_a, HEAD_DIM)), (BF16, (kva,)),
                                   (BF16, (kva,))], norm_gain=g_norm.reshape(1, d))
    (q16,) = _proj(_proj_q_kernel, h, wt, (0, branch, 2), tile, [(wts["g_q"], tile, 0)], [(BF16, (wa_w + wb_w,))])
    kb, vb, kb16, vb16, logf = _proj(functools.partial(_proj_kv_kernel, n_kv=n_kv_b, n_forget=hb), h, wt,
                                     (branch + wb_w, 0, 1), tile, [(wts["g_kb"], tile, 0), (wts["b_f"], LANES, 0)],
                                     [(F32, (n_kv_b, HEAD_DIM)), (F32, (n_kv_b, HEAD_DIM)), (BF16, (kvb,)),
                                      (BF16, (kvb,)), (F32, (hb,))], forget_row=f_row)
    (z,) = _proj(_proj_plain_kernel, h, wt, (wa_w + 2 * kva, branch, 2), tile, [], [(F32, (wa_w + wb_w,))])
    gate_tile = min(tile, 2 * d)
    (gates,) = _proj(_proj_gate_kernel, h, wt, (f_row + hb, gate_tile, 2 * d // gate_tile), gate_tile,
                     [(wts["b_g"], gate_tile, 0)], [(F32, (2 * d,))])
    return dict(q16=q16, ka=ka, va=va, ka16=ka16, va16=va16, kb=kb, vb=vb, kb16=kb16, vb16=vb16, logf=logf, z=z,
                gates=gates)


def _layer_weights(b_forget, b_merge, gq_a, gk_a, gq_b, gk_b, wa, wb, wo, dims):
    d, wa_w, wb_w, kva, kvb, hb = dims
    return dict(
        g_q=jnp.concatenate([jnp.tile(gq_a, wa_w // HEAD_DIM), jnp.tile(gq_b, wb_w // HEAD_DIM)]).reshape(1, -1),
        g_ka=jnp.concatenate([jnp.tile(gk_a, kva // HEAD_DIM), jnp.zeros((kva,), F32)]).reshape(1, -1),
        g_kb=jnp.concatenate([jnp.tile(gk_b, kvb // HEAD_DIM), jnp.zeros((kvb,), F32)]).reshape(1, -1),
        b_f=jnp.concatenate([b_forget, jnp.zeros((LANES - hb,), F32)]).reshape(1, -1),
        b_g=b_merge.reshape(1, 2 * d),
        wa=wa.astype(BF16), wb=wb.astype(BF16), wo=wo.astype(BF16),
    )


def kernel(x_prompt, x_sample, cache_k_moba, cache_v_moba, cache_k_fox, cache_v_fox, cache_logf_fox, page_table,
           g_norm, w_in, b_forget, b_merge, gq_moba, gk_moba, gq_fox, gk_fox, w_branch_moba, w_branch_fox, w_out):
    depth = w_in.shape[0]
    batch, seq, d = x_prompt.shape
    rows, dec_seq, _ = x_sample.shape
    assert dec_seq == 1 and seq % MOBA_BLOCK == 0
    n_kv_a, n_kv_b = cache_k_moba.shape[3], cache_k_fox.shape[3]
    hb = b_forget.shape[1]
    wa_w, wb_w = w_branch_moba.shape[1], w_branch_fox.shape[1]
    ha = wa_w // HEAD_DIM
    assert wb_w == hb * HEAD_DIM
    kva, kvb = n_kv_a * HEAD_DIM, n_kv_b * HEAD_DIM
    slopes = 2.0 ** (-8.0 * jnp.arange(1, ha + 1, dtype=F32) / ha)
    past = page_table.shape[1] * cache_k_moba.shape[2]
    pos = jnp.arange(past + SUBLANES)[:, None]
    alibi_decode = jnp.where(pos < past, (-LOG2E) * slopes[None, :] * (past - pos).astype(F32),
                             jnp.where(pos == past, 0.0, NEG_INF))

    xp = x_prompt.reshape(batch * seq, d)
    xs = x_sample.reshape(rows, d)
    outs_p, outs_s = [], []
    for l in range(depth):
        dims = (d, wa_w, wb_w, kva, kvb, hb)
        wt = w_in[l].T
        wts = _layer_weights(b_forget[l], b_merge[l], gq_moba[l], gk_moba[l], gq_fox[l], gk_fox[l],
                             w_branch_moba[l], w_branch_fox[l], w_out[l], dims)
        pp = _project(xp, g_norm[l], wt, wts, dims)
        kmean = _kmean(pp["ka"], batch, seq)
        oa = _moba_prompt(pp["q16"], 0, pp["ka16"], pp["va16"], kmean, slopes, batch, seq, ha, n_kv_a)
        group_b = hb // n_kv_b
        c_row = _cumsum_rows(pp["logf"].reshape(batch, seq, hb).transpose(0, 2, 1))
        c_col = c_row.reshape(batch, n_kv_b, group_b, seq).transpose(1, 0, 3, 2).reshape(n_kv_b, batch * seq, group_b)
        ob = _fox_prompt(pp["q16"], wa_w, pp["kb16"], pp["vb16"], c_col,
                         c_row.reshape(batch, n_kv_b, group_b, seq), batch, seq, hb, n_kv_b)
        xp = _merge(xp, oa, ob, pp["z"], pp["gates"], wts["wa"], wts["wb"], wts["wo"])
        outs_p.append((pp["ka"], pp["va"], pp["kb"], pp["vb"], pp["logf"]))
        ps = _project(xs, g_norm[l], wt, wts, dims)
        oa_s = _decode_attention(_moba_decode_kernel, l, page_table, ps["q16"][:, :wa_w], ps["ka"], ps["va"],
                                 alibi_decode, False, [cache_k_moba, cache_v_moba], n_kv_a)
        ob_s = _decode_attention(_fox_decode_kernel, l, page_table, ps["q16"][:, wa_w:], ps["kb"], ps["vb"],
                                 ps["logf"], True, [cache_k_fox, cache_v_fox, _page_cumsum(cache_logf_fox[l])], n_kv_b)
        xs = _merge(xs, oa_s, ob_s, ps["z"], ps["gates"], wts["wa"], wts["wb"], wts["wo"])
        outs_s.append((ps["ka"], ps["va"], ps["kb"], ps["vb"], ps["logf"]))

    def stacked(outs, i, shape):
        return jnp.stack([o[i].reshape(shape) for o in outs])

    res = [xp.reshape(batch, seq, d), xs.reshape(rows, 1, d)]
    for outs, lead in ((outs_p, (batch, seq)), (outs_s, (rows, 1))):
        res += [stacked(outs, 0, lead + (n_kv_a, HEAD_DIM)), stacked(outs, 1, lead + (n_kv_a, HEAD_DIM)),
                stacked(outs, 2, lead + (n_kv_b, HEAD_DIM)), stacked(outs, 3, lead + (n_kv_b, HEAD_DIM)),
                stacked(outs, 4, lead + (hb,))]
    return tuple(res)
```

```python
import functools

import jax
import jax.numpy as jnp
from jax import lax
from jax.experimental import pallas as pl
from jax.experimental.pallas import tpu as pltpu

HEAD_DIM = 128
MOBA_BLOCK = 256
MOBA_TOPK = 3
NORM_EPS = 1e-6
NEG_INF = -1e30
M_INIT = 0.5 * NEG_INF
SCALE = HEAD_DIM ** -0.5
LOG2E = 1.4426950408889634
LANES = 128
SUBLANES = 8
VMEM_LIMIT = 52 * 1024 * 1024
ATTN_TILE = 512

F32 = jnp.float32
BF16 = jnp.bfloat16
HI = lax.Precision.HIGHEST
NT_DIMS = (((1,), (1,)), ((), ()))
TN_DIMS = (((0,), (0,)), ((), ()))


def _params(*sem):
    return pltpu.CompilerParams(dimension_semantics=sem, vmem_limit_bytes=VMEM_LIMIT)


def _row_tile(m, want):
    t = min(m, want)
    assert m % t == 0
    return t


def _head_norm(a, g):
    ms = jnp.mean(a * a, axis=-1, keepdims=True)
    return a * lax.rsqrt(ms + NORM_EPS) * g


def _log_sigmoid(x):
    return jnp.minimum(x, 0.0) - jnp.log1p(jnp.exp(-jnp.abs(x)))


def _proj_q_kernel(h_ref, w_ref, g_ref, o_ref):
    acc = lax.dot_general(h_ref[...], w_ref[...], NT_DIMS, preferred_element_type=F32)
    for j in range(acc.shape[1] // HEAD_DIM):
        sl = slice(j * HEAD_DIM, (j + 1) * HEAD_DIM)
        o_ref[:, sl] = (_head_norm(acc[:, sl], g_ref[:, sl]) * (SCALE * LOG2E)).astype(o_ref.dtype)


def _proj_kv_kernel(h_ref, w_ref, g_ref, *refs, n_kv, n_forget):
    if n_forget:
        wf_ref, bf_ref, *refs = refs
    k32_hbm, v32_hbm, k16_ref, v16_ref, *refs = refs
    if n_forget:
        logf_ref, *refs = refs
    kbuf, vbuf, sems = refs
    kvw = n_kv * HEAD_DIM
    tm = h_ref.shape[0]
    i = pl.program_id(1)
    last = pl.num_programs(1) - 1
    slot = lax.rem(i, 2)

    def writebacks(step, slot):
        rows = pl.ds(pl.multiple_of(step * tm, tm), tm)
        return [pltpu.make_async_copy(buf.at[slot, :, j * HEAD_DIM:(j + 1) * HEAD_DIM], dst.at[rows, j, :],
                                      sems.at[a, slot])
                for a, (buf, dst) in enumerate(((kbuf, k32_hbm), (vbuf, v32_hbm))) for j in range(n_kv)]

    @pl.when(i >= 2)
    def _():
        for c in writebacks(i - 2, slot):
            c.wait()

    h = h_ref[...]
    acc = lax.dot_general(h, w_ref[...], NT_DIMS, preferred_element_type=F32)
    for j in range(n_kv):
        sl = slice(j * HEAD_DIM, (j + 1) * HEAD_DIM)
        kn = _head_norm(acc[:, sl], g_ref[:, sl])
        kbuf[slot, :, sl] = kn
        k16_ref[:, sl] = kn.astype(BF16)
        v = acc[:, kvw + j * HEAD_DIM:kvw + (j + 1) * HEAD_DIM]
        vbuf[slot, :, sl] = v
        v16_ref[:, sl] = v.astype(BF16)
    if n_forget:
        f = lax.dot_general(h, wf_ref[...].astype(BF16), NT_DIMS, preferred_element_type=F32)
        logf_ref[...] = _log_sigmoid(f[:, :n_forget] + bf_ref[:, :n_forget])
    for c in writebacks(i, slot):
        c.start()

    @pl.when((i == last) & (i >= 1))
    def _():
        for c in writebacks(i - 1, 1 - slot):
            c.wait()

    @pl.when(i == last)
    def _():
        for c in writebacks(i, slot):
            c.wait()


def _proj_plain_kernel(h_ref, w_ref, o_ref):
    o_ref[...] = lax.dot_general(h_ref[...], w_ref[...], NT_DIMS, preferred_element_type=F32).astype(o_ref.dtype)


def _proj_gate_kernel(h_ref, w_ref, b_ref, o_ref):
    acc = lax.dot_general(h_ref[...], w_ref[...], NT_DIMS, preferred_element_type=F32)
    o_ref[...] = jax.nn.sigmoid(acc + b_ref[...]).astype(o_ref.dtype)


def _with_cached_weights(body, h_ref, w_ref, *refs):
    *refs, w16_sc = refs

    @pl.when(pl.program_id(1) == 0)
    def _():
        w16_sc[...] = w_ref[...].astype(BF16)

    body(h_ref, w16_sc, *refs)


def _with_rmsnorm(body, n_inputs, x_ref, w_ref, gn_ref, *refs):
    ins, (h_ref, *rest) = refs[:n_inputs], refs[n_inputs:]
    x = x_ref[...]
    ms = jnp.mean(x * x, axis=-1, keepdims=True)
    h_ref[...] = (x * lax.rsqrt(ms + NORM_EPS) * gn_ref[...]).astype(h_ref.dtype)
    body(h_ref, w_ref, *ins, *rest)


def _weight_rows(wt, n_rows, first, stride=0):
    return pl.BlockSpec((pl.Element(n_rows), pl.Element(wt.shape[1])),
                        lambda j, i: (pl.multiple_of(first + stride * j, SUBLANES), 0))


def _proj(body, h, wt, rows, tn, extras, outs, forget_row=None, norm_gain=None):
    m, d = h.shape
    first, stride, count = rows
    assert first % SUBLANES == 0 and stride % SUBLANES == 0
    tm = _row_tile(m, 1024 if norm_gain is None else 512)
    in_specs = [pl.BlockSpec((tm, d), lambda j, i: (i, 0)), _weight_rows(wt, tn, first, stride)]
    args = [h, wt]
    if norm_gain is not None:
        assert count == 1
        in_specs.append(pl.BlockSpec((1, d), lambda j, i: (0, 0)))
        args.append(norm_gain)
        outs = [(BF16, (d,))] + list(outs)
        body = functools.partial(_with_rmsnorm, body, len(extras) + (forget_row is not None))
    for arr, width, base in extras[:1]:
        in_specs.append(pl.BlockSpec((arr.shape[0], width), lambda j, i, base=base: (0, base + j)))
        args.append(arr)
    if forget_row is not None:
        in_specs.append(_weight_rows(wt, LANES, forget_row))
        args.append(wt)
    for arr, width, base in extras[1:]:
        in_specs.append(pl.BlockSpec((arr.shape[0], width), lambda j, i, base=base: (0, base + j)))
        args.append(arr)
    out_specs, out_shape, staging = [], [], []
    for dt, trail in outs:
        if len(trail) == 2:
            assert count == 1
            out_specs.append(pl.BlockSpec(memory_space=pl.ANY))
            staging.append(pltpu.VMEM((2, tm, trail[0] * trail[1]), dt))
        else:
            out_specs.append(pl.BlockSpec((tm, trail[0] // count), lambda j, i: (i, j)))
        out_shape.append(jax.ShapeDtypeStruct((m,) + trail, dt))
    if staging:
        staging.append(pltpu.SemaphoreType.DMA((len(staging), 2)))
    return pl.pallas_call(
        functools.partial(_with_cached_weights, body),
        grid=(count, m // tm),
        in_specs=in_specs,
        out_specs=out_specs,
        out_shape=out_shape,
        scratch_shapes=staging + [pltpu.VMEM((tn, d), BF16)],
        compiler_params=_params("parallel", "arbitrary"),
        name="proj",
    )(*args)


def _kmean_kernel(k_ref, o_ref, *, nblk, n_kv):
    o_ref[...] = jnp.zeros_like(o_ref)
    for j in range(n_kv):
        for n in range(nblk):
            blk = k_ref[n * MOBA_BLOCK:(n + 1) * MOBA_BLOCK, j, :]
            o_ref[0, j, n:n + 1, :] = jnp.sum(blk, axis=0, keepdims=True) * (1.0 / MOBA_BLOCK)


def _kmean(k32, batch, seq):
    n_kv = k32.shape[1]
    nblk = seq // MOBA_BLOCK
    assert nblk <= LANES
    return pl.pallas_call(
        functools.partial(_kmean_kernel, nblk=nblk, n_kv=n_kv),
        grid=(batch,),
        in_specs=[pl.BlockSpec((seq, n_kv, HEAD_DIM), lambda b: (b, 0, 0))],
        out_specs=pl.BlockSpec((1, n_kv, LANES, HEAD_DIM), lambda b: (b, 0, 0, 0)),
        out_shape=jax.ShapeDtypeStruct((batch, n_kv, LANES, HEAD_DIM), F32),
        compiler_params=_params("parallel"),
        name="kmean",
    )(k32)


def _split3(x):
    hi = x.astype(BF16).astype(F32)
    r = x - hi
    mid = r.astype(BF16).astype(F32)
    return hi, mid, (r - mid).astype(BF16).astype(F32)


def _store_wide_queries(qw_sc, q_ref, coef_rows):
    tq = q_ref.shape[0]
    for h, coef in enumerate(coef_rows):
        qw_sc[h * tq:(h + 1) * tq, :HEAD_DIM] = q_ref[:, h * HEAD_DIM:(h + 1) * HEAD_DIM]
        qw_sc[h * tq:(h + 1) * tq, HEAD_DIM:] = jnp.broadcast_to(coef.astype(BF16), (tq, LANES))


def _qk_logits(k, kx, qw_sc):
    return lax.dot_general(jnp.concatenate([k, kx], axis=1), qw_sc[...], NT_DIMS, preferred_element_type=F32)


def _pipelined_tiles(qi, s_refs, logits_into, process):
    s0, s1 = s_refs
    logits_into(s0, 0)

    def pair(kp, carry):
        t = 2 * kp
        logits_into(s1, t + 1)
        process(s0, t, False)
        logits_into(s0, t + 2)
        process(s1, t + 1, False)
        return carry

    lax.fori_loop(0, qi // 2, pair, 0)

    @pl.when(qi % 2 == 0)
    def _():
        process(s0, qi, True)

    @pl.when(qi % 2 == 1)
    def _():
        logits_into(s1, qi)
        process(s0, qi - 1, False)
        process(s1, qi, True)


def _softmax_tile_t(head_parts, v, m_sc, l_sc, acc_sc):
    heads = range(len(head_parts))
    tq = head_parts[0][0][0].shape[1]
    m_prev = [m_sc[h] for h in heads]
    m_new = []
    for h in heads:
        m = m_prev[h]
        for u, rb in head_parts[h]:
            m = jnp.maximum(m, jnp.max(u, axis=0, keepdims=True) + rb)
        m_new.append(m)
    alphas, p16s = [], []
    for h in heads:
        alpha = jnp.exp2(m_prev[h] - m_new[h])
        l_new = alpha * l_sc[h]
        ps = []
        for u, rb in head_parts[h]:
            p = jnp.exp2(u - (m_new[h] - rb))
            l_new = l_new + jnp.sum(p, axis=0, keepdims=True)
            ps.append(p.astype(v.dtype))
        p16s.append(ps[0] if len(ps) == 1 else jnp.concatenate(ps, axis=0))
        alphas.append(alpha)
        l_sc[h] = l_new
        m_sc[h] = m_new[h]
    pv = lax.dot_general(v, jnp.concatenate(p16s, axis=1), TN_DIMS, preferred_element_type=F32)
    for h in heads:
        acc_sc[h] = alphas[h] * acc_sc[h] + pv[:, h * tq:(h + 1) * tq]


def _init_softmax_state(m_sc, l_sc, acc_sc):
    m_sc[...] = jnp.full(m_sc.shape, M_INIT, F32)
    l_sc[...] = jnp.zeros(l_sc.shape, F32)
    acc_sc[...] = jnp.zeros(acc_sc.shape, F32)


def _write_heads(o_ref, l_sc, acc_sc, group):
    for h in range(group):
        o_ref[:, h * HEAD_DIM:(h + 1) * HEAD_DIM] = (acc_sc[h] / l_sc[h]).T.astype(o_ref.dtype)


def _moba_prompt_kernel(slopes_ref, q_ref, k_ref, v_ref, kmean_ref, o_ref, m_sc, l_sc, acc_sc, selb_sc, qw_sc, kx_sc,
                        s0_sc, s1_sc, *, group, nblk):
    tq = q_ref.shape[0]
    per_tile = tq // MOBA_BLOCK
    kvh = pl.program_id(1)
    qi = pl.program_id(2)
    _init_softmax_state(m_sc, l_sc, acc_sc)

    qcol = lax.broadcasted_iota(jnp.int32, (1, tq), 1)
    own = qi * per_tile + qcol // MOBA_BLOCK
    blk = lax.broadcasted_iota(jnp.int32, (nblk, tq), 0)
    km = kmean_ref[0, 0, :-(-nblk // SUBLANES) * SUBLANES, :]
    for h in range(group):
        qh = q_ref[:, h * HEAD_DIM:(h + 1) * HEAD_DIM].astype(F32)
        g = lax.dot_general(km, qh, NT_DIMS, precision=HI, preferred_element_type=F32)[:nblk]
        cnt = jnp.zeros((nblk, tq), jnp.int32)
        for m in range(nblk):
            gm = g[m:m + 1, :]
            beats = (gm > g) | ((gm == g) & (blk > m))
            cnt = cnt + (beats & (own > m)).astype(jnp.int32)
        sel = (blk < own) & (cnt < MOBA_TOPK)
        selb_sc[h] = jnp.where(sel, 0.0, NEG_INF)

    slope2 = [slopes_ref[kvh * group + h] * LOG2E for h in range(group)]
    lane = lax.broadcasted_iota(jnp.int32, (1, LANES), 1)
    coefs = []
    for h in range(group):
        hi, mid, lo = _split3(jnp.zeros((1, LANES), F32) + slope2[h])
        coefs.append(jnp.where(lane < 6, jnp.where(lane % 3 == 0, hi, jnp.where(lane % 3 == 1, mid, lo)), 0.0))
    _store_wide_queries(qw_sc, q_ref, coefs)
    koff = lax.broadcasted_iota(jnp.int32, (tq, LANES), 0)
    klane = lax.broadcasted_iota(jnp.int32, (tq, LANES), 1)
    kx_sc[...] = jnp.where(klane < 3, (koff // MOBA_BLOCK) * MOBA_BLOCK,
                           jnp.where(klane < 6, koff % MOBA_BLOCK, 0)).astype(F32).astype(BF16)

    def logits_into(s_ref, ki):
        s_ref[...] = _qk_logits(k_ref[pl.ds(pl.multiple_of(ki * tq, tq), tq), :], kx_sc[...], qw_sc)

    def process(s_ref, ki, diagonal):
        parts = []
        for h in range(group):
            query_bias = -slope2[h] * (((qi - ki) * tq).astype(F32) + qcol.astype(F32))
            parts.append([])
            for c in range(per_tile):
                kb = ki * per_tile + c
                u = s_ref[c * MOBA_BLOCK:(c + 1) * MOBA_BLOCK, h * tq:(h + 1) * tq]
                selrow = selb_sc[h, pl.ds(kb, 1), :]
                if diagonal:
                    rel = (lax.broadcasted_iota(jnp.int32, (MOBA_BLOCK, tq), 1)
                           - lax.broadcasted_iota(jnp.int32, (MOBA_BLOCK, tq), 0))
                    u = jnp.where(rel >= c * MOBA_BLOCK, u, NEG_INF)
                    selrow = jnp.where(own > kb, selrow, 0.0)
                parts[h].append((u, selrow + query_bias))
        _softmax_tile_t(parts, v_ref[pl.ds(pl.multiple_of(ki * tq, tq), tq), :], m_sc, l_sc, acc_sc)

    _pipelined_tiles(qi, (s0_sc, s1_sc), logits_into, process)
    _write_heads(o_ref, l_sc, acc_sc, group)


def _moba_prompt(q16, q_col0, k16, v16, kmean, slopes, batch, seq, n_heads, n_kv):
    group = n_heads // n_kv
    tq = _row_tile(seq, ATTN_TILE)
    assert tq % MOBA_BLOCK == 0
    nq = seq // tq
    nblk = seq // MOBA_BLOCK
    gw = group * HEAD_DIM
    qb0 = q_col0 // gw
    return pl.pallas_call(
        functools.partial(_moba_prompt_kernel, group=group, nblk=nblk),
        grid=(batch, n_kv, nq),
        in_specs=[
            pl.BlockSpec(memory_space=pltpu.SMEM),
            pl.BlockSpec((tq, gw), lambda b, j, i: (b * nq + i, qb0 + j)),
            pl.BlockSpec((seq, HEAD_DIM), lambda b, j, i: (b, j)),
            pl.BlockSpec((seq, HEAD_DIM), lambda b, j, i: (b, j)),
            pl.BlockSpec((1, 1, LANES, HEAD_DIM), lambda b, j, i: (b, j, 0, 0)),
        ],
        out_specs=pl.BlockSpec((tq, gw), lambda b, j, i: (b * nq + i, j)),
        out_shape=jax.ShapeDtypeStruct((batch * seq, n_heads * HEAD_DIM), F32),
        scratch_shapes=[
            pltpu.VMEM((group, 1, tq), F32),
            pltpu.VMEM((group, 1, tq), F32),
            pltpu.VMEM((group, HEAD_DIM, tq), F32),
            pltpu.VMEM((group, nblk, tq), F32),
            pltpu.VMEM((group * tq, 2 * HEAD_DIM), BF16),
            pltpu.VMEM((tq, LANES), BF16),
            pltpu.VMEM((tq, group * tq), F32),
            pltpu.VMEM((tq, group * tq), F32),
        ],
        compiler_params=_params("parallel", "parallel", "arbitrary"),
        name="moba_prompt",
    )(slopes, q16, k16, v16, kmean)


def _fox_prompt_kernel(q_ref, k_ref, v_ref, ccol_ref, crow_ref, o_ref, m_sc, l_sc, acc_sc, qw_sc, kx_sc, s0_sc, s1_sc,
                       *, group):
    tq = q_ref.shape[0]
    seq = k_ref.shape[0]
    qi = pl.program_id(2)
    _init_softmax_state(m_sc, l_sc, acc_sc)
    lane = lax.broadcasted_iota(jnp.int32, (1, LANES), 1)
    _store_wide_queries(qw_sc, q_ref, [jnp.where(lane // 3 == h, -1.0, 0.0) for h in range(group)])

    @pl.when(qi == 0)
    def _():
        place = (lax.broadcasted_iota(jnp.int32, (group, LANES), 1) // 3
                 == lax.broadcasted_iota(jnp.int32, (group, LANES), 0)).astype(F32)
        x = jnp.dot(ccol_ref[0] * LOG2E, place, precision=HI, preferred_element_type=F32)
        hi, mid, lo = _split3(x)
        l3 = lax.broadcasted_iota(jnp.int32, (seq, LANES), 1) % 3
        kx_sc[...] = jnp.where(l3 == 0, hi, jnp.where(l3 == 1, mid, lo)).astype(BF16)

    def logits_into(s_ref, ki):
        off = pl.multiple_of(ki * tq, tq)
        s_ref[...] = _qk_logits(k_ref[pl.ds(off, tq), :], kx_sc[pl.ds(off, tq), :], qw_sc)

    def cq(h):
        return crow_ref[0, 0, h:h + 1, pl.ds(pl.multiple_of(qi * tq, tq), tq)] * LOG2E

    def process(s_ref, ki, diagonal):
        u = [s_ref[:, h * tq:(h + 1) * tq] for h in range(group)]
        if diagonal:
            krow = lax.broadcasted_iota(jnp.int32, (tq, tq), 0)
            qcol = lax.broadcasted_iota(jnp.int32, (tq, tq), 1)
            u = [jnp.where(qcol >= krow, uh, NEG_INF) for uh in u]
        _softmax_tile_t([[(u[h], cq(h))] for h in range(group)], v_ref[pl.ds(pl.multiple_of(ki * tq, tq), tq), :],
                        m_sc, l_sc, acc_sc)

    _pipelined_tiles(qi, (s0_sc, s1_sc), logits_into, process)
    _write_heads(o_ref, l_sc, acc_sc, group)


def _fox_prompt(q16, q_col0, k16, v16, c_col, c_row, batch, seq, n_heads, n_kv):
    group = n_heads // n_kv
    tq = _row_tile(seq, ATTN_TILE)
    nq = seq // tq
    gw = group * HEAD_DIM
    qb0 = q_col0 // gw
    return pl.pallas_call(
        functools.partial(_fox_prompt_kernel, group=group),
        grid=(batch, n_kv, nq),
        in_specs=[
            pl.BlockSpec((tq, gw), lambda b, j, i: (b * nq + i, qb0 + j)),
            pl.BlockSpec((seq, HEAD_DIM), lambda b, j, i: (b, j)),
            pl.BlockSpec((seq, HEAD_DIM), lambda b, j, i: (b, j)),
            pl.BlockSpec((1, seq, group), lambda b, j, i: (j, b, 0)),
            pl.BlockSpec((1, 1, group, seq), lambda b, j, i: (b, j, 0, 0)),
        ],
        out_specs=pl.BlockSpec((tq, gw), lambda b, j, i: (b * nq + i, j)),
        out_shape=jax.ShapeDtypeStruct((batch * seq, n_heads * HEAD_DIM), F32),
        scratch_shapes=[
            pltpu.VMEM((group, 1, tq), F32),
            pltpu.VMEM((group, 1, tq), F32),
            pltpu.VMEM((group, HEAD_DIM, tq), F32),
            pltpu.VMEM((group * tq, 2 * HEAD_DIM), BF16),
            pltpu.VMEM((seq, LANES), BF16),
            pltpu.VMEM((tq, group * tq), F32),
            pltpu.VMEM((tq, group * tq), F32),
        ],
        compiler_params=_params("arbitrary", "arbitrary", "arbitrary"),
        name="fox_prompt",
    )(q16, k16, v16, c_col, c_row)


def _cumsum_kernel(x_ref, o_ref, *, chunk):
    rows, seq = x_ref.shape[1:]
    r = lax.broadcasted_iota(jnp.int32, (chunk, chunk), 0)
    c = lax.broadcasted_iota(jnp.int32, (chunk, chunk), 1)
    upper = (r <= c).astype(F32)
    carry = jnp.zeros((rows, 1), F32)
    for i in range(seq // chunk):
        sl = slice(i * chunk, (i + 1) * chunk)
        cs = jnp.dot(x_ref[0, :, sl], upper, precision=HI, preferred_element_type=F32) + carry
        o_ref[0, :, sl] = cs
        carry = cs[:, chunk - 1:chunk]


def _cumsum_rows(x):
    batch, rows, seq = x.shape
    chunk = 256
    assert seq % chunk == 0
    return pl.pallas_call(
        functools.partial(_cumsum_kernel, chunk=chunk),
        grid=(batch,),
        in_specs=[pl.BlockSpec((1, rows, seq), lambda b: (b, 0, 0))],
        out_specs=pl.BlockSpec((1, rows, seq), lambda b: (b, 0, 0)),
        out_shape=jax.ShapeDtypeStruct(x.shape, F32),
        compiler_params=_params("parallel"),
        name="cumsum",
    )(x)


def _page_copies(pt_ref, row, slot, srcs, bufs, sems, n_pages, page):
    copies = []
    for p in range(n_pages):
        idx = pt_ref[row * n_pages + p]
        rows = pl.ds(p * page, page)
        for a, ((src, lead), buf) in enumerate(zip(srcs, bufs)):
            src_page = src.at[lead + (idx,)]
            if len(buf.shape) == 4:
                for j in range(buf.shape[1]):
                    copies.append(pltpu.make_async_copy(src_page.at[:, j, :], buf.at[slot, j, rows, :], sems.at[a, slot]))
            else:
                copies.append(pltpu.make_async_copy(src_page, buf.at[slot, rows], sems.at[a, slot]))
    return copies


def _fetch_rows(pt_ref, srcs, bufs, sems, n_pages, page):
    b = pl.program_id(0)
    nb = pl.num_programs(0)
    slot = lax.rem(b, 2)

    def start_all(copies):
        for n, c in enumerate(copies):
            c.start(priority=n % 2)

    @pl.when(b == 0)
    def _():
        start_all(_page_copies(pt_ref, b, slot, srcs, bufs, sems, n_pages, page))

    @pl.when(b + 1 < nb)
    def _():
        start_all(_page_copies(pt_ref, b + 1, 1 - slot, srcs, bufs, sems, n_pages, page))

    for c in _page_copies(pt_ref, b, slot, srcs, bufs, sems, n_pages, page):
        c.wait()
    return slot


def _append_new_row(buf, slot, past, new):
    first = lax.broadcasted_iota(jnp.int32, (SUBLANES, HEAD_DIM), 0) == 0
    for j in range(new.shape[0]):
        buf[slot, j, past:past + SUBLANES, :] = jnp.where(first, new[j:j + 1, :], 0.0)


def _tail_bias(heads):
    first = lax.broadcasted_iota(jnp.int32, (SUBLANES, heads), 0) == 0
    return jnp.where(first, 0.0, NEG_INF)


def _group_queries(q, j, group):
    r = lax.broadcasted_iota(jnp.int32, q.shape, 0)
    return jnp.where(r // group == j, q, jnp.zeros_like(q))


def _decode_scores(kb, q, n_kv):
    group = q.shape[0] // n_kv
    s = None
    for j0 in range(0, n_kv, 2):
        js = range(j0, min(j0 + 2, n_kv))
        k2 = jnp.concatenate([kb[j].astype(BF16) for j in js], axis=1)
        q2 = jnp.concatenate([_group_queries(q, j, group) for j in js], axis=1)
        sj = lax.dot_general(k2, q2, NT_DIMS, preferred_element_type=F32)
        s = sj if s is None else s + sj
    return s


def _decode_softmax_pv(s, vb, n_kv):
    heads = s.shape[1]
    group = heads // n_kv
    m = jnp.max(s, axis=0, keepdims=True)
    p = jnp.exp2(s - m)
    l = jnp.sum(p, axis=0, keepdims=True)
    p16 = p.astype(BF16)
    o_t = jnp.concatenate(
        [lax.dot_general(vb[j].astype(BF16), p16, TN_DIMS, preferred_element_type=F32) for j in range(n_kv)],
        axis=0) / l
    o = o_t.T
    return jnp.concatenate(
        [o[h:h + 1, (h // group) * HEAD_DIM:(h // group + 1) * HEAD_DIM] for h in range(heads)], axis=0)


def _moba_decode_kernel(pt_ref, q_ref, knew_ref, vnew_ref, alibi_ref, k_hbm, v_hbm, o_ref, kbuf, vbuf, sems,
                        *, layer, n_kv, n_pages, page):
    slot = _fetch_rows(pt_ref, ((k_hbm, (layer,)), (v_hbm, (layer,))), (kbuf, vbuf), sems, n_pages, page)
    past = n_pages * page
    nblk = past // MOBA_BLOCK
    _append_new_row(kbuf, slot, past, knew_ref[0])
    _append_new_row(vbuf, slot, past, vnew_ref[0])
    kb, vb = kbuf.at[slot], vbuf.at[slot]
    q = q_ref[0]
    heads = q.shape[0]
    group = heads // n_kv

    g = None
    for j in range(n_kv):
        means = jnp.concatenate(
            [jnp.sum(kb[j, n * MOBA_BLOCK:(n + 1) * MOBA_BLOCK, :], axis=0, keepdims=True) for n in range(nblk)],
            axis=0) * (1.0 / MOBA_BLOCK)
        gj = lax.dot_general(means, _group_queries(q, j, group).astype(F32), NT_DIMS, precision=HI,
                             preferred_element_type=F32)
        g = gj if g is None else g + gj
    blk = lax.broadcasted_iota(jnp.int32, g.shape, 0)
    cnt = jnp.zeros(g.shape, jnp.int32)
    for m in range(nblk):
        gm = g[m:m + 1, :]
        cnt = cnt + ((gm > g) | ((gm == g) & (blk > m))).astype(jnp.int32)
    sel = cnt < MOBA_TOPK
    keep = jnp.concatenate([jnp.broadcast_to(sel[n:n + 1, :], (MOBA_BLOCK, heads)) for n in range(nblk)]
                           + [jnp.ones((SUBLANES, heads), jnp.bool_)], axis=0)
    s = jnp.where(keep, _decode_scores(kb, q, n_kv) + alibi_ref[...], NEG_INF)
    o_ref[0] = _decode_softmax_pv(s, vb, n_kv).astype(o_ref.dtype)


def _fox_decode_kernel(pt_ref, q_ref, knew_ref, vnew_ref, fnew_ref, k_hbm, v_hbm, c_hbm, o_ref, kbuf, vbuf, cbuf, sems,
                       *, layer, n_kv, n_pages, page):
    slot = _fetch_rows(pt_ref, ((k_hbm, (layer,)), (v_hbm, (layer,)), (c_hbm, ())), (kbuf, vbuf, cbuf), sems,
                       n_pages, page)
    past = n_pages * page
    _append_new_row(kbuf, slot, past, knew_ref[0])
    _append_new_row(vbuf, slot, past, vnew_ref[0])
    kb, vb = kbuf.at[slot], vbuf.at[slot]
    q = q_ref[0]
    heads = q.shape[0]

    carry = jnp.zeros((1, heads), F32)
    cums = []
    for j in range(n_pages):
        cs = cbuf[slot, j * page:(j + 1) * page, :] + carry
        cums.append(cs)
        carry = cs[page - 1:page, :]
    c_now = carry + fnew_ref[0]
    bias = [(c_now - cs) * LOG2E for cs in cums]
    s = _decode_scores(kb, q, n_kv) + jnp.concatenate(bias + [_tail_bias(heads)], axis=0)
    o_ref[0] = _decode_softmax_pv(s, vb, n_kv).astype(o_ref.dtype)


def _decode_attention(body, layer, page_table, q16, k_new, v_new, extra, extra_per_row, caches, n_kv):
    rows, n_pages = page_table.shape
    heads = q16.shape[1] // HEAD_DIM
    page = caches[0].shape[2]
    past = n_pages * page
    assert past % MOBA_BLOCK == 0
    row_spec = lambda shape: pl.BlockSpec((1,) + shape, lambda b, pt: (b, 0, 0))
    in_specs = [row_spec((heads, HEAD_DIM)), row_spec((n_kv, HEAD_DIM)), row_spec((n_kv, HEAD_DIM))]
    if extra_per_row:
        in_specs.append(row_spec((1, heads)))
        extra = extra.reshape(rows, 1, heads)
    else:
        in_specs.append(pl.BlockSpec(extra.shape, lambda b, pt: (0, 0)))
    in_specs += [pl.BlockSpec(memory_space=pl.ANY)] * len(caches)
    scratch = [pltpu.VMEM((2, n_kv, past + SUBLANES, HEAD_DIM) if c.ndim == 5 else (2, past, c.shape[2]), F32)
               for c in caches]
    scratch.append(pltpu.SemaphoreType.DMA((len(caches), 2)))
    return pl.pallas_call(
        functools.partial(body, layer=layer, n_kv=n_kv, n_pages=n_pages, page=page),
        grid_spec=pltpu.PrefetchScalarGridSpec(
            num_scalar_prefetch=1,
            grid=(rows,),
            in_specs=in_specs,
            out_specs=pl.BlockSpec((1, heads, HEAD_DIM), lambda b, pt: (b, 0, 0)),
            scratch_shapes=scratch,
        ),
        out_shape=jax.ShapeDtypeStruct((rows, heads, HEAD_DIM), F32),
        compiler_params=_params("arbitrary"),
        name="decode_attention",
    )(page_table.reshape(-1), q16.reshape(rows, heads, HEAD_DIM), k_new, v_new, extra, *caches
      ).reshape(rows, heads * HEAD_DIM)


def _page_cumsum_kernel(x_ref, o_ref):
    page = x_ref.shape[1]
    r = lax.broadcasted_iota(jnp.int32, (page, page), 0)
    c = lax.broadcasted_iota(jnp.int32, (page, page), 1)
    o_ref[...] = jnp.dot(x_ref[...], (r <= c).astype(F32), precision=HI, preferred_element_type=F32)


def _page_cumsum(logf_cache):
    pool, page, heads = logf_cache.shape
    x = logf_cache.transpose(0, 2, 1).reshape(pool * heads, page)
    tm = _row_tile(pool * heads, 2048)
    out = pl.pallas_call(
        _page_cumsum_kernel,
        grid=(pool * heads // tm,),
        in_specs=[pl.BlockSpec((tm, page), lambda i: (i, 0))],
        out_specs=pl.BlockSpec((tm, page), lambda i: (i, 0)),
        out_shape=jax.ShapeDtypeStruct((pool * heads, page), F32),
        compiler_params=_params("parallel"),
        name="page_cumsum",
    )(x)
    return out.reshape(pool, heads, page).transpose(0, 2, 1)


def _merge_kernel(x_ref, oa_ref, ob_ref, z_ref, g_ref, wa_ref, wb_ref, wo_ref, y_ref):
    wa_w = oa_ref.shape[1]
    d = x_ref.shape[1]

    def branch(o_ref, z, w_ref):
        u = o_ref[...] * (z * jax.nn.sigmoid(z))
        return jnp.dot(u.astype(BF16), w_ref[...], preferred_element_type=F32)

    ya = branch(oa_ref, z_ref[:, :wa_w], wa_ref)
    yb = branch(ob_ref, z_ref[:, wa_w:], wb_ref)
    mixed = g_ref[:, :d] * ya + g_ref[:, d:] * yb
    y_ref[...] = x_ref[...] + jnp.dot(mixed.astype(BF16), wo_ref[...], preferred_element_type=F32)


def _merge(x, oa, ob, z, gates, wa, wb, wo):
    m, d = x.shape
    tm = _row_tile(m, 256)
    rows = lambda width: pl.BlockSpec((tm, width), lambda i: (i, 0))
    resident = lambda w: pl.BlockSpec(w.shape, lambda i: (0, 0), pipeline_mode=pl.Buffered(1))
    return pl.pallas_call(
        _merge_kernel,
        grid=(m // tm,),
        in_specs=[rows(d), rows(oa.shape[1]), rows(ob.shape[1]), rows(z.shape[1]), rows(gates.shape[1]),
                  resident(wa), resident(wb), resident(wo)],
        out_specs=rows(d),
        out_shape=jax.ShapeDtypeStruct((m, d), F32),
        compiler_params=_params("parallel"),
        name="merge",
    )(x, oa, ob, z, gates, wa, wb, wo)


def _project(x2d, g_norm, wt, wts, dims):
    d, wa_w, wb_w, kva, kvb, hb = dims
    n_kv_a, n_kv_b = kva // HEAD_DIM, kvb // HEAD_DIM
    tile = wa_w
    assert wa_w == wb_w == 2 * kva == 2 * kvb
    branch = 2 * wa_w + 2 * kva
    f_row = 2 * branch
    h, ka, va, ka16, va16 = _proj(functools.partial(_proj_kv_kernel, n_kv=n_kv_a, n_forget=0), x2d, wt, (wa_w, 0, 1),
                                  tile, [(wts["g_ka"], tile, 0)],
                                  [(F32, (n_kv_a, HEAD_DIM)), (F32, (n_kv_a, HEAD_DIM)), (BF16, (kva,)),
                                   (BF16, (kva,))], norm_gain=g_norm.reshape(1, d))
    (q16,) = _proj(_proj_q_kernel, h, wt, (0, branch, 2), tile, [(wts["g_q"], tile, 0)], [(BF16, (wa_w + wb_w,))])
    kb, vb, kb16, vb16, logf = _proj(functools.partial(_proj_kv_kernel, n_kv=n_kv_b, n_forget=hb), h, wt,
                                     (branch + wb_w, 0, 1), tile, [(wts["g_kb"], tile, 0), (wts["b_f"], LANES, 0)],
                                     [(F32, (n_kv_b, HEAD_DIM)), (F32, (n_kv_b, HEAD_DIM)), (BF16, (kvb,)),
                                      (BF16, (kvb,)), (F32, (hb,))], forget_row=f_row)
    (z,) = _proj(_proj_plain_kernel, h, wt, (wa_w + 2 * kva, branch, 2), tile, [], [(F32, (wa_w + wb_w,))])
    gate_tile = min(tile, 2 * d)
    (gates,) = _proj(_proj_gate_kernel, h, wt, (f_row + hb, gate_tile, 2 * d // gate_tile), gate_tile,
                     [(wts["b_g"], gate_tile, 0)], [(F32, (2 * d,))])
    return dict(q16=q16, ka=ka, va=va, ka16=ka16, va16=va16, kb=kb, vb=vb, kb16=kb16, vb16=vb16, logf=logf, z=z,
                gates=gates)


def _layer_weights(b_forget, b_merge, gq_a, gk_a, gq_b, gk_b, wa, wb, wo, dims):
    d, wa_w, wb_w, kva, kvb, hb = dims
    return dict(
        g_q=jnp.concatenate([jnp.tile(gq_a, wa_w // HEAD_DIM), jnp.tile(gq_b, wb_w // HEAD_DIM)]).reshape(1, -1),
        g_ka=jnp.concatenate([jnp.tile(gk_a, kva // HEAD_DIM), jnp.zeros((kva,), F32)]).reshape(1, -1),
        g_kb=jnp.concatenate([jnp.tile(gk_b, kvb // HEAD_DIM), jnp.zeros((kvb,), F32)]).reshape(1, -1),
        b_f=jnp.concatenate([b_forget, jnp.zeros((LANES - hb,), F32)]).reshape(1, -1),
        b_g=b_merge.reshape(1, 2 * d),
        wa=wa.astype(BF16), wb=wb.astype(BF16), wo=wo.astype(BF16),
    )


def kernel(x_prompt, x_sample, cache_k_moba, cache_v_moba, cache_k_fox, cache_v_fox, cache_logf_fox, page_table,
           g_norm, w_in, b_forget, b_merge, gq_moba, gk_moba, gq_fox, gk_fox, w_branch_moba, w_branch_fox, w_out):
    depth = w_in.shape[0]
    batch, seq, d = x_prompt.shape
    rows, dec_seq, _ = x_sample.shape
    assert dec_seq == 1 and seq % MOBA_BLOCK == 0
    n_kv_a, n_kv_b = cache_k_moba.shape[3], cache_k_fox.shape[3]
    hb = b_forget.shape[1]
    wa_w, wb_w = w_branch_moba.shape[1], w_branch_fox.shape[1]
    ha = wa_w // HEAD_DIM
    assert wb_w == hb * HEAD_DIM
    kva, kvb = n_kv_a * HEAD_DIM, n_kv_b * HEAD_DIM
    slopes = 2.0 ** (-8.0 * jnp.arange(1, ha + 1, dtype=F32) / ha)
    past = page_table.shape[1] * cache_k_moba.shape[2]
    pos = jnp.arange(past + SUBLANES)[:, None]
    alibi_decode = jnp.where(pos < past, (-LOG2E) * slopes[None, :] * (past - pos).astype(F32),
                             jnp.where(pos == past, 0.0, NEG_INF))

    xp = x_prompt.reshape(batch * seq, d)
    xs = x_sample.reshape(rows, d)
    outs_p, outs_s = [], []
    for l in range(depth):
        dims = (d, wa_w, wb_w, kva, kvb, hb)
        wt = w_in[l].T
        wts = _layer_weights(b_forget[l], b_merge[l], gq_moba[l], gk_moba[l], gq_fox[l], gk_fox[l],
                             w_branch_moba[l], w_branch_fox[l], w_out[l], dims)
        pp = _project(xp, g_norm[l], wt, wts, dims)
        kmean = _kmean(pp["ka"], batch, seq)
        oa = _moba_prompt(pp["q16"], 0, pp["ka16"], pp["va16"], kmean, slopes, batch, seq, ha, n_kv_a)
        group_b = hb // n_kv_b
        c_row = _cumsum_rows(pp["logf"].reshape(batch, seq, hb).transpose(0, 2, 1))
        c_col = c_row.reshape(batch, n_kv_b, group_b, seq).transpose(1, 0, 3, 2).reshape(n_kv_b, batch * seq, group_b)
        ob = _fox_prompt(pp["q16"], wa_w, pp["kb16"], pp["vb16"], c_col,
                         c_row.reshape(batch, n_kv_b, group_b, seq), batch, seq, hb, n_kv_b)
        xp = _merge(xp, oa, ob, pp["z"], pp["gates"], wts["wa"], wts["wb"], wts["wo"])
        outs_p.append((pp["ka"], pp["va"], pp["kb"], pp["vb"], pp["logf"]))
        ps = _project(xs, g_norm[l], wt, wts, dims)
        oa_s = _decode_attention(_moba_decode_kernel, l, page_table, ps["q16"][:, :wa_w], ps["ka"], ps["va"],
                                 alibi_decode, False, [cache_k_moba, cache_v_moba], n_kv_a)
        ob_s = _decode_attention(_fox_decode_kernel, l, page_table, ps["q16"][:, wa_w:], ps["kb"], ps["vb"],
                                 ps["logf"], True, [cache_k_fox, cache_v_fox, _page_cumsum(cache_logf_fox[l])], n_kv_b)
        xs = _merge(xs, oa_s, ob_s, ps["z"], ps["gates"], wts["wa"], wts["wb"], wts["wo"])
        outs_s.append((ps["ka"], ps["va"], ps["kb"], ps["vb"], ps["logf"]))

    def stacked(outs, i, shape):
        return jnp.stack([o[i].reshape(shape) for o in outs])

    res = [xp.reshape(batch, seq, d), xs.reshape(rows, 1, d)]
    for outs, lead in ((outs_p, (batch, seq)), (outs_s, (rows, 1))):
        res += [stacked(outs, 0, lead + (n_kv_a, HEAD_DIM)), stacked(outs, 1, lead + (n_kv_a, HEAD_DIM)),
                stacked(outs, 2, lead + (n_kv_b, HEAD_DIM)), stacked(outs, 3, lead + (n_kv_b, HEAD_DIM)),
                stacked(outs, 4, lead + (hb,))]
    return tuple(res)
```

```python
import functools

import jax
import jax.numpy as jnp
from jax import lax
from jax.experimental import pallas as pl
from jax.experimental.pallas import tpu as pltpu

HEAD_DIM = 128
MOBA_BLOCK = 256
MOBA_TOPK = 3
NORM_EPS = 1e-6
NEG_INF = -1e30
M_INIT = 0.5 * NEG_INF
SCALE = HEAD_DIM ** -0.5
LOG2E = 1.4426950408889634
LANES = 128
SUBLANES = 8
VMEM_LIMIT = 52 * 1024 * 1024
ATTN_TILE = 512

F32 = jnp.float32
BF16 = jnp.bfloat16
HI = lax.Precision.HIGHEST
NT_DIMS = (((1,), (1,)), ((), ()))
TN_DIMS = (((0,), (0,)), ((), ()))


def _params(*sem):
    return pltpu.CompilerParams(dimension_semantics=sem, vmem_limit_bytes=VMEM_LIMIT)


def _row_tile(m, want):
    t = min(m, want)
    assert m % t == 0
    return t


def _head_norm(a, g):
    ms = jnp.mean(a * a, axis=-1, keepdims=True)
    return a * lax.rsqrt(ms + NORM_EPS) * g


def _log_sigmoid(x):
    return jnp.minimum(x, 0.0) - jnp.log1p(jnp.exp(-jnp.abs(x)))


def _proj_q_kernel(h_ref, w_ref, g_ref, o_ref):
    acc = lax.dot_general(h_ref[...], w_ref[...], NT_DIMS, preferred_element_type=F32)
    for j in range(acc.shape[1] // HEAD_DIM):
        sl = slice(j * HEAD_DIM, (j + 1) * HEAD_DIM)
        o_ref[:, sl] = (_head_norm(acc[:, sl], g_ref[:, sl]) * (SCALE * LOG2E)).astype(o_ref.dtype)


def _proj_kv_kernel(h_ref, w_ref, g_ref, *refs, n_kv, n_forget):
    if n_forget:
        wf_ref, bf_ref, *refs = refs
    k32_hbm, v32_hbm, k16_ref, v16_ref, *refs = refs
    if n_forget:
        logf_ref, *refs = refs
    kbuf, vbuf, sems = refs
    kvw = n_kv * HEAD_DIM
    tm = h_ref.shape[0]
    i = pl.program_id(1)
    last = pl.num_programs(1) - 1
    slot = lax.rem(i, 2)

    def writebacks(step, slot):
        rows = pl.ds(pl.multiple_of(step * tm, tm), tm)
        return [pltpu.make_async_copy(buf.at[slot, :, j * HEAD_DIM:(j + 1) * HEAD_DIM], dst.at[rows, j, :],
                                      sems.at[a, slot])
                for a, (buf, dst) in enumerate(((kbuf, k32_hbm), (vbuf, v32_hbm))) for j in range(n_kv)]

    @pl.when(i >= 2)
    def _():
        for c in writebacks(i - 2, slot):
            c.wait()

    h = h_ref[...]
    acc = lax.dot_general(h, w_ref[...], NT_DIMS, preferred_element_type=F32)
    for j in range(n_kv):
        sl = slice(j * HEAD_DIM, (j + 1) * HEAD_DIM)
        kn = _head_norm(acc[:, sl], g_ref[:, sl])
        kbuf[slot, :, sl] = kn
        k16_ref[:, sl] = kn.astype(BF16)
        v = acc[:, kvw + j * HEAD_DIM:kvw + (j + 1) * HEAD_DIM]
        vbuf[slot, :, sl] = v
        v16_ref[:, sl] = v.astype(BF16)
    if n_forget:
        f = lax.dot_general(h, wf_ref[...].astype(BF16), NT_DIMS, preferred_element_type=F32)
        logf_ref[...] = _log_sigmoid(f[:, :n_forget] + bf_ref[:, :n_forget])
    for c in writebacks(i, slot):
        c.start()

    @pl.when((i == last) & (i >= 1))
    def _():
        for c in writebacks(i - 1, 1 - slot):
            c.wait()

    @pl.when(i == last)
    def _():
        for c in writebacks(i, slot):
            c.wait()


def _proj_plain_kernel(h_ref, w_ref, o_ref):
    o_ref[...] = lax.dot_general(h_ref[...], w_ref[...], NT_DIMS, preferred_element_type=F32).astype(o_ref.dtype)


def _proj_gate_kernel(h_ref, w_ref, b_ref, o_ref):
    acc = lax.dot_general(h_ref[...], w_ref[...], NT_DIMS, preferred_element_type=F32)
    o_ref[...] = jax.nn.sigmoid(acc + b_ref[...]).astype(o_ref.dtype)


def _with_cached_weights(body, h_ref, w_ref, *refs):
    *refs, w16_sc = refs

    @pl.when(pl.program_id(1) == 0)
    def _():
        w16_sc[...] = w_ref[...].astype(BF16)

    body(h_ref, w16_sc, *refs)


def _with_rmsnorm(body, n_inputs, x_ref, w_ref, gn_ref, *refs):
    ins, (h_ref, *rest) = refs[:n_inputs], refs[n_inputs:]
    x = x_ref[...]
    ms = jnp.mean(x * x, axis=-1, keepdims=True)
    h_ref[...] = (x * lax.rsqrt(ms + NORM_EPS) * gn_ref[...]).astype(h_ref.dtype)
    body(h_ref, w_ref, *ins, *rest)


def _weight_rows(wt, n_rows, first, stride=0):
    return pl.BlockSpec((pl.Element(n_rows), pl.Element(wt.shape[1])),
                        lambda j, i: (pl.multiple_of(first + stride * j, SUBLANES), 0))


def _proj(body, h, wt, rows, tn, extras, outs, forget_row=None, norm_gain=None):
    m, d = h.shape
    first, stride, count = rows
    assert first % SUBLANES == 0 and stride % SUBLANES == 0
    tm = _row_tile(m, 1024 if norm_gain is None else 512)
    in_specs = [pl.BlockSpec((tm, d), lambda j, i: (i, 0)), _weight_rows(wt, tn, first, stride)]
    args = [h, wt]
    if norm_gain is not None:
        assert count == 1
        in_specs.append(pl.BlockSpec((1, d), lambda j, i: (0, 0)))
        args.append(norm_gain)
        outs = [(BF16, (d,))] + list(outs)
        body = functools.partial(_with_rmsnorm, body, len(extras) + (forget_row is not None))
    for arr, width, base in extras[:1]:
        in_specs.append(pl.BlockSpec((arr.shape[0], width), lambda j, i, base=base: (0, base + j)))
        args.append(arr)
    if forget_row is not None:
        in_specs.append(_weight_rows(wt, LANES, forget_row))
        args.append(wt)
    for arr, width, base in extras[1:]:
        in_specs.append(pl.BlockSpec((arr.shape[0], width), lambda j, i, base=base: (0, base + j)))
        args.append(arr)
    out_specs, out_shape, staging = [], [], []
    for dt, trail in outs:
        if len(trail) == 2:
            assert count == 1
            out_specs.append(pl.BlockSpec(memory_space=pl.ANY))
            staging.append(pltpu.VMEM((2, tm, trail[0] * trail[1]), dt))
        else:
            out_specs.append(pl.BlockSpec((tm, trail[0] // count), lambda j, i: (i, j)))
        out_shape.append(jax.ShapeDtypeStruct((m,) + trail, dt))
    if staging:
        staging.append(pltpu.SemaphoreType.DMA((len(staging), 2)))
    return pl.pallas_call(
        functools.partial(_with_cached_weights, body),
        grid=(count, m // tm),
        in_specs=in_specs,
        out_specs=out_specs,
        out_shape=out_shape,
        scratch_shapes=staging + [pltpu.VMEM((tn, d), BF16)],
        compiler_params=_params("parallel", "arbitrary"),
        name="proj",
    )(*args)


def _kmean_kernel(k_ref, o_ref, *, nblk, n_kv):
    o_ref[...] = jnp.zeros_like(o_ref)
    for j in range(n_kv):
        for n in range(nblk):
            blk = k_ref[n * MOBA_BLOCK:(n + 1) * MOBA_BLOCK, j, :]
            o_ref[0, j, n:n + 1, :] = jnp.sum(blk, axis=0, keepdims=True) * (1.0 / MOBA_BLOCK)


def _kmean(k32, batch, seq):
    n_kv = k32.shape[1]
    nblk = seq // MOBA_BLOCK
    assert nblk <= LANES
    return pl.pallas_call(
        functools.partial(_kmean_kernel, nblk=nblk, n_kv=n_kv),
        grid=(batch,),
        in_specs=[pl.BlockSpec((seq, n_kv, HEAD_DIM), lambda b: (b, 0, 0))],
        out_specs=pl.BlockSpec((1, n_kv, LANES, HEAD_DIM), lambda b: (b, 0, 0, 0)),
        out_shape=jax.ShapeDtypeStruct((batch, n_kv, LANES, HEAD_DIM), F32),
        compiler_params=_params("parallel"),
        name="kmean",
    )(k32)


def _split3(x):
    hi = x.astype(BF16).astype(F32)
    r = x - hi
    mid = r.astype(BF16).astype(F32)
    return hi, mid, (r - mid).astype(BF16).astype(F32)


def _store_wide_queries(qw_sc, q_ref, coef_rows):
    tq = q_ref.shape[0]
    for h, coef in enumerate(coef_rows):
        qw_sc[h * tq:(h + 1) * tq, :HEAD_DIM] = q_ref[:, h * HEAD_DIM:(h + 1) * HEAD_DIM]
        qw_sc[h * tq:(h + 1) * tq, HEAD_DIM:] = jnp.broadcast_to(coef.astype(BF16), (tq, LANES))


def _qk_logits(k, kx, qw_sc):
    return lax.dot_general(jnp.concatenate([k, kx], axis=1), qw_sc[...], NT_DIMS, preferred_element_type=F32)


def _pipelined_tiles(qi, s_refs, logits_into, process):
    s0, s1 = s_refs
    logits_into(s0, 0)

    def pair(kp, carry):
        t = 2 * kp
        logits_into(s1, t + 1)
        process(s0, t, False)
        logits_into(s0, t + 2)
        process(s1, t + 1, False)
        return carry

    lax.fori_loop(0, qi // 2, pair, 0)

    @pl.when(qi % 2 == 0)
    def _():
        process(s0, qi, True)

    @pl.when(qi % 2 == 1)
    def _():
        logits_into(s1, qi)
        process(s0, qi - 1, False)
        process(s1, qi, True)


def _softmax_tile_t(head_parts, v, m_sc, l_sc, acc_sc):
    heads = range(len(head_parts))
    tq = head_parts[0][0][0].shape[1]
    m_prev = [m_sc[h] for h in heads]
    m_new = []
    for h in heads:
        m = m_prev[h]
        for u, rb in head_parts[h]:
            m = jnp.maximum(m, jnp.max(u, axis=0, keepdims=True) + rb)
        m_new.append(m)
    alphas, p16s = [], []
    for h in heads:
        alpha = jnp.exp2(m_prev[h] - m_new[h])
        l_new = alpha * l_sc[h]
        ps = []
        for u, rb in head_parts[h]:
            p = jnp.exp2(u - (m_new[h] - rb))
            l_new = l_new + jnp.sum(p, axis=0, keepdims=True)
            ps.append(p.astype(v.dtype))
        p16s.append(ps[0] if len(ps) == 1 else jnp.concatenate(ps, axis=0))
        alphas.append(alpha)
        l_sc[h] = l_new
        m_sc[h] = m_new[h]
    pv = lax.dot_general(v, jnp.concatenate(p16s, axis=1), TN_DIMS, preferred_element_type=F32)
    for h in heads:
        acc_sc[h] = alphas[h] * acc_sc[h] + pv[:, h * tq:(h + 1) * tq]


def _init_softmax_state(m_sc, l_sc, acc_sc):
    m_sc[...] = jnp.full(m_sc.shape, M_INIT, F32)
    l_sc[...] = jnp.zeros(l_sc.shape, F32)
    acc_sc[...] = jnp.zeros(acc_sc.shape, F32)


def _write_heads(o_ref, l_sc, acc_sc, group):
    for h in range(group):
        o_ref[:, h * HEAD_DIM:(h + 1) * HEAD_DIM] = (acc_sc[h] / l_sc[h]).T.astype(o_ref.dtype)


def _moba_prompt_kernel(slopes_ref, q_ref, k_ref, v_ref, kmean_ref, o_ref, m_sc, l_sc, acc_sc, selb_sc, qw_sc, kx_sc,
                        s0_sc, s1_sc, *, group, nblk):
    tq = q_ref.shape[0]
    per_tile = tq // MOBA_BLOCK
    kvh = pl.program_id(1)
    qi = pl.program_id(2)
    _init_softmax_state(m_sc, l_sc, acc_sc)

    qcol = lax.broadcasted_iota(jnp.int32, (1, tq), 1)
    own = qi * per_tile + qcol // MOBA_BLOCK
    blk = lax.broadcasted_iota(jnp.int32, (nblk, tq), 0)
    km = kmean_ref[0, 0, :-(-nblk // SUBLANES) * SUBLANES, :]
    for h in range(group):
        qh = q_ref[:, h * HEAD_DIM:(h + 1) * HEAD_DIM].astype(F32)
        g = lax.dot_general(km, qh, NT_DIMS, precision=HI, preferred_element_type=F32)[:nblk]
        cnt = jnp.zeros((nblk, tq), jnp.int32)
        for m in range(nblk):
            gm = g[m:m + 1, :]
            beats = (gm > g) | ((gm == g) & (blk > m))
            cnt = cnt + (beats & (own > m)).astype(jnp.int32)
        sel = (blk < own) & (cnt < MOBA_TOPK)
        selb_sc[h] = jnp.where(sel, 0.0, NEG_INF)

    slope2 = [slopes_ref[kvh * group + h] * LOG2E for h in range(group)]
    lane = lax.broadcasted_iota(jnp.int32, (1, LANES), 1)
    coefs = []
    for h in range(group):
        hi, mid, lo = _split3(jnp.zeros((1, LANES), F32) + slope2[h])
        coefs.append(jnp.where(lane < 6, jnp.where(lane % 3 == 0, hi, jnp.where(lane % 3 == 1, mid, lo)), 0.0))
    _store_wide_queries(qw_sc, q_ref, coefs)
    koff = lax.broadcasted_iota(jnp.int32, (tq, LANES), 0)
    klane = lax.broadcasted_iota(jnp.int32, (tq, LANES), 1)
    kx_sc[...] = jnp.where(klane < 3, (koff // MOBA_BLOCK) * MOBA_BLOCK,
                           jnp.where(klane < 6, koff % MOBA_BLOCK, 0)).astype(F32).astype(BF16)

    def logits_into(s_ref, ki):
        s_ref[...] = _qk_logits(k_ref[pl.ds(pl.multiple_of(ki * tq, tq), tq), :], kx_sc[...], qw_sc)

    def process(s_ref, ki, diagonal):
        parts = []
        for h in range(group):
            query_bias = -slope2[h] * (((qi - ki) * tq).astype(F32) + qcol.astype(F32))
            parts.append([])
            for c in range(per_tile):
                kb = ki * per_tile + c
                u = s_ref[c * MOBA_BLOCK:(c + 1) * MOBA_BLOCK, h * tq:(h + 1) * tq]
                selrow = selb_sc[h, pl.ds(kb, 1), :]
                if diagonal:
                    rel = (lax.broadcasted_iota(jnp.int32, (MOBA_BLOCK, tq), 1)
                           - lax.broadcasted_iota(jnp.int32, (MOBA_BLOCK, tq), 0))
                    u = jnp.where(rel >= c * MOBA_BLOCK, u, NEG_INF)
                    selrow = jnp.where(own > kb, selrow, 0.0)
                parts[h].append((u, selrow + query_bias))
        _softmax_tile_t(parts, v_ref[pl.ds(pl.multiple_of(ki * tq, tq), tq), :], m_sc, l_sc, acc_sc)

    _pipelined_tiles(qi, (s0_sc, s1_sc), logits_into, process)
    _write_heads(o_ref, l_sc, acc_sc, group)


def _moba_prompt(q16, q_col0, k16, v16, kmean, slopes, batch, seq, n_heads, n_kv):
    group = n_heads // n_kv
    tq = _row_tile(seq, ATTN_TILE)
    assert tq % MOBA_BLOCK == 0
    nq = seq // tq
    nblk = seq // MOBA_BLOCK
    gw = group * HEAD_DIM
    qb0 = q_col0 // gw
    return pl.pallas_call(
        functools.partial(_moba_prompt_kernel, group=group, nblk=nblk),
        grid=(batch, n_kv, nq),
        in_specs=[
            pl.BlockSpec(memory_space=pltpu.SMEM),
            pl.BlockSpec((tq, gw), lambda b, j, i: (b * nq + i, qb0 + j)),
            pl.BlockSpec((seq, HEAD_DIM), lambda b, j, i: (b, j)),
            pl.BlockSpec((seq, HEAD_DIM), lambda b, j, i: (b, j)),
            pl.BlockSpec((1, 1, LANES, HEAD_DIM), lambda b, j, i: (b, j, 0, 0)),
        ],
        out_specs=pl.BlockSpec((tq, gw), lambda b, j, i: (b * nq + i, j)),
        out_shape=jax.ShapeDtypeStruct((batch * seq, n_heads * HEAD_DIM), F32),
        scratch_shapes=[
            pltpu.VMEM((group, 1, tq), F32),
            pltpu.VMEM((group, 1, tq), F32),
            pltpu.VMEM((group, HEAD_DIM, tq), F32),
            pltpu.VMEM((group, nblk, tq), F32),
            pltpu.VMEM((group * tq, 2 * HEAD_DIM), BF16),
            pltpu.VMEM((tq, LANES), BF16),
            pltpu.VMEM((tq, group * tq), F32),
            pltpu.VMEM((tq, group * tq), F32),
        ],
        compiler_params=_params("parallel", "parallel", "arbitrary"),
        name="moba_prompt",
    )(slopes, q16, k16, v16, kmean)


def _fox_prompt_kernel(q_ref, k_ref, v_ref, ccol_ref, crow_ref, o_ref, m_sc, l_sc, acc_sc, qw_sc, kx_sc, s0_sc, s1_sc,
                       *, group):
    tq = q_ref.shape[0]
    seq = k_ref.shape[0]
    qi = pl.program_id(2)
    _init_softmax_state(m_sc, l_sc, acc_sc)
    lane = lax.broadcasted_iota(jnp.int32, (1, LANES), 1)
    _store_wide_queries(qw_sc, q_ref, [jnp.where(lane // 3 == h, -1.0, 0.0) for h in range(group)])

    @pl.when(qi == 0)
    def _():
        place = (lax.broadcasted_iota(jnp.int32, (group, LANES), 1) // 3
                 == lax.broadcasted_iota(jnp.int32, (group, LANES), 0)).astype(F32)
        x = jnp.dot(ccol_ref[0] * LOG2E, place, precision=HI, preferred_element_type=F32)
        hi, mid, lo = _split3(x)
        l3 = lax.broadcasted_iota(jnp.int32, (seq, LANES), 1) % 3
        kx_sc[...] = jnp.where(l3 == 0, hi, jnp.where(l3 == 1, mid, lo)).astype(BF16)

    def logits_into(s_ref, ki):
        off = pl.multiple_of(ki * tq, tq)
        s_ref[...] = _qk_logits(k_ref[pl.ds(off, tq), :], kx_sc[pl.ds(off, tq), :], qw_sc)

    def cq(h):
        return crow_ref[0, 0, h:h + 1, pl.ds(pl.multiple_of(qi * tq, tq), tq)] * LOG2E

    def process(s_ref, ki, diagonal):
        u = [s_ref[:, h * tq:(h + 1) * tq] for h in range(group)]
        if diagonal:
            krow = lax.broadcasted_iota(jnp.int32, (tq, tq), 0)
            qcol = lax.broadcasted_iota(jnp.int32, (tq, tq), 1)
            u = [jnp.where(qcol >= krow, uh, NEG_INF) for uh in u]
        _softmax_tile_t([[(u[h], cq(h))] for h in range(group)], v_ref[pl.ds(pl.multiple_of(ki * tq, tq), tq), :],
                        m_sc, l_sc, acc_sc)

    _pipelined_tiles(qi, (s0_sc, s1_sc), logits_into, process)
    _write_heads(o_ref, l_sc, acc_sc, group)


def _fox_prompt(q16, q_col0, k16, v16, c_col, c_row, batch, seq, n_heads, n_kv):
    group = n_heads // n_kv
    tq = _row_tile(seq, ATTN_TILE)
    nq = seq // tq
    gw = group * HEAD_DIM
    qb0 = q_col0 // gw
    return pl.pallas_call(
        functools.partial(_fox_prompt_kernel, group=group),
        grid=(batch, n_kv, nq),
        in_specs=[
            pl.BlockSpec((tq, gw), lambda b, j, i: (b * nq + i, qb0 + j)),
            pl.BlockSpec((seq, HEAD_DIM), lambda b, j, i: (b, j)),
            pl.BlockSpec((seq, HEAD_DIM), lambda b, j, i: (b, j)),
            pl.BlockSpec((1, seq, group), lambda b, j, i: (j, b, 0)),
            pl.BlockSpec((1, 1, group, seq), lambda b, j, i: (b, j, 0, 0)),
        ],
        out_specs=pl.BlockSpec((tq, gw), lambda b, j, i: (b * nq + i, j)),
        out_shape=jax.ShapeDtypeStruct((batch * seq, n_heads * HEAD_DIM), F32),
        scratch_shapes=[
            pltpu.VMEM((group, 1, tq), F32),
            pltpu.VMEM((group, 1, tq), F32),
            pltpu.VMEM((group, HEAD_DIM, tq), F32),
            pltpu.VMEM((group * tq, 2 * HEAD_DIM), BF16),
            pltpu.VMEM((seq, LANES), BF16),
            pltpu.VMEM((tq, group * tq), F32),
            pltpu.VMEM((tq, group * tq), F32),
        ],
        compiler_params=_params("arbitrary", "arbitrary", "arbitrary"),
        name="fox_prompt",
    )(q16, k16, v16, c_col, c_row)


def _cumsum_kernel(x_ref, o_ref, *, chunk):
    rows, seq = x_ref.shape[1:]
    r = lax.broadcasted_iota(jnp.int32, (chunk, chunk), 0)
    c = lax.broadcasted_iota(jnp.int32, (chunk, chunk), 1)
    upper = (r <= c).astype(F32)
    carry = jnp.zeros((rows, 1), F32)
    for i in range(seq // chunk):
        sl = slice(i * chunk, (i + 1) * chunk)
        cs = jnp.dot(x_ref[0, :, sl], upper, precision=HI, preferred_element_type=F32) + carry
        o_ref[0, :, sl] = cs
        carry = cs[:, chunk - 1:chunk]


def _cumsum_rows(x):
    batch, rows, seq = x.shape
    chunk = 256
    assert seq % chunk == 0
    return pl.pallas_call(
        functools.partial(_cumsum_kernel, chunk=chunk),
        grid=(batch,),
        in_specs=[pl.BlockSpec((1, rows, seq), lambda b: (b, 0, 0))],
        out_specs=pl.BlockSpec((1, rows, seq), lambda b: (b, 0, 0)),
        out_shape=jax.ShapeDtypeStruct(x.shape, F32),
        compiler_params=_params("parallel"),
        name="cumsum",
    )(x)


def _page_copies(pt_ref, row, slot, srcs, bufs, sems, n_pages):
    copies = []
    for p in range(n_pages):
        idx = pt_ref[row * n_pages + p]
        for a, ((src, lead), buf) in enumerate(zip(srcs, bufs)):
            src_page = src.at[lead + (idx,)]
            r = src_page.shape[0]
            copies.append(pltpu.make_async_copy(src_page, buf.at[slot, pl.ds(p * r, r)], sems.at[a, slot]))
    return copies


def _fetch_rows(pt_ref, srcs, bufs, sems, n_pages):
    b = pl.program_id(0)
    nb = pl.num_programs(0)
    slot = lax.rem(b, 2)

    def start_all(copies):
        for n, c in enumerate(copies):
            c.start(priority=n % 2)

    @pl.when(b == 0)
    def _():
        start_all(_page_copies(pt_ref, b, slot, srcs, bufs, sems, n_pages))

    @pl.when(b + 1 < nb)
    def _():
        start_all(_page_copies(pt_ref, b + 1, 1 - slot, srcs, bufs, sems, n_pages))

    for c in _page_copies(pt_ref, b, slot, srcs, bufs, sems, n_pages):
        c.wait()
    return slot


def _append_new_row(buf, slot, past, new):
    n_kv = new.shape[0]
    tail = jnp.concatenate([new, jnp.zeros(((SUBLANES - 1) * n_kv, HEAD_DIM), new.dtype)], axis=0)
    buf[slot, past * n_kv:(past + SUBLANES) * n_kv, :] = tail


def _head_rows(buf, j, n_kv, first, count):
    return buf[pl.ds(first * n_kv + j, count, stride=n_kv), :]


def _tail_bias(heads):
    first = lax.broadcasted_iota(jnp.int32, (SUBLANES, heads), 0) == 0
    return jnp.where(first, 0.0, NEG_INF)


def _group_queries(q, j, group):
    r = lax.broadcasted_iota(jnp.int32, q.shape, 0)
    return jnp.where(r // group == j, q, jnp.zeros_like(q))


def _decode_scores(kb, q, n_kv):
    group = q.shape[0] // n_kv
    tokens = kb.shape[0] // n_kv
    s = None
    for j0 in range(0, n_kv, 2):
        js = range(j0, min(j0 + 2, n_kv))
        k2 = jnp.concatenate([_head_rows(kb, j, n_kv, 0, tokens).astype(BF16) for j in js], axis=1)
        q2 = jnp.concatenate([_group_queries(q, j, group) for j in js], axis=1)
        sj = lax.dot_general(k2, q2, NT_DIMS, preferred_element_type=F32)
        s = sj if s is None else s + sj
    return s


def _decode_softmax_pv(s, vb, n_kv):
    tokens, heads = s.shape
    group = heads // n_kv
    m = jnp.max(s, axis=0, keepdims=True)
    p = jnp.exp2(s - m)
    l = jnp.sum(p, axis=0, keepdims=True)
    p16 = p.astype(BF16)
    o_t = jnp.concatenate(
        [lax.dot_general(_head_rows(vb, j, n_kv, 0, tokens).astype(BF16), p16, TN_DIMS, preferred_element_type=F32)
         for j in range(n_kv)],
        axis=0) / l
    o = o_t.T
    return jnp.concatenate(
        [o[h:h + 1, (h // group) * HEAD_DIM:(h // group + 1) * HEAD_DIM] for h in range(heads)], axis=0)


def _moba_decode_kernel(pt_ref, q_ref, knew_ref, vnew_ref, alibi_ref, k_hbm, v_hbm, o_ref, kbuf, vbuf, sems,
                        *, layer, n_kv, n_pages, page):
    slot = _fetch_rows(pt_ref, ((k_hbm, (layer,)), (v_hbm, (layer,))), (kbuf, vbuf), sems, n_pages)
    past = n_pages * page
    nblk = past // MOBA_BLOCK
    _append_new_row(kbuf, slot, past, knew_ref[0])
    _append_new_row(vbuf, slot, past, vnew_ref[0])
    kb, vb = kbuf.at[slot], vbuf.at[slot]
    q = q_ref[0]
    heads = q.shape[0]
    group = heads // n_kv

    g = None
    for j in range(n_kv):
        means = jnp.concatenate(
            [jnp.sum(_head_rows(kb, j, n_kv, n * MOBA_BLOCK, MOBA_BLOCK), axis=0, keepdims=True) for n in range(nblk)],
            axis=0) * (1.0 / MOBA_BLOCK)
        gj = lax.dot_general(means, _group_queries(q, j, group).astype(F32), NT_DIMS, precision=HI,
                             preferred_element_type=F32)
        g = gj if g is None else g + gj
    blk = lax.broadcasted_iota(jnp.int32, g.shape, 0)
    cnt = jnp.zeros(g.shape, jnp.int32)
    for m in range(nblk):
        gm = g[m:m + 1, :]
        cnt = cnt + ((gm > g) | ((gm == g) & (blk > m))).astype(jnp.int32)
    sel = cnt < MOBA_TOPK
    keep = jnp.concatenate([jnp.broadcast_to(sel[n:n + 1, :], (MOBA_BLOCK, heads)) for n in range(nblk)]
                           + [jnp.ones((SUBLANES, heads), jnp.bool_)], axis=0)
    s = jnp.where(keep, _decode_scores(kb, q, n_kv) + alibi_ref[...], NEG_INF)
    o_ref[0] = _decode_softmax_pv(s, vb, n_kv).astype(o_ref.dtype)


def _fox_decode_kernel(pt_ref, q_ref, knew_ref, vnew_ref, fnew_ref, k_hbm, v_hbm, c_hbm, o_ref, kbuf, vbuf, cbuf, sems,
                       *, layer, n_kv, n_pages, page):
    slot = _fetch_rows(pt_ref, ((k_hbm, (layer,)), (v_hbm, (layer,)), (c_hbm, ())), (kbuf, vbuf, cbuf), sems, n_pages)
    past = n_pages * page
    _append_new_row(kbuf, slot, past, knew_ref[0])
    _append_new_row(vbuf, slot, past, vnew_ref[0])
    kb, vb = kbuf.at[slot], vbuf.at[slot]
    q = q_ref[0]
    heads = q.shape[0]

    carry = jnp.zeros((1, heads), F32)
    cums = []
    for j in range(n_pages):
        cs = cbuf[slot, j * page:(j + 1) * page, :] + carry
        cums.append(cs)
        carry = cs[page - 1:page, :]
    c_now = carry + fnew_ref[0]
    bias = [(c_now - cs) * LOG2E for cs in cums]
    s = _decode_scores(kb, q, n_kv) + jnp.concatenate(bias + [_tail_bias(heads)], axis=0)
    o_ref[0] = _decode_softmax_pv(s, vb, n_kv).astype(o_ref.dtype)


def _decode_attention(body, layer, page_table, q16, k_new, v_new, extra, extra_per_row, caches, n_kv):
    rows, n_pages = page_table.shape
    heads = q16.shape[1] // HEAD_DIM
    page = caches[0].shape[2]
    past = n_pages * page
    assert past % MOBA_BLOCK == 0
    row_spec = lambda shape: pl.BlockSpec((1,) + shape, lambda b, pt: (b, 0, 0))
    in_specs = [row_spec((heads, HEAD_DIM)), row_spec((n_kv, HEAD_DIM)), row_spec((n_kv, HEAD_DIM))]
    if extra_per_row:
        in_specs.append(row_spec((1, heads)))
        extra = extra.reshape(rows, 1, heads)
    else:
        in_specs.append(pl.BlockSpec(extra.shape, lambda b, pt: (0, 0)))
    in_specs += [pl.BlockSpec(memory_space=pl.ANY)] * len(caches)
    scratch = [pltpu.VMEM((2, (past + SUBLANES) * n_kv, HEAD_DIM) if c.ndim == 5 else (2, past, c.shape[2]), F32)
               for c in caches]
    caches = [c.reshape(c.shape[:2] + (page * n_kv, HEAD_DIM)) if c.ndim == 5 else c for c in caches]
    scratch.append(pltpu.SemaphoreType.DMA((len(caches), 2)))
    return pl.pallas_call(
        functools.partial(body, layer=layer, n_kv=n_kv, n_pages=n_pages, page=page),
        grid_spec=pltpu.PrefetchScalarGridSpec(
            num_scalar_prefetch=1,
            grid=(rows,),
            in_specs=in_specs,
            out_specs=pl.BlockSpec((1, heads, HEAD_DIM), lambda b, pt: (b, 0, 0)),
            scratch_shapes=scratch,
        ),
        out_shape=jax.ShapeDtypeStruct((rows, heads, HEAD_DIM), F32),
        compiler_params=_params("arbitrary"),
        name="decode_attention",
    )(page_table.reshape(-1), q16.reshape(rows, heads, HEAD_DIM), k_new, v_new, extra, *caches
      ).reshape(rows, heads * HEAD_DIM)


def _page_cumsum_kernel(x_ref, o_ref):
    page = x_ref.shape[1]
    r = lax.broadcasted_iota(jnp.int32, (page, page), 0)
    c = lax.broadcasted_iota(jnp.int32, (page, page), 1)
    o_ref[...] = jnp.dot(x_ref[...], (r <= c).astype(F32), precision=HI, preferred_element_type=F32)


def _page_cumsum(logf_cache):
    pool, page, heads = logf_cache.shape
    x = logf_cache.transpose(0, 2, 1).reshape(pool * heads, page)
    tm = _row_tile(pool * heads, 2048)
    out = pl.pallas_call(
        _page_cumsum_kernel,
        grid=(pool * heads // tm,),
        in_specs=[pl.BlockSpec((tm, page), lambda i: (i, 0))],
        out_specs=pl.BlockSpec((tm, page), lambda i: (i, 0)),
        out_shape=jax.ShapeDtypeStruct((pool * heads, page), F32),
        compiler_params=_params("parallel"),
        name="page_cumsum",
    )(x)
    return out.reshape(pool, heads, page).transpose(0, 2, 1)


def _merge_kernel(x_ref, oa_ref, ob_ref, z_ref, g_ref, wa_ref, wb_ref, wo_ref, y_ref):
    wa_w = oa_ref.shape[1]
    d = x_ref.shape[1]

    def branch(o_ref, z, w_ref):
        u = o_ref[...] * (z * jax.nn.sigmoid(z))
        return jnp.dot(u.astype(BF16), w_ref[...], preferred_element_type=F32)

    ya = branch(oa_ref, z_ref[:, :wa_w], wa_ref)
    yb = branch(ob_ref, z_ref[:, wa_w:], wb_ref)
    mixed = g_ref[:, :d] * ya + g_ref[:, d:] * yb
    y_ref[...] = x_ref[...] + jnp.dot(mixed.astype(BF16), wo_ref[...], preferred_element_type=F32)


def _merge(x, oa, ob, z, gates, wa, wb, wo):
    m, d = x.shape
    tm = _row_tile(m, 256)
    rows = lambda width: pl.BlockSpec((tm, width), lambda i: (i, 0))
    resident = lambda w: pl.BlockSpec(w.shape, lambda i: (0, 0), pipeline_mode=pl.Buffered(1))
    return pl.pallas_call(
        _merge_kernel,
        grid=(m // tm,),
        in_specs=[rows(d), rows(oa.shape[1]), rows(ob.shape[1]), rows(z.shape[1]), rows(gates.shape[1]),
                  resident(wa), resident(wb), resident(wo)],
        out_specs=rows(d),
        out_shape=jax.ShapeDtypeStruct((m, d), F32),
        compiler_params=_params("parallel"),
        name="merge",
    )(x, oa, ob, z, gates, wa, wb, wo)


def _project(x2d, g_norm, wt, wts, dims):
    d, wa_w, wb_w, kva, kvb, hb = dims
    n_kv_a, n_kv_b = kva // HEAD_DIM, kvb // HEAD_DIM
    tile = wa_w
    assert wa_w == wb_w == 2 * kva == 2 * kvb
    branch = 2 * wa_w + 2 * kva
    f_row = 2 * branch
    h, ka, va, ka16, va16 = _proj(functools.partial(_proj_kv_kernel, n_kv=n_kv_a, n_forget=0), x2d, wt, (wa_w, 0, 1),
                                  tile, [(wts["g_ka"], tile, 0)],
                                  [(F32, (n_kv_a, HEAD_DIM)), (F32, (n_kv_a, HEAD_DIM)), (BF16, (kva,)),
                                   (BF16, (kva,))], norm_gain=g_norm.reshape(1, d))
    (q16,) = _proj(_proj_q_kernel, h, wt, (0, branch, 2), tile, [(wts["g_q"], tile, 0)], [(BF16, (wa_w + wb_w,))])
    kb, vb, kb16, vb16, logf = _proj(functools.partial(_proj_kv_kernel, n_kv=n_kv_b, n_forget=hb), h, wt,
                                     (branch + wb_w, 0, 1), tile, [(wts["g_kb"], tile, 0), (wts["b_f"], LANES, 0)],
                                     [(F32, (n_kv_b, HEAD_DIM)), (F32, (n_kv_b, HEAD_DIM)), (BF16, (kvb,)),
                                      (BF16, (kvb,)), (F32, (hb,))], forget_row=f_row)
    (z,) = _proj(_proj_plain_kernel, h, wt, (wa_w + 2 * kva, branch, 2), tile, [], [(F32, (wa_w + wb_w,))])
    gate_tile = min(tile, 2 * d)
    (gates,) = _proj(_proj_gate_kernel, h, wt, (f_row + hb, gate_tile, 2 * d // gate_tile), gate_tile,
                     [(wts["b_g"], gate_tile, 0)], [(F32, (2 * d,))])
    return dict(q16=q16, ka=ka, va=va, ka16=ka16, va16=va16, kb=kb, vb=vb, kb16=kb16, vb16=vb16, logf=logf, z=z,
                gates=gates)


def _layer_weights(b_forget, b_merge, gq_a, gk_a, gq_b, gk_b, wa, wb, wo, dims):
    d, wa_w, wb_w, kva, kvb, hb = dims
    return dict(
        g_q=jnp.concatenate([jnp.tile(gq_a, wa_w // HEAD_DIM), jnp.tile(gq_b, wb_w // HEAD_DIM)]).reshape(1, -1),
        g_ka=jnp.concatenate([jnp.tile(gk_a, kva // HEAD_DIM), jnp.zeros((kva,), F32)]).reshape(1, -1),
        g_kb=jnp.concatenate([jnp.tile(gk_b, kvb // HEAD_DIM), jnp.zeros((kvb,), F32)]).reshape(1, -1),
        b_f=jnp.concatenate([b_forget, jnp.zeros((LANES - hb,), F32)]).reshape(1, -1),
        b_g=b_merge.reshape(1, 2 * d),
        wa=wa.astype(BF16), wb=wb.astype(BF16), wo=wo.astype(BF16),
    )


def kernel(x_prompt, x_sample, cache_k_moba, cache_v_moba, cache_k_fox, cache_v_fox, cache_logf_fox, page_table,
           g_norm, w_in, b_forget, b_merge, gq_moba, gk_moba, gq_fox, gk_fox, w_branch_moba, w_branch_fox, w_out):
    depth = w_in.shape[0]
    batch, seq, d = x_prompt.shape
    rows, dec_seq, _ = x_sample.shape
    assert dec_seq == 1 and seq % MOBA_BLOCK == 0
    n_kv_a, n_kv_b = cache_k_moba.shape[3], cache_k_fox.shape[3]
    hb = b_forget.shape[1]
    wa_w, wb_w = w_branch_moba.shape[1], w_branch_fox.shape[1]
    ha = wa_w // HEAD_DIM
    assert wb_w == hb * HEAD_DIM
    kva, kvb = n_kv_a * HEAD_DIM, n_kv_b * HEAD_DIM
    slopes = 2.0 ** (-8.0 * jnp.arange(1, ha + 1, dtype=F32) / ha)
    past = page_table.shape[1] * cache_k_moba.shape[2]
    pos = jnp.arange(past + SUBLANES)[:, None]
    alibi_decode = jnp.where(pos < past, (-LOG2E) * slopes[None, :] * (past - pos).astype(F32),
                             jnp.where(pos == past, 0.0, NEG_INF))

    xp = x_prompt.reshape(batch * seq, d)
    xs = x_sample.reshape(rows, d)
    outs_p, outs_s = [], []
    for l in range(depth):
        dims = (d, wa_w, wb_w, kva, kvb, hb)
        wt = w_in[l].T
        wts = _layer_weights(b_forget[l], b_merge[l], gq_moba[l], gk_moba[l], gq_fox[l], gk_fox[l],
                             w_branch_moba[l], w_branch_fox[l], w_out[l], dims)
        pp = _project(xp, g_norm[l], wt, wts, dims)
        kmean = _kmean(pp["ka"], batch, seq)
        oa = _moba_prompt(pp["q16"], 0, pp["ka16"], pp["va16"], kmean, slopes, batch, seq, ha, n_kv_a)
        group_b = hb // n_kv_b
        c_row = _cumsum_rows(pp["logf"].reshape(batch, seq, hb).transpose(0, 2, 1))
        c_col = c_row.reshape(batch, n_kv_b, group_b, seq).transpose(1, 0, 3, 2).reshape(n_kv_b, batch * seq, group_b)
        ob = _fox_prompt(pp["q16"], wa_w, pp["kb16"], pp["vb16"], c_col,
                         c_row.reshape(batch, n_kv_b, group_b, seq), batch, seq, hb, n_kv_b)
        xp = _merge(xp, oa, ob, pp["z"], pp["gates"], wts["wa"], wts["wb"], wts["wo"])
        outs_p.append((pp["ka"], pp["va"], pp["kb"], pp["vb"], pp["logf"]))
        ps = _project(xs, g_norm[l], wt, wts, dims)
        oa_s = _decode_attention(_moba_decode_kernel, l, page_table, ps["q16"][:, :wa_w], ps["ka"], ps["va"],
                                 alibi_decode, False, [cache_k_moba, cache_v_moba], n_kv_a)
        ob_s = _decode_attention(_fox_decode_kernel, l, page_table, ps["q16"][:, wa_w:], ps["kb"], ps["vb"],
                                 ps["logf"], True, [cache_k_fox, cache_v_fox, _page_cumsum(cache_logf_fox[l])], n_kv_b)
        xs = _merge(xs, oa_s, ob_s, ps["z"], ps["gates"], wts["wa"], wts["wb"], wts["wo"])
        outs_s.append((ps["ka"], ps["va"], ps["kb"], ps["vb"], ps["logf"]))

    def stacked(outs, i, shape):
        return jnp.stack([o[i].reshape(shape) for o in outs])

    res = [xp.reshape(batch, seq, d), xs.reshape(rows, 1, d)]
    for outs, lead in ((outs_p, (batch, seq)), (outs_s, (rows, 1))):
        res += [stacked(outs, 0, lead + (n_kv_a, HEAD_DIM)), stacked(outs, 1, lead + (n_kv_a, HEAD_DIM)),
                stacked(outs, 2, lead + (n_kv_b, HEAD_DIM)), stacked(outs, 3, lead + (n_kv_b, HEAD_DIM)),
                stacked(outs, 4, lead + (hb,))]
    return tuple(res)
```

```python
import functools

import jax
import jax.numpy as jnp
from jax import lax
from jax.experimental import pallas as pl
from jax.experimental.pallas import tpu as pltpu

HEAD_DIM = 128
MOBA_BLOCK = 256
MOBA_TOPK = 3
NORM_EPS = 1e-6
NEG_INF = -1e30
M_INIT = 0.5 * NEG_INF
SCALE = HEAD_DIM ** -0.5
LOG2E = 1.4426950408889634
LANES = 128
SUBLANES = 8
VMEM_LIMIT = 52 * 1024 * 1024
ATTN_TILE = 512

F32 = jnp.float32
BF16 = jnp.bfloat16
HI = lax.Precision.HIGHEST
NT_DIMS = (((1,), (1,)), ((), ()))
TN_DIMS = (((0,), (0,)), ((), ()))


def _params(*sem):
    return pltpu.CompilerParams(dimension_semantics=sem, vmem_limit_bytes=VMEM_LIMIT)


def _row_tile(m, want):
    t = min(m, want)
    assert m % t == 0
    return t


def _head_norm(a, g):
    ms = jnp.mean(a * a, axis=-1, keepdims=True)
    return a * lax.rsqrt(ms + NORM_EPS) * g


def _log_sigmoid(x):
    return jnp.minimum(x, 0.0) - jnp.log1p(jnp.exp(-jnp.abs(x)))


def _proj_q_kernel(h_ref, w_ref, g_ref, o_ref):
    acc = lax.dot_general(h_ref[...], w_ref[...], NT_DIMS, preferred_element_type=F32)
    for j in range(acc.shape[1] // HEAD_DIM):
        sl = slice(j * HEAD_DIM, (j + 1) * HEAD_DIM)
        o_ref[:, sl] = (_head_norm(acc[:, sl], g_ref[:, sl]) * (SCALE * LOG2E)).astype(o_ref.dtype)


def _proj_kv_kernel(h_ref, w_ref, g_ref, *refs, n_kv, n_forget):
    if n_forget:
        wf_ref, bf_ref, *refs = refs
    k32_hbm, v32_hbm, k16_ref, v16_ref, *refs = refs
    if n_forget:
        logf_ref, *refs = refs
    kbuf, vbuf, sems = refs
    kvw = n_kv * HEAD_DIM
    tm = h_ref.shape[0]
    i = pl.program_id(1)
    last = pl.num_programs(1) - 1
    slot = lax.rem(i, 2)

    def writebacks(step, slot):
        rows = pl.ds(pl.multiple_of(step * tm, tm), tm)
        return [pltpu.make_async_copy(buf.at[slot, :, j * HEAD_DIM:(j + 1) * HEAD_DIM], dst.at[rows, j, :],
                                      sems.at[a, slot])
                for a, (buf, dst) in enumerate(((kbuf, k32_hbm), (vbuf, v32_hbm))) for j in range(n_kv)]

    @pl.when(i >= 2)
    def _():
        for c in writebacks(i - 2, slot):
            c.wait()

    h = h_ref[...]
    acc = lax.dot_general(h, w_ref[...], NT_DIMS, preferred_element_type=F32)
    for j in range(n_kv):
        sl = slice(j * HEAD_DIM, (j + 1) * HEAD_DIM)
        kn = _head_norm(acc[:, sl], g_ref[:, sl])
        kbuf[slot, :, sl] = kn
        k16_ref[:, sl] = kn.astype(BF16)
        v = acc[:, kvw + j * HEAD_DIM:kvw + (j + 1) * HEAD_DIM]
        vbuf[slot, :, sl] = v
        v16_ref[:, sl] = v.astype(BF16)
    if n_forget:
        f = lax.dot_general(h, wf_ref[...].astype(BF16), NT_DIMS, preferred_element_type=F32)
        logf_ref[...] = _log_sigmoid(f[:, :n_forget] + bf_ref[:, :n_forget])
    for c in writebacks(i, slot):
        c.start()

    @pl.when((i == last) & (i >= 1))
    def _():
        for c in writebacks(i - 1, 1 - slot):
            c.wait()

    @pl.when(i == last)
    def _():
        for c in writebacks(i, slot):
            c.wait()


def _proj_plain_kernel(h_ref, w_ref, o_ref):
    o_ref[...] = lax.dot_general(h_ref[...], w_ref[...], NT_DIMS, preferred_element_type=F32).astype(o_ref.dtype)


def _proj_gate_kernel(h_ref, w_ref, b_ref, o_ref):
    acc = lax.dot_general(h_ref[...], w_ref[...], NT_DIMS, preferred_element_type=F32)
    o_ref[...] = jax.nn.sigmoid(acc + b_ref[...]).astype(o_ref.dtype)


def _with_cached_weights(body, h_ref, w_ref, *refs):
    *refs, w16_sc = refs

    @pl.when(pl.program_id(1) == 0)
    def _():
        w16_sc[...] = w_ref[...].astype(BF16)

    body(h_ref, w16_sc, *refs)


def _with_rmsnorm(body, n_inputs, x_ref, w_ref, gn_ref, *refs):
    ins, (h_ref, *rest) = refs[:n_inputs], refs[n_inputs:]
    x = x_ref[...]
    ms = jnp.mean(x * x, axis=-1, keepdims=True)
    h_ref[...] = (x * lax.rsqrt(ms + NORM_EPS) * gn_ref[...]).astype(h_ref.dtype)
    body(h_ref, w_ref, *ins, *rest)


def _weight_rows(wt, n_rows, first, stride=0):
    return pl.BlockSpec((pl.Element(n_rows), pl.Element(wt.shape[1])),
                        lambda j, i: (pl.multiple_of(first + stride * j, SUBLANES), 0))


def _proj(body, h, wt, rows, tn, extras, outs, forget_row=None, norm_gain=None):
    m, d = h.shape
    first, stride, count = rows
    assert first % SUBLANES == 0 and stride % SUBLANES == 0
    tm = _row_tile(m, 1024 if norm_gain is None else 512)
    in_specs = [pl.BlockSpec((tm, d), lambda j, i: (i, 0)), _weight_rows(wt, tn, first, stride)]
    args = [h, wt]
    if norm_gain is not None:
        assert count == 1
        in_specs.append(pl.BlockSpec((1, d), lambda j, i: (0, 0)))
        args.append(norm_gain)
        outs = [(BF16, (d,))] + list(outs)
        body = functools.partial(_with_rmsnorm, body, len(extras) + (forget_row is not None))
    for arr, width, base in extras[:1]:
        in_specs.append(pl.BlockSpec((arr.shape[0], width), lambda j, i, base=base: (0, base + j)))
        args.append(arr)
    if forget_row is not None:
        in_specs.append(_weight_rows(wt, LANES, forget_row))
        args.append(wt)
    for arr, width, base in extras[1:]:
        in_specs.append(pl.BlockSpec((arr.shape[0], width), lambda j, i, base=base: (0, base + j)))
        args.append(arr)
    out_specs, out_shape, staging = [], [], []
    for dt, trail in outs:
        if len(trail) == 2:
            assert count == 1
            out_specs.append(pl.BlockSpec(memory_space=pl.ANY))
            staging.append(pltpu.VMEM((2, tm, trail[0] * trail[1]), dt))
        else:
            out_specs.append(pl.BlockSpec((tm, trail[0] // count), lambda j, i: (i, j)))
        out_shape.append(jax.ShapeDtypeStruct((m,) + trail, dt))
    if staging:
        staging.append(pltpu.SemaphoreType.DMA((len(staging), 2)))
    return pl.pallas_call(
        functools.partial(_with_cached_weights, body),
        grid=(count, m // tm),
        in_specs=in_specs,
        out_specs=out_specs,
        out_shape=out_shape,
        scratch_shapes=staging + [pltpu.VMEM((tn, d), BF16)],
        compiler_params=_params("parallel", "arbitrary"),
        name="proj",
    )(*args)


def _kmean_kernel(k_ref, o_ref, *, nblk, n_kv):
    o_ref[...] = jnp.zeros_like(o_ref)
    for j in range(n_kv):
        for n in range(nblk):
            blk = k_ref[n * MOBA_BLOCK:(n + 1) * MOBA_BLOCK, j, :]
            o_ref[0, j, n:n + 1, :] = jnp.sum(blk, axis=0, keepdims=True) * (1.0 / MOBA_BLOCK)


def _kmean(k32, batch, seq):
    n_kv = k32.shape[1]
    nblk = seq // MOBA_BLOCK
    assert nblk <= LANES
    return pl.pallas_call(
        functools.partial(_kmean_kernel, nblk=nblk, n_kv=n_kv),
        grid=(batch,),
        in_specs=[pl.BlockSpec((seq, n_kv, HEAD_DIM), lambda b: (b, 0, 0))],
        out_specs=pl.BlockSpec((1, n_kv, LANES, HEAD_DIM), lambda b: (b, 0, 0, 0)),
        out_shape=jax.ShapeDtypeStruct((batch, n_kv, LANES, HEAD_DIM), F32),
        compiler_params=_params("parallel"),
        name="kmean",
    )(k32)


def _split3(x):
    hi = x.astype(BF16).astype(F32)
    r = x - hi
    mid = r.astype(BF16).astype(F32)
    return hi, mid, (r - mid).astype(BF16).astype(F32)


def _store_wide_queries(qw_sc, q_ref, coef_rows):
    tq = q_ref.shape[0]
    for h, coef in enumerate(coef_rows):
        qw_sc[h * tq:(h + 1) * tq, :HEAD_DIM] = q_ref[:, h * HEAD_DIM:(h + 1) * HEAD_DIM]
        qw_sc[h * tq:(h + 1) * tq, HEAD_DIM:] = jnp.broadcast_to(coef.astype(BF16), (tq, LANES))


def _qk_logits_into(buf, block, k, kx, qw_sc):
    s_ref, mx_ref = buf
    s = lax.dot_general(jnp.concatenate([k, kx], axis=1), qw_sc[...], NT_DIMS, preferred_element_type=F32)
    s_ref[...] = s
    for c in range(s.shape[0] // block):
        mx_ref[c:c + 1, :] = jnp.max(s[c * block:(c + 1) * block], axis=0, keepdims=True)


def _pipelined_tiles(qi, s_refs, logits_into, process):
    s0, s1 = s_refs
    logits_into(s0, 0)

    def pair(kp, carry):
        t = 2 * kp
        logits_into(s1, t + 1)
        process(s0, t, False)
        logits_into(s0, t + 2)
        process(s1, t + 1, False)
        return carry

    lax.fori_loop(0, qi // 2, pair, 0)

    @pl.when(qi % 2 == 0)
    def _():
        process(s0, qi, True)

    @pl.when(qi % 2 == 1)
    def _():
        logits_into(s1, qi)
        process(s0, qi - 1, False)
        process(s1, qi, True)


def _softmax_tile_t(head_parts, v, m_sc, l_sc, acc_sc):
    heads = range(len(head_parts))
    tq = head_parts[0][0][0].shape[1]
    m_prev = [m_sc[h] for h in heads]
    m_new = []
    for h in heads:
        m = m_prev[h]
        for u, rb, umax in head_parts[h]:
            m = jnp.maximum(m, (jnp.max(u, axis=0, keepdims=True) if umax is None else umax) + rb)
        m_new.append(m)
    alphas, p16s = [], []
    for h in heads:
        alpha = jnp.exp2(m_prev[h] - m_new[h])
        l_new = alpha * l_sc[h]
        ps = []
        for u, rb, _ in head_parts[h]:
            p = jnp.exp2(u - (m_new[h] - rb))
            l_new = l_new + jnp.sum(p, axis=0, keepdims=True)
            ps.append(p.astype(v.dtype))
        p16s.append(ps[0] if len(ps) == 1 else jnp.concatenate(ps, axis=0))
        alphas.append(alpha)
        l_sc[h] = l_new
        m_sc[h] = m_new[h]
    pv = lax.dot_general(v, jnp.concatenate(p16s, axis=1), TN_DIMS, preferred_element_type=F32)
    for h in heads:
        acc_sc[h] = alphas[h] * acc_sc[h] + pv[:, h * tq:(h + 1) * tq]


def _init_softmax_state(m_sc, l_sc, acc_sc):
    m_sc[...] = jnp.full(m_sc.shape, M_INIT, F32)
    l_sc[...] = jnp.zeros(l_sc.shape, F32)
    acc_sc[...] = jnp.zeros(acc_sc.shape, F32)


def _write_heads(o_ref, l_sc, acc_sc, group):
    for h in range(group):
        o_ref[:, h * HEAD_DIM:(h + 1) * HEAD_DIM] = (acc_sc[h] / l_sc[h]).T.astype(o_ref.dtype)


def _moba_prompt_kernel(slopes_ref, q_ref, k_ref, v_ref, kmean_ref, o_ref, m_sc, l_sc, acc_sc, selb_sc, qw_sc, kx_sc,
                        s0_sc, s1_sc, mx0_sc, mx1_sc, *, group, nblk):
    tq = q_ref.shape[0]
    per_tile = tq // MOBA_BLOCK
    kvh = pl.program_id(1)
    qi = pl.program_id(2)
    _init_softmax_state(m_sc, l_sc, acc_sc)

    qcol = lax.broadcasted_iota(jnp.int32, (1, tq), 1)
    own = qi * per_tile + qcol // MOBA_BLOCK
    blk = lax.broadcasted_iota(jnp.int32, (nblk, tq), 0)
    km = kmean_ref[0, 0, :-(-nblk // SUBLANES) * SUBLANES, :]
    for h in range(group):
        qh = q_ref[:, h * HEAD_DIM:(h + 1) * HEAD_DIM].astype(F32)
        g = lax.dot_general(km, qh, NT_DIMS, precision=HI, preferred_element_type=F32)[:nblk]
        cnt = jnp.zeros((nblk, tq), jnp.int32)
        for m in range(nblk):
            gm = g[m:m + 1, :]
            beats = (gm > g) | ((gm == g) & (blk > m))
            cnt = cnt + (beats & (own > m)).astype(jnp.int32)
        sel = (blk < own) & (cnt < MOBA_TOPK)
        selb_sc[h] = jnp.where(sel, 0.0, NEG_INF)

    slope2 = [slopes_ref[kvh * group + h] * LOG2E for h in range(group)]
    lane = lax.broadcasted_iota(jnp.int32, (1, LANES), 1)
    coefs = []
    for h in range(group):
        hi, mid, lo = _split3(jnp.zeros((1, LANES), F32) + slope2[h])
        coefs.append(jnp.where(lane < 6, jnp.where(lane % 3 == 0, hi, jnp.where(lane % 3 == 1, mid, lo)), 0.0))
    _store_wide_queries(qw_sc, q_ref, coefs)
    koff = lax.broadcasted_iota(jnp.int32, (tq, LANES), 0)
    klane = lax.broadcasted_iota(jnp.int32, (tq, LANES), 1)
    kx_sc[...] = jnp.where(klane < 3, (koff // MOBA_BLOCK) * MOBA_BLOCK,
                           jnp.where(klane < 6, koff % MOBA_BLOCK, 0)).astype(F32).astype(BF16)

    def logits_into(buf, ki):
        _qk_logits_into(buf, MOBA_BLOCK, k_ref[pl.ds(pl.multiple_of(ki * tq, tq), tq), :], kx_sc[...], qw_sc)

    def process(buf, ki, diagonal):
        s_ref, mx_ref = buf
        parts = []
        for h in range(group):
            query_bias = -slope2[h] * (((qi - ki) * tq).astype(F32) + qcol.astype(F32))
            parts.append([])
            for c in range(per_tile):
                kb = ki * per_tile + c
                u = s_ref[c * MOBA_BLOCK:(c + 1) * MOBA_BLOCK, h * tq:(h + 1) * tq]
                selrow = selb_sc[h, pl.ds(kb, 1), :]
                umax = mx_ref[c:c + 1, h * tq:(h + 1) * tq]
                if diagonal:
                    rel = (lax.broadcasted_iota(jnp.int32, (MOBA_BLOCK, tq), 1)
                           - lax.broadcasted_iota(jnp.int32, (MOBA_BLOCK, tq), 0))
                    u = jnp.where(rel >= c * MOBA_BLOCK, u, NEG_INF)
                    selrow = jnp.where(own > kb, selrow, 0.0)
                    umax = None
                parts[h].append((u, selrow + query_bias, umax))
        _softmax_tile_t(parts, v_ref[pl.ds(pl.multiple_of(ki * tq, tq), tq), :], m_sc, l_sc, acc_sc)

    _pipelined_tiles(qi, ((s0_sc, mx0_sc), (s1_sc, mx1_sc)), logits_into, process)
    _write_heads(o_ref, l_sc, acc_sc, group)


def _moba_prompt(q16, q_col0, k16, v16, kmean, slopes, batch, seq, n_heads, n_kv):
    group = n_heads // n_kv
    tq = _row_tile(seq, ATTN_TILE)
    assert tq % MOBA_BLOCK == 0
    nq = seq // tq
    nblk = seq // MOBA_BLOCK
    gw = group * HEAD_DIM
    qb0 = q_col0 // gw
    return pl.pallas_call(
        functools.partial(_moba_prompt_kernel, group=group, nblk=nblk),
        grid=(batch, n_kv, nq),
        in_specs=[
            pl.BlockSpec(memory_space=pltpu.SMEM),
            pl.BlockSpec((tq, gw), lambda b, j, i: (b * nq + i, qb0 + j)),
            pl.BlockSpec((seq, HEAD_DIM), lambda b, j, i: (b, j)),
            pl.BlockSpec((seq, HEAD_DIM), lambda b, j, i: (b, j)),
            pl.BlockSpec((1, 1, LANES, HEAD_DIM), lambda b, j, i: (b, j, 0, 0)),
        ],
        out_specs=pl.BlockSpec((tq, gw), lambda b, j, i: (b * nq + i, j)),
        out_shape=jax.ShapeDtypeStruct((batch * seq, n_heads * HEAD_DIM), F32),
        scratch_shapes=[
            pltpu.VMEM((group, 1, tq), F32),
            pltpu.VMEM((group, 1, tq), F32),
            pltpu.VMEM((group, HEAD_DIM, tq), F32),
            pltpu.VMEM((group, nblk, tq), F32),
            pltpu.VMEM((group * tq, 2 * HEAD_DIM), BF16),
            pltpu.VMEM((tq, LANES), BF16),
            pltpu.VMEM((tq, group * tq), F32),
            pltpu.VMEM((tq, group * tq), F32),
            pltpu.VMEM((SUBLANES, group * tq), F32),
            pltpu.VMEM((SUBLANES, group * tq), F32),
        ],
        compiler_params=_params("parallel", "parallel", "arbitrary"),
        name="moba_prompt",
    )(slopes, q16, k16, v16, kmean)


def _fox_prompt_kernel(q_ref, k_ref, v_ref, ccol_ref, crow_ref, o_ref, m_sc, l_sc, acc_sc, qw_sc, kx_sc, s0_sc, s1_sc,
                       mx0_sc, mx1_sc, *, group):
    tq = q_ref.shape[0]
    seq = k_ref.shape[0]
    qi = pl.program_id(2)
    _init_softmax_state(m_sc, l_sc, acc_sc)
    lane = lax.broadcasted_iota(jnp.int32, (1, LANES), 1)
    _store_wide_queries(qw_sc, q_ref, [jnp.where(lane // 3 == h, -1.0, 0.0) for h in range(group)])

    @pl.when(qi == 0)
    def _():
        place = (lax.broadcasted_iota(jnp.int32, (group, LANES), 1) // 3
                 == lax.broadcasted_iota(jnp.int32, (group, LANES), 0)).astype(F32)
        x = jnp.dot(ccol_ref[0] * LOG2E, place, precision=HI, preferred_element_type=F32)
        hi, mid, lo = _split3(x)
        l3 = lax.broadcasted_iota(jnp.int32, (seq, LANES), 1) % 3
        kx_sc[...] = jnp.where(l3 == 0, hi, jnp.where(l3 == 1, mid, lo)).astype(BF16)

    def logits_into(buf, ki):
        off = pl.multiple_of(ki * tq, tq)
        _qk_logits_into(buf, tq, k_ref[pl.ds(off, tq), :], kx_sc[pl.ds(off, tq), :], qw_sc)

    def cq(h):
        return crow_ref[0, 0, h:h + 1, pl.ds(pl.multiple_of(qi * tq, tq), tq)] * LOG2E

    def process(buf, ki, diagonal):
        s_ref, mx_ref = buf
        u = [s_ref[:, h * tq:(h + 1) * tq] for h in range(group)]
        umax = [mx_ref[0:1, h * tq:(h + 1) * tq] for h in range(group)]
        if diagonal:
            krow = lax.broadcasted_iota(jnp.int32, (tq, tq), 0)
            qcol = lax.broadcasted_iota(jnp.int32, (tq, tq), 1)
            u = [jnp.where(qcol >= krow, uh, NEG_INF) for uh in u]
            umax = [None] * group
        _softmax_tile_t([[(u[h], cq(h), umax[h])] for h in range(group)],
                        v_ref[pl.ds(pl.multiple_of(ki * tq, tq), tq), :], m_sc, l_sc, acc_sc)

    _pipelined_tiles(qi, ((s0_sc, mx0_sc), (s1_sc, mx1_sc)), logits_into, process)
    _write_heads(o_ref, l_sc, acc_sc, group)


def _fox_prompt(q16, q_col0, k16, v16, c_col, c_row, batch, seq, n_heads, n_kv):
    group = n_heads // n_kv
    tq = _row_tile(seq, ATTN_TILE)
    nq = seq // tq
    gw = group * HEAD_DIM
    qb0 = q_col0 // gw
    return pl.pallas_call(
        functools.partial(_fox_prompt_kernel, group=group),
        grid=(batch, n_kv, nq),
        in_specs=[
            pl.BlockSpec((tq, gw), lambda b, j, i: (b * nq + i, qb0 + j)),
            pl.BlockSpec((seq, HEAD_DIM), lambda b, j, i: (b, j)),
            pl.BlockSpec((seq, HEAD_DIM), lambda b, j, i: (b, j)),
            pl.BlockSpec((1, seq, group), lambda b, j, i: (j, b, 0)),
            pl.BlockSpec((1, 1, group, seq), lambda b, j, i: (b, j, 0, 0)),
        ],
        out_specs=pl.BlockSpec((tq, gw), lambda b, j, i: (b * nq + i, j)),
        out_shape=jax.ShapeDtypeStruct((batch * seq, n_heads * HEAD_DIM), F32),
        scratch_shapes=[
            pltpu.VMEM((group, 1, tq), F32),
            pltpu.VMEM((group, 1, tq), F32),
            pltpu.VMEM((group, HEAD_DIM, tq), F32),
            pltpu.VMEM((group * tq, 2 * HEAD_DIM), BF16),
            pltpu.VMEM((seq, LANES), BF16),
            pltpu.VMEM((tq, group * tq), F32),
            pltpu.VMEM((tq, group * tq), F32),
            pltpu.VMEM((SUBLANES, group * tq), F32),
            pltpu.VMEM((SUBLANES, group * tq), F32),
        ],
        compiler_params=_params("arbitrary", "arbitrary", "arbitrary"),
        name="fox_prompt",
    )(q16, k16, v16, c_col, c_row)


def _cumsum_kernel(x_ref, o_ref, *, chunk):
    rows, seq = x_ref.shape[1:]
    r = lax.broadcasted_iota(jnp.int32, (chunk, chunk), 0)
    c = lax.broadcasted_iota(jnp.int32, (chunk, chunk), 1)
    upper = (r <= c).astype(F32)
    carry = jnp.zeros((rows, 1), F32)
    for i in range(seq // chunk):
        sl = slice(i * chunk, (i + 1) * chunk)
        cs = jnp.dot(x_ref[0, :, sl], upper, precision=HI, preferred_element_type=F32) + carry
        o_ref[0, :, sl] = cs
        carry = cs[:, chunk - 1:chunk]


def _cumsum_rows(x):
    batch, rows, seq = x.shape
    chunk = 256
    assert seq % chunk == 0
    return pl.pallas_call(
        functools.partial(_cumsum_kernel, chunk=chunk),
        grid=(batch,),
        in_specs=[pl.BlockSpec((1, rows, seq), lambda b: (b, 0, 0))],
        out_specs=pl.BlockSpec((1, rows, seq), lambda b: (b, 0, 0)),
        out_shape=jax.ShapeDtypeStruct(x.shape, F32),
        compiler_params=_params("parallel"),
        name="cumsum",
    )(x)


def _page_copies(pt_ref, row, slot, srcs, bufs, sems, n_pages, page):
    copies = []
    for p in range(n_pages):
        idx = pt_ref[row * n_pages + p]
        rows = pl.ds(p * page, page)
        for a, ((src, lead), buf) in enumerate(zip(srcs, bufs)):
            src_page = src.at[lead + (idx,)]
            if len(buf.shape) == 4:
                for j in range(buf.shape[1]):
                    copies.append(pltpu.make_async_copy(src_page.at[:, j, :], buf.at[slot, j, rows, :], sems.at[a, slot]))
            else:
                copies.append(pltpu.make_async_copy(src_page, buf.at[slot, rows], sems.at[a, slot]))
    return copies


def _fetch_rows(pt_ref, srcs, bufs, sems, n_pages, page):
    b = pl.program_id(0)
    nb = pl.num_programs(0)
    slot = lax.rem(b, 2)

    def start_all(copies):
        for n, c in enumerate(copies):
            c.start(priority=n % 2)

    @pl.when(b == 0)
    def _():
        start_all(_page_copies(pt_ref, b, slot, srcs, bufs, sems, n_pages, page))

    @pl.when(b + 1 < nb)
    def _():
        start_all(_page_copies(pt_ref, b + 1, 1 - slot, srcs, bufs, sems, n_pages, page))

    for c in _page_copies(pt_ref, b, slot, srcs, bufs, sems, n_pages, page):
        c.wait()
    return slot


def _append_new_row(buf, slot, past, new):
    first = lax.broadcasted_iota(jnp.int32, (SUBLANES, HEAD_DIM), 0) == 0
    for j in range(new.shape[0]):
        buf[slot, j, past:past + SUBLANES, :] = jnp.where(first, new[j:j + 1, :], 0.0)


def _tail_bias(heads):
    first = lax.broadcasted_iota(jnp.int32, (SUBLANES, heads), 0) == 0
    return jnp.where(first, 0.0, NEG_INF)


def _group_queries(q, j, group):
    r = lax.broadcasted_iota(jnp.int32, q.shape, 0)
    return jnp.where(r // group == j, q, jnp.zeros_like(q))


def _decode_scores(kb, q, n_kv):
    group = q.shape[0] // n_kv
    s = None
    for j0 in range(0, n_kv, 2):
        js = range(j0, min(j0 + 2, n_kv))
        k2 = jnp.concatenate([kb[j].astype(BF16) for j in js], axis=1)
        q2 = jnp.concatenate([_group_queries(q, j, group) for j in js], axis=1)
        sj = lax.dot_general(k2, q2, NT_DIMS, preferred_element_type=F32)
        s = sj if s is None else s + sj
    return s


def _decode_softmax_pv(s, vb, n_kv):
    heads = s.shape[1]
    group = heads // n_kv
    m = jnp.max(s, axis=0, keepdims=True)
    p = jnp.exp2(s - m)
    l = jnp.sum(p, axis=0, keepdims=True)
    p16 = p.astype(BF16)
    o_t = jnp.concatenate(
        [lax.dot_general(vb[j].astype(BF16), p16, TN_DIMS, preferred_element_type=F32) for j in range(n_kv)],
        axis=0) / l
    o = o_t.T
    return jnp.concatenate(
        [o[h:h + 1, (h // group) * HEAD_DIM:(h // group + 1) * HEAD_DIM] for h in range(heads)], axis=0)


def _moba_decode_kernel(pt_ref, q_ref, knew_ref, vnew_ref, alibi_ref, k_hbm, v_hbm, o_ref, kbuf, vbuf, sems,
                        *, layer, n_kv, n_pages, page):
    slot = _fetch_rows(pt_ref, ((k_hbm, (layer,)), (v_hbm, (layer,))), (kbuf, vbuf), sems, n_pages, page)
    past = n_pages * page
    nblk = past // MOBA_BLOCK
    _append_new_row(kbuf, slot, past, knew_ref[0])
    _append_new_row(vbuf, slot, past, vnew_ref[0])
    kb, vb = kbuf.at[slot], vbuf.at[slot]
    q = q_ref[0]
    heads = q.shape[0]
    group = heads // n_kv

    g = None
    for j in range(n_kv):
        means = jnp.concatenate(
            [jnp.sum(kb[j, n * MOBA_BLOCK:(n + 1) * MOBA_BLOCK, :], axis=0, keepdims=True) for n in range(nblk)],
            axis=0) * (1.0 / MOBA_BLOCK)
        gj = lax.dot_general(means, _group_queries(q, j, group).astype(F32), NT_DIMS, precision=HI,
                             preferred_element_type=F32)
        g = gj if g is None else g + gj
    blk = lax.broadcasted_iota(jnp.int32, g.shape, 0)
    cnt = jnp.zeros(g.shape, jnp.int32)
    for m in range(nblk):
        gm = g[m:m + 1, :]
        cnt = cnt + ((gm > g) | ((gm == g) & (blk > m))).astype(jnp.int32)
    sel = cnt < MOBA_TOPK
    keep = jnp.concatenate([jnp.broadcast_to(sel[n:n + 1, :], (MOBA_BLOCK, heads)) for n in range(nblk)]
                           + [jnp.ones((SUBLANES, heads), jnp.bool_)], axis=0)
    s = jnp.where(keep, _decode_scores(kb, q, n_kv) + alibi_ref[...], NEG_INF)
    o_ref[0] = _decode_softmax_pv(s, vb, n_kv).astype(o_ref.dtype)


def _fox_decode_kernel(pt_ref, q_ref, knew_ref, vnew_ref, fnew_ref, k_hbm, v_hbm, c_hbm, o_ref, kbuf, vbuf, cbuf, sems,
                       *, layer, n_kv, n_pages, page):
    slot = _fetch_rows(pt_ref, ((k_hbm, (layer,)), (v_hbm, (layer,)), (c_hbm, ())), (kbuf, vbuf, cbuf), sems,
                       n_pages, page)
    past = n_pages * page
    _append_new_row(kbuf, slot, past, knew_ref[0])
    _append_new_row(vbuf, slot, past, vnew_ref[0])
    kb, vb = kbuf.at[slot], vbuf.at[slot]
    q = q_ref[0]
    heads = q.shape[0]

    carry = jnp.zeros((1, heads), F32)
    cums = []
    for j in range(n_pages):
        cs = cbuf[slot, j * page:(j + 1) * page, :] + carry
        cums.append(cs)
        carry = cs[page - 1:page, :]
    c_now = carry + fnew_ref[0]
    bias = [(c_now - cs) * LOG2E for cs in cums]
    s = _decode_scores(kb, q, n_kv) + jnp.concatenate(bias + [_tail_bias(heads)], axis=0)
    o_ref[0] = _decode_softmax_pv(s, vb, n_kv).astype(o_ref.dtype)


def _decode_attention(body, layer, page_table, q16, k_new, v_new, extra, extra_per_row, caches, n_kv):
    rows, n_pages = page_table.shape
    heads = q16.shape[1] // HEAD_DIM
    page = caches[0].shape[2]
    past = n_pages * page
    assert past % MOBA_BLOCK == 0
    row_spec = lambda shape: pl.BlockSpec((1,) + shape, lambda b, pt: (b, 0, 0))
    in_specs = [row_spec((heads, HEAD_DIM)), row_spec((n_kv, HEAD_DIM)), row_spec((n_kv, HEAD_DIM))]
    if extra_per_row:
        in_specs.append(row_spec((1, heads)))
        extra = extra.reshape(rows, 1, heads)
    else:
        in_specs.append(pl.BlockSpec(extra.shape, lambda b, pt: (0, 0)))
    in_specs += [pl.BlockSpec(memory_space=pl.ANY)] * len(caches)
    scratch = [pltpu.VMEM((2, n_kv, past + SUBLANES, HEAD_DIM) if c.ndim == 5 else (2, past, c.shape[2]), F32)
               for c in caches]
    scratch.append(pltpu.SemaphoreType.DMA((len(caches), 2)))
    return pl.pallas_call(
        functools.partial(body, layer=layer, n_kv=n_kv, n_pages=n_pages, page=page),
        grid_spec=pltpu.PrefetchScalarGridSpec(
            num_scalar_prefetch=1,
            grid=(rows,),
            in_specs=in_specs,
            out_specs=pl.BlockSpec((1, heads, HEAD_DIM), lambda b, pt: (b, 0, 0)),
            scratch_shapes=scratch,
        ),
        out_shape=jax.ShapeDtypeStruct((rows, heads, HEAD_DIM), F32),
        compiler_params=_params("arbitrary"),
        name="decode_attention",
    )(page_table.reshape(-1), q16.reshape(rows, heads, HEAD_DIM), k_new, v_new, extra, *caches
      ).reshape(rows, heads * HEAD_DIM)


def _page_cumsum_kernel(x_ref, o_ref):
    page = x_ref.shape[1]
    r = lax.broadcasted_iota(jnp.int32, (page, page), 0)
    c = lax.broadcasted_iota(jnp.int32, (page, page), 1)
    o_ref[...] = jnp.dot(x_ref[...], (r <= c).astype(F32), precision=HI, preferred_element_type=F32)


def _page_cumsum(logf_cache):
    pool, page, heads = logf_cache.shape
    x = logf_cache.transpose(0, 2, 1).reshape(pool * heads, page)
    tm = _row_tile(pool * heads, 2048)
    out = pl.pallas_call(
        _page_cumsum_kernel,
        grid=(pool * heads // tm,),
        in_specs=[pl.BlockSpec((tm, page), lambda i: (i, 0))],
        out_specs=pl.BlockSpec((tm, page), lambda i: (i, 0)),
        out_shape=jax.ShapeDtypeStruct((pool * heads, page), F32),
        compiler_params=_params("parallel"),
        name="page_cumsum",
    )(x)
    return out.reshape(pool, heads, page).transpose(0, 2, 1)


def _merge_kernel(x_ref, oa_ref, ob_ref, z_ref, g_ref, wa_ref, wb_ref, wo_ref, y_ref):
    wa_w = oa_ref.shape[1]
    d = x_ref.shape[1]

    def branch(o_ref, z, w_ref):
        u = o_ref[...] * (z * jax.nn.sigmoid(z))
        return jnp.dot(u.astype(BF16), w_ref[...], preferred_element_type=F32)

    ya = branch(oa_ref, z_ref[:, :wa_w], wa_ref)
    yb = branch(ob_ref, z_ref[:, wa_w:], wb_ref)
    mixed = g_ref[:, :d] * ya + g_ref[:, d:] * yb
    y_ref[...] = x_ref[...] + jnp.dot(mixed.astype(BF16), wo_ref[...], preferred_element_type=F32)


def _merge(x, oa, ob, z, gates, wa, wb, wo):
    m, d = x.shape
    tm = _row_tile(m, 256)
    rows = lambda width: pl.BlockSpec((tm, width), lambda i: (i, 0))
    resident = lambda w: pl.BlockSpec(w.shape, lambda i: (0, 0), pipeline_mode=pl.Buffered(1))
    return pl.pallas_call(
        _merge_kernel,
        grid=(m // tm,),
        in_specs=[rows(d), rows(oa.shape[1]), rows(ob.shape[1]), rows(z.shape[1]), rows(gates.shape[1]),
                  resident(wa), resident(wb), resident(wo)],
        out_specs=rows(d),
        out_shape=jax.ShapeDtypeStruct((m, d), F32),
        compiler_params=_params("parallel"),
        name="merge",
    )(x, oa, ob, z, gates, wa, wb, wo)


def _project(x2d, g_norm, wt, wts, dims):
    d, wa_w, wb_w, kva, kvb, hb = dims
    n_kv_a, n_kv_b = kva // HEAD_DIM, kvb // HEAD_DIM
    tile = wa_w
    assert wa_w == wb_w == 2 * kva == 2 * kvb
    branch = 2 * wa_w + 2 * kva
    f_row = 2 * branch
    h, ka, va, ka16, va16 = _proj(functools.partial(_proj_kv_kernel, n_kv=n_kv_a, n_forget=0), x2d, wt, (wa_w, 0, 1),
                                  tile, [(wts["g_ka"], tile, 0)],
                                  [(F32, (n_kv_a, HEAD_DIM)), (F32, (n_kv_a, HEAD_DIM)), (BF16, (kva,)),
                                   (BF16, (kva,))], norm_gain=g_norm.reshape(1, d))
    (q16,) = _proj(_proj_q_kernel, h, wt, (0, branch, 2), tile, [(wts["g_q"], tile, 0)], [(BF16, (wa_w + wb_w,))])
    kb, vb, kb16, vb16, logf = _proj(functools.partial(_proj_kv_kernel, n_kv=n_kv_b, n_forget=hb), h, wt,
                                     (branch + wb_w, 0, 1), tile, [(wts["g_kb"], tile, 0), (wts["b_f"], LANES, 0)],
                                     [(F32, (n_kv_b, HEAD_DIM)), (F32, (n_kv_b, HEAD_DIM)), (BF16, (kvb,)),
                                      (BF16, (kvb,)), (F32, (hb,))], forget_row=f_row)
    (z,) = _proj(_proj_plain_kernel, h, wt, (wa_w + 2 * kva, branch, 2), tile, [], [(F32, (wa_w + wb_w,))])
    gate_tile = min(tile, 2 * d)
    (gates,) = _proj(_proj_gate_kernel, h, wt, (f_row + hb, gate_tile, 2 * d // gate_tile), gate_tile,
                     [(wts["b_g"], gate_tile, 0)], [(F32, (2 * d,))])
    return dict(q16=q16, ka=ka, va=va, ka16=ka16, va16=va16, kb=kb, vb=vb, kb16=kb16, vb16=vb16, logf=logf, z=z,
                gates=gates)


def _layer_weights(b_forget, b_merge, gq_a, gk_a, gq_b, gk_b, wa, wb, wo, dims):
    d, wa_w, wb_w, kva, kvb, hb = dims
    return dict(
        g_q=jnp.concatenate([jnp.tile(gq_a, wa_w // HEAD_DIM), jnp.tile(gq_b, wb_w // HEAD_DIM)]).reshape(1, -1),
        g_ka=jnp.concatenate([jnp.tile(gk_a, kva // HEAD_DIM), jnp.zeros((kva,), F32)]).reshape(1, -1),
        g_kb=jnp.concatenate([jnp.tile(gk_b, kvb // HEAD_DIM), jnp.zeros((kvb,), F32)]).reshape(1, -1),
        b_f=jnp.concatenate([b_forget, jnp.zeros((LANES - hb,), F32)]).reshape(1, -1),
        b_g=b_merge.reshape(1, 2 * d),
        wa=wa.astype(BF16), wb=wb.astype(BF16), wo=wo.astype(BF16),
    )


def kernel(x_prompt, x_sample, cache_k_moba, cache_v_moba, cache_k_fox, cache_v_fox, cache_logf_fox, page_table,
           g_norm, w_in, b_forget, b_merge, gq_moba, gk_moba, gq_fox, gk_fox, w_branch_moba, w_branch_fox, w_out):
    depth = w_in.shape[0]
    batch, seq, d = x_prompt.shape
    rows, dec_seq, _ = x_sample.shape
    assert dec_seq == 1 and seq % MOBA_BLOCK == 0
    n_kv_a, n_kv_b = cache_k_moba.shape[3], cache_k_fox.shape[3]
    hb = b_forget.shape[1]
    wa_w, wb_w = w_branch_moba.shape[1], w_branch_fox.shape[1]
    ha = wa_w // HEAD_DIM
    assert wb_w == hb * HEAD_DIM
    kva, kvb = n_kv_a * HEAD_DIM, n_kv_b * HEAD_DIM
    slopes = 2.0 ** (-8.0 * jnp.arange(1, ha + 1, dtype=F32) / ha)
    past = page_table.shape[1] * cache_k_moba.shape[2]
    pos = jnp.arange(past + SUBLANES)[:, None]
    alibi_decode = jnp.where(pos < past, (-LOG2E) * slopes[None, :] * (past - pos).astype(F32),
                             jnp.where(pos == past, 0.0, NEG_INF))

    xp = x_prompt.reshape(batch * seq, d)
    xs = x_sample.reshape(rows, d)
    outs_p, outs_s = [], []
    for l in range(depth):
        dims = (d, wa_w, wb_w, kva, kvb, hb)
        wt = w_in[l].T
        wts = _layer_weights(b_forget[l], b_merge[l], gq_moba[l], gk_moba[l], gq_fox[l], gk_fox[l],
                             w_branch_moba[l], w_branch_fox[l], w_out[l], dims)
        pp = _project(xp, g_norm[l], wt, wts, dims)
        kmean = _kmean(pp["ka"], batch, seq)
        oa = _moba_prompt(pp["q16"], 0, pp["ka16"], pp["va16"], kmean, slopes, batch, seq, ha, n_kv_a)
        group_b = hb // n_kv_b
        c_row = _cumsum_rows(pp["logf"].reshape(batch, seq, hb).transpose(0, 2, 1))
        c_col = c_row.reshape(batch, n_kv_b, group_b, seq).transpose(1, 0, 3, 2).reshape(n_kv_b, batch * seq, group_b)
        ob = _fox_prompt(pp["q16"], wa_w, pp["kb16"], pp["vb16"], c_col,
                         c_row.reshape(batch, n_kv_b, group_b, seq), batch, seq, hb, n_kv_b)
        xp = _merge(xp, oa, ob, pp["z"], pp["gates"], wts["wa"], wts["wb"], wts["wo"])
        outs_p.append((pp["ka"], pp["va"], pp["kb"], pp["vb"], pp["logf"]))
        ps = _project(xs, g_norm[l], wt, wts, dims)
        oa_s = _decode_attention(_moba_decode_kernel, l, page_table, ps["q16"][:, :wa_w], ps["ka"], ps["va"],
                                 alibi_decode, False, [cache_k_moba, cache_v_moba], n_kv_a)
        ob_s = _decode_attention(_fox_decode_kernel, l, page_table, ps["q16"][:, wa_w:], ps["kb"], ps["vb"],
                                 ps["logf"], True, [cache_k_fox, cache_v_fox, _page_cumsum(cache_logf_fox[l])], n_kv_b)
        xs = _merge(xs, oa_s, ob_s, ps["z"], ps["gates"], wts["wa"], wts["wb"], wts["wo"])
        outs_s.append((ps["ka"], ps["va"], ps["kb"], ps["vb"], ps["logf"]))

    def stacked(outs, i, shape):
        return jnp.stack([o[i].reshape(shape) for o in outs])

    res = [xp.reshape(batch, seq, d), xs.reshape(rows, 1, d)]
    for outs, lead in ((outs_p, (batch, seq)), (outs_s, (rows, 1))):
        res += [stacked(outs, 0, lead + (n_kv_a, HEAD_DIM)), stacked(outs, 1, lead + (n_kv_a, HEAD_DIM)),
                stacked(outs, 2, lead + (n_kv_b, HEAD_DIM)), stacked(outs, 3, lead + (n_kv_b, HEAD_DIM)),
                stacked(outs, 4, lead + (hb,))]
    return tuple(res)
```

```python
import functools

import jax
import jax.numpy as jnp
from jax import lax
from jax.experimental import pallas as pl
from jax.experimental.pallas import tpu as pltpu

HEAD_DIM = 128
MOBA_BLOCK = 256
MOBA_TOPK = 3
NORM_EPS = 1e-6
NEG_INF = -1e30
M_INIT = 0.5 * NEG_INF
SCALE = HEAD_DIM ** -0.5
LOG2E = 1.4426950408889634
LANES = 128
SUBLANES = 8
VMEM_LIMIT = 52 * 1024 * 1024
ATTN_TILE = 512

F32 = jnp.float32
BF16 = jnp.bfloat16
HI = lax.Precision.HIGHEST
NT_DIMS = (((1,), (1,)), ((), ()))
TN_DIMS = (((0,), (0,)), ((), ()))


def _params(*sem):
    return pltpu.CompilerParams(dimension_semantics=sem, vmem_limit_bytes=VMEM_LIMIT)


def _row_tile(m, want):
    t = min(m, want)
    assert m % t == 0
    return t


def _head_norm(a, g):
    ms = jnp.mean(a * a, axis=-1, keepdims=True)
    return a * lax.rsqrt(ms + NORM_EPS) * g


def _log_sigmoid(x):
    return jnp.minimum(x, 0.0) - jnp.log1p(jnp.exp(-jnp.abs(x)))


def _proj_q_kernel(h_ref, w_ref, g_ref, o_ref):
    acc = lax.dot_general(h_ref[...], w_ref[...], NT_DIMS, preferred_element_type=F32)
    for j in range(acc.shape[1] // HEAD_DIM):
        sl = slice(j * HEAD_DIM, (j + 1) * HEAD_DIM)
        o_ref[:, sl] = (_head_norm(acc[:, sl], g_ref[:, sl]) * (SCALE * LOG2E)).astype(o_ref.dtype)


def _proj_kv_kernel(h_ref, w_ref, g_ref, *refs, n_kv, n_forget):
    if n_forget:
        wf_ref, bf_ref, *refs = refs
    k32_hbm, v32_hbm, k16_ref, v16_ref, *refs = refs
    if n_forget:
        logf_ref, *refs = refs
    kbuf, vbuf, sems = refs
    kvw = n_kv * HEAD_DIM
    tm = h_ref.shape[0]
    i = pl.program_id(1)
    last = pl.num_programs(1) - 1
    slot = lax.rem(i, 2)

    def writebacks(step, slot):
        rows = pl.ds(pl.multiple_of(step * tm, tm), tm)
        return [pltpu.make_async_copy(buf.at[slot, :, j * HEAD_DIM:(j + 1) * HEAD_DIM], dst.at[rows, j, :],
                                      sems.at[a, slot])
                for a, (buf, dst) in enumerate(((kbuf, k32_hbm), (vbuf, v32_hbm))) for j in range(n_kv)]

    @pl.when(i >= 2)
    def _():
        for c in writebacks(i - 2, slot):
            c.wait()

    h = h_ref[...]
    acc = lax.dot_general(h, w_ref[...], NT_DIMS, preferred_element_type=F32)
    for j in range(n_kv):
        sl = slice(j * HEAD_DIM, (j + 1) * HEAD_DIM)
        kn = _head_norm(acc[:, sl], g_ref[:, sl])
        kbuf[slot, :, sl] = kn
        k16_ref[:, sl] = kn.astype(BF16)
        v = acc[:, kvw + j * HEAD_DIM:kvw + (j + 1) * HEAD_DIM]
        vbuf[slot, :, sl] = v
        v16_ref[:, sl] = v.astype(BF16)
    if n_forget:
        f = lax.dot_general(h, wf_ref[...].astype(BF16), NT_DIMS, preferred_element_type=F32)
        logf_ref[...] = _log_sigmoid(f[:, :n_forget] + bf_ref[:, :n_forget])
    for c in writebacks(i, slot):
        c.start()

    @pl.when((i == last) & (i >= 1))
    def _():
        for c in writebacks(i - 1, 1 - slot):
            c.wait()

    @pl.when(i == last)
    def _():
        for c in writebacks(i, slot):
            c.wait()


def _proj_plain_kernel(h_ref, w_ref, o_ref):
    o_ref[...] = lax.dot_general(h_ref[...], w_ref[...], NT_DIMS, preferred_element_type=F32).astype(o_ref.dtype)


def _proj_gate_kernel(h_ref, w_ref, b_ref, o_ref):
    acc = lax.dot_general(h_ref[...], w_ref[...], NT_DIMS, preferred_element_type=F32)
    o_ref[...] = jax.nn.sigmoid(acc + b_ref[...]).astype(o_ref.dtype)


def _with_cached_weights(body, n_staging, h_ref, w_ref, *refs):
    *refs, w16_sc = refs
    n = len(refs) - n_staging
    w16_out = refs[n - 1]

    @pl.when(pl.program_id(1) == 0)
    def _():
        w16_sc[...] = w_ref[...].astype(BF16)
        w16_out[...] = w16_sc[...]

    body(h_ref, w16_sc, *refs[:n - 1], *refs[n:])


def _with_rmsnorm(body, n_inputs, x_ref, w_ref, gn_ref, *refs):
    ins, (h_ref, *rest) = refs[:n_inputs], refs[n_inputs:]
    x = x_ref[...]
    ms = jnp.mean(x * x, axis=-1, keepdims=True)
    h_ref[...] = (x * lax.rsqrt(ms + NORM_EPS) * gn_ref[...]).astype(h_ref.dtype)
    body(h_ref, w_ref, *ins, *rest)


def _weight_rows(wt, n_rows, first, stride=0):
    return pl.BlockSpec((pl.Element(n_rows), pl.Element(wt.shape[1])),
                        lambda j, i: (pl.multiple_of(first + stride * j, SUBLANES), 0))


def _proj(body, h, wt, rows, tn, extras, outs, forget_row=None, norm_gain=None, w16=None):
    m, d = h.shape
    first, stride, count = rows
    assert first % SUBLANES == 0 and stride % SUBLANES == 0
    tm = _row_tile(m, 1024 if norm_gain is None else 512)
    cast = w16 is None
    in_specs = [pl.BlockSpec((tm, d), lambda j, i: (i, 0)),
                _weight_rows(wt, tn, first, stride) if cast else pl.BlockSpec((tn, d), lambda j, i: (j, 0))]
    args = [h, wt if cast else w16]
    if norm_gain is not None:
        assert count == 1
        in_specs.append(pl.BlockSpec((1, d), lambda j, i: (0, 0)))
        args.append(norm_gain)
        outs = [(BF16, (d,))] + list(outs)
        body = functools.partial(_with_rmsnorm, body, len(extras) + (forget_row is not None))
    for arr, width, base in extras[:1]:
        in_specs.append(pl.BlockSpec((arr.shape[0], width), lambda j, i, base=base: (0, base + j)))
        args.append(arr)
    if forget_row is not None:
        in_specs.append(_weight_rows(wt, LANES, forget_row))
        args.append(wt)
    for arr, width, base in extras[1:]:
        in_specs.append(pl.BlockSpec((arr.shape[0], width), lambda j, i, base=base: (0, base + j)))
        args.append(arr)
    out_specs, out_shape, staging = [], [], []
    for dt, trail in outs:
        if len(trail) == 2:
            assert count == 1
            out_specs.append(pl.BlockSpec(memory_space=pl.ANY))
            staging.append(pltpu.VMEM((2, tm, trail[0] * trail[1]), dt))
        else:
            out_specs.append(pl.BlockSpec((tm, trail[0] // count), lambda j, i: (i, j)))
        out_shape.append(jax.ShapeDtypeStruct((m,) + trail, dt))
    if staging:
        staging.append(pltpu.SemaphoreType.DMA((len(staging), 2)))
    if cast:
        out_specs.append(pl.BlockSpec((tn, d), lambda j, i: (j, 0)))
        out_shape.append(jax.ShapeDtypeStruct((count * tn, d), BF16))
    return pl.pallas_call(
        functools.partial(_with_cached_weights, body, len(staging)) if cast else body,
        grid=(count, m // tm),
        in_specs=in_specs,
        out_specs=out_specs,
        out_shape=out_shape,
        scratch_shapes=staging + ([pltpu.VMEM((tn, d), BF16)] if cast else []),
        compiler_params=_params("parallel", "arbitrary"),
        name="proj",
    )(*args)


def _kmean_kernel(k_ref, o_ref, *, nblk, n_kv):
    o_ref[...] = jnp.zeros_like(o_ref)
    for j in range(n_kv):
        for n in range(nblk):
            blk = k_ref[pl.ds(n * MOBA_BLOCK * n_kv + j, MOBA_BLOCK, stride=n_kv), :]
            o_ref[0, j, n:n + 1, :] = jnp.sum(blk, axis=0, keepdims=True) * (1.0 / MOBA_BLOCK)


def _kmean(k32, batch, seq):
    n_kv = k32.shape[1]
    nblk = seq // MOBA_BLOCK
    assert nblk <= LANES
    k32 = k32.reshape(batch * seq * n_kv, HEAD_DIM)
    return pl.pallas_call(
        functools.partial(_kmean_kernel, nblk=nblk, n_kv=n_kv),
        grid=(batch,),
        in_specs=[pl.BlockSpec((seq * n_kv, HEAD_DIM), lambda b: (b, 0))],
        out_specs=pl.BlockSpec((1, n_kv, LANES, HEAD_DIM), lambda b: (b, 0, 0, 0)),
        out_shape=jax.ShapeDtypeStruct((batch, n_kv, LANES, HEAD_DIM), F32),
        compiler_params=_params("parallel"),
        name="kmean",
    )(k32)


def _split3(x):
    hi = x.astype(BF16).astype(F32)
    r = x - hi
    mid = r.astype(BF16).astype(F32)
    return hi, mid, (r - mid).astype(BF16).astype(F32)


N_SPLIT = 3


def _interleave_terms(terms, lane):
    hi, mid, lo = terms
    return jnp.where(lane % N_SPLIT == 0, hi, jnp.where(lane % N_SPLIT == 1, mid, lo))


def _store_wide_queries(qw_sc, q_ref, coef_rows):
    tq = q_ref.shape[0]
    for h, coef in enumerate(coef_rows):
        qw_sc[h * tq:(h + 1) * tq, :HEAD_DIM] = q_ref[:, h * HEAD_DIM:(h + 1) * HEAD_DIM]
        qw_sc[h * tq:(h + 1) * tq, HEAD_DIM:] = jnp.broadcast_to(coef.astype(BF16), (tq, LANES))


def _qk_logits_into(buf, block, k, kx, qw_sc):
    s_ref, mx_ref = buf
    s = lax.dot_general(jnp.concatenate([k, kx], axis=1), qw_sc[...], NT_DIMS, preferred_element_type=F32)
    s_ref[...] = s
    for c in range(s.shape[0] // block):
        mx_ref[c:c + 1, :] = jnp.max(s[c * block:(c + 1) * block], axis=0, keepdims=True)


def _pipelined_tiles(qi, s_refs, logits_into, process):
    s0, s1 = s_refs
    logits_into(s0, 0)

    def pair(kp, carry):
        t = 2 * kp
        logits_into(s1, t + 1)
        process(s0, t, False)
        logits_into(s0, t + 2)
        process(s1, t + 1, False)
        return carry

    lax.fori_loop(0, qi // 2, pair, 0)

    @pl.when(qi % 2 == 0)
    def _():
        process(s0, qi, True)

    @pl.when(qi % 2 == 1)
    def _():
        logits_into(s1, qi)
        process(s0, qi - 1, False)
        process(s1, qi, True)


def _softmax_tile_t(head_parts, v, m_sc, l_sc, acc_sc):
    heads = range(len(head_parts))
    tq = head_parts[0][0][0].shape[1]
    m_prev = [m_sc[h] for h in heads]
    m_new = []
    for h in heads:
        m = m_prev[h]
        for u, rb, umax in head_parts[h]:
            m = jnp.maximum(m, (jnp.max(u, axis=0, keepdims=True) if umax is None else umax) + rb)
        m_new.append(m)
    alphas, p16s = [], []
    for h in heads:
        alpha = jnp.exp2(m_prev[h] - m_new[h])
        l_new = alpha * l_sc[h]
        ps = []
        for u, rb, _ in head_parts[h]:
            p = jnp.exp2(u - (m_new[h] - rb))
            l_new = l_new + jnp.sum(p, axis=0, keepdims=True)
            ps.append(p.astype(v.dtype))
        p16s.append(ps[0] if len(ps) == 1 else jnp.concatenate(ps, axis=0))
        alphas.append(alpha)
        l_sc[h] = l_new
        m_sc[h] = m_new[h]
    pv = lax.dot_general(v, jnp.concatenate(p16s, axis=1), TN_DIMS, preferred_element_type=F32)
    for h in heads:
        acc_sc[h] = alphas[h] * acc_sc[h] + pv[:, h * tq:(h + 1) * tq]


def _init_softmax_state(m_sc, l_sc, acc_sc):
    m_sc[...] = jnp.full(m_sc.shape, M_INIT, F32)
    l_sc[...] = jnp.zeros(l_sc.shape, F32)
    acc_sc[...] = jnp.zeros(acc_sc.shape, F32)


def _write_heads(o_ref, l_sc, acc_sc, group):
    for h in range(group):
        o_ref[:, h * HEAD_DIM:(h + 1) * HEAD_DIM] = (acc_sc[h] / l_sc[h]).T.astype(o_ref.dtype)


def _moba_prompt_kernel(slopes_ref, q_ref, k_ref, v_ref, kmean_ref, o_ref, m_sc, l_sc, acc_sc, selb_sc, qw_sc, kx_sc,
                        s0_sc, s1_sc, mx0_sc, mx1_sc, *, group, nblk):
    tq = q_ref.shape[0]
    per_tile = tq // MOBA_BLOCK
    kvh = pl.program_id(1)
    qi = pl.program_id(2)
    _init_softmax_state(m_sc, l_sc, acc_sc)

    qcol = lax.broadcasted_iota(jnp.int32, (1, tq), 1)
    own = qi * per_tile + qcol // MOBA_BLOCK
    blk = lax.broadcasted_iota(jnp.int32, (nblk, tq), 0)
    km = kmean_ref[0, 0, :-(-nblk // SUBLANES) * SUBLANES, :]
    for h in range(group):
        qh = q_ref[:, h * HEAD_DIM:(h + 1) * HEAD_DIM].astype(F32)
        g = lax.dot_general(km, qh, NT_DIMS, precision=HI, preferred_element_type=F32)[:nblk]
        cnt = jnp.zeros((nblk, tq), jnp.int32)
        for m in range(nblk):
            gm = g[m:m + 1, :]
            beats = (gm > g) | ((gm == g) & (blk > m))
            cnt = cnt + (beats & (own > m)).astype(jnp.int32)
        sel = (blk < own) & (cnt < MOBA_TOPK)
        selb_sc[h] = jnp.where(sel, 0.0, NEG_INF)

    slope2 = [slopes_ref[kvh * group + h] * LOG2E for h in range(group)]
    lane = lax.broadcasted_iota(jnp.int32, (1, LANES), 1)
    coefs = []
    for h in range(group):
        terms = _split3(jnp.zeros((1, LANES), F32) + slope2[h])
        coefs.append(jnp.where(lane < 2 * N_SPLIT, _interleave_terms(terms, lane), 0.0))
    _store_wide_queries(qw_sc, q_ref, coefs)
    koff = lax.broadcasted_iota(jnp.int32, (tq, LANES), 0)
    klane = lax.broadcasted_iota(jnp.int32, (tq, LANES), 1)
    kx_sc[...] = jnp.where(klane < N_SPLIT, (koff // MOBA_BLOCK) * MOBA_BLOCK,
                           jnp.where(klane < 2 * N_SPLIT, koff % MOBA_BLOCK, 0)).astype(F32).astype(BF16)

    def logits_into(buf, ki):
        _qk_logits_into(buf, MOBA_BLOCK, k_ref[pl.ds(pl.multiple_of(ki * tq, tq), tq), :], kx_sc[...], qw_sc)

    def process(buf, ki, diagonal):
        s_ref, mx_ref = buf
        parts = []
        for h in range(group):
            query_bias = -slope2[h] * (((qi - ki) * tq).astype(F32) + qcol.astype(F32))
            parts.append([])
            for c in range(per_tile):
                kb = ki * per_tile + c
                u = s_ref[c * MOBA_BLOCK:(c + 1) * MOBA_BLOCK, h * tq:(h + 1) * tq]
                selrow = selb_sc[h, pl.ds(kb, 1), :]
                umax = mx_ref[c:c + 1, h * tq:(h + 1) * tq]
                if diagonal:
                    rel = (lax.broadcasted_iota(jnp.int32, (MOBA_BLOCK, tq), 1)
                           - lax.broadcasted_iota(jnp.int32, (MOBA_BLOCK, tq), 0))
                    u = jnp.where(rel >= c * MOBA_BLOCK, u, NEG_INF)
                    selrow = jnp.where(own > kb, selrow, 0.0)
                    umax = None
                parts[h].append((u, selrow + query_bias, umax))
        _softmax_tile_t(parts, v_ref[pl.ds(pl.multiple_of(ki * tq, tq), tq), :], m_sc, l_sc, acc_sc)

    _pipelined_tiles(qi, ((s0_sc, mx0_sc), (s1_sc, mx1_sc)), logits_into, process)
    _write_heads(o_ref, l_sc, acc_sc, group)


def _moba_prompt(q16, q_col0, k16, v16, kmean, slopes, batch, seq, n_heads, n_kv):
    group = n_heads // n_kv
    tq = _row_tile(seq, ATTN_TILE)
    assert tq % MOBA_BLOCK == 0
    nq = seq // tq
    nblk = seq // MOBA_BLOCK
    gw = group * HEAD_DIM
    qb0 = q_col0 // gw
    return pl.pallas_call(
        functools.partial(_moba_prompt_kernel, group=group, nblk=nblk),
        grid=(batch, n_kv, nq),
        in_specs=[
            pl.BlockSpec(memory_space=pltpu.SMEM),
            pl.BlockSpec((tq, gw), lambda b, j, i: (b * nq + i, qb0 + j)),
            pl.BlockSpec((seq, HEAD_DIM), lambda b, j, i: (b, j)),
            pl.BlockSpec((seq, HEAD_DIM), lambda b, j, i: (b, j)),
            pl.BlockSpec((1, 1, LANES, HEAD_DIM), lambda b, j, i: (b, j, 0, 0)),
        ],
        out_specs=pl.BlockSpec((tq, gw), lambda b, j, i: (b * nq + i, j)),
        out_shape=jax.ShapeDtypeStruct((batch * seq, n_heads * HEAD_DIM), F32),
        scratch_shapes=[
            pltpu.VMEM((group, 1, tq), F32),
            pltpu.VMEM((group, 1, tq), F32),
            pltpu.VMEM((group, HEAD_DIM, tq), F32),
            pltpu.VMEM((group, nblk, tq), F32),
            pltpu.VMEM((group * tq, 2 * HEAD_DIM), BF16),
            pltpu.VMEM((tq, LANES), BF16),
            pltpu.VMEM((tq, group * tq), F32),
            pltpu.VMEM((tq, group * tq), F32),
            pltpu.VMEM((SUBLANES, group * tq), F32),
            pltpu.VMEM((SUBLANES, group * tq), F32),
        ],
        compiler_params=_params("parallel", "parallel", "arbitrary"),
        name="moba_prompt",
    )(slopes, q16, k16, v16, kmean)


def _fox_prompt_kernel(q_ref, k_ref, v_ref, ccol_ref, crow_ref, o_ref, m_sc, l_sc, acc_sc, qw_sc, kx_sc, s0_sc, s1_sc,
                       mx0_sc, mx1_sc, *, group):
    tq = q_ref.shape[0]
    seq = k_ref.shape[0]
    qi = pl.program_id(2)
    _init_softmax_state(m_sc, l_sc, acc_sc)
    lane = lax.broadcasted_iota(jnp.int32, (1, LANES), 1)
    _store_wide_queries(qw_sc, q_ref, [jnp.where(lane // N_SPLIT == h, -1.0, 0.0) for h in range(group)])

    @pl.when(qi == 0)
    def _():
        place = (lax.broadcasted_iota(jnp.int32, (group, LANES), 1) // N_SPLIT
                 == lax.broadcasted_iota(jnp.int32, (group, LANES), 0)).astype(F32)
        x = jnp.dot(ccol_ref[0] * LOG2E, place, precision=HI, preferred_element_type=F32)
        kx_sc[...] = _interleave_terms(_split3(x), lax.broadcasted_iota(jnp.int32, (seq, LANES), 1)).astype(BF16)

    def logits_into(buf, ki):
        off = pl.multiple_of(ki * tq, tq)
        _qk_logits_into(buf, tq, k_ref[pl.ds(off, tq), :], kx_sc[pl.ds(off, tq), :], qw_sc)

    def cq(h):
        return crow_ref[0, 0, h:h + 1, pl.ds(pl.multiple_of(qi * tq, tq), tq)] * LOG2E

    def process(buf, ki, diagonal):
        s_ref, mx_ref = buf
        u = [s_ref[:, h * tq:(h + 1) * tq] for h in range(group)]
        umax = [mx_ref[0:1, h * tq:(h + 1) * tq] for h in range(group)]
        if diagonal:
            krow = lax.broadcasted_iota(jnp.int32, (tq, tq), 0)
            qcol = lax.broadcasted_iota(jnp.int32, (tq, tq), 1)
            u = [jnp.where(qcol >= krow, uh, NEG_INF) for uh in u]
            umax = [None] * group
        _softmax_tile_t([[(u[h], cq(h), umax[h])] for h in range(group)],
                        v_ref[pl.ds(pl.multiple_of(ki * tq, tq), tq), :], m_sc, l_sc, acc_sc)

    _pipelined_tiles(qi, ((s0_sc, mx0_sc), (s1_sc, mx1_sc)), logits_into, process)
    _write_heads(o_ref, l_sc, acc_sc, group)


def _fox_prompt(q16, q_col0, k16, v16, c_col, c_row, batch, seq, n_heads, n_kv):
    group = n_heads // n_kv
    tq = _row_tile(seq, ATTN_TILE)
    nq = seq // tq
    gw = group * HEAD_DIM
    qb0 = q_col0 // gw
    return pl.pallas_call(
        functools.partial(_fox_prompt_kernel, group=group),
        grid=(batch, n_kv, nq),
        in_specs=[
            pl.BlockSpec((tq, gw), lambda b, j, i: (b * nq + i, qb0 + j)),
            pl.BlockSpec((seq, HEAD_DIM), lambda b, j, i: (b, j)),
            pl.BlockSpec((seq, HEAD_DIM), lambda b, j, i: (b, j)),
            pl.BlockSpec((1, seq, group), lambda b, j, i: (j, b, 0)),
            pl.BlockSpec((1, 1, group, seq), lambda b, j, i: (b, j, 0, 0)),
        ],
        out_specs=pl.BlockSpec((tq, gw), lambda b, j, i: (b * nq + i, j)),
        out_shape=jax.ShapeDtypeStruct((batch * seq, n_heads * HEAD_DIM), F32),
        scratch_shapes=[
            pltpu.VMEM((group, 1, tq), F32),
            pltpu.VMEM((group, 1, tq), F32),
            pltpu.VMEM((group, HEAD_DIM, tq), F32),
            pltpu.VMEM((group * tq, 2 * HEAD_DIM), BF16),
            pltpu.VMEM((seq, LANES), BF16),
            pltpu.VMEM((tq, group * tq), F32),
            pltpu.VMEM((tq, group * tq), F32),
            pltpu.VMEM((SUBLANES, group * tq), F32),
            pltpu.VMEM((SUBLANES, group * tq), F32),
        ],
        compiler_params=_params("arbitrary", "arbitrary", "arbitrary"),
        name="fox_prompt",
    )(q16, k16, v16, c_col, c_row)


def _cumsum_kernel(x_ref, o_ref, *, chunk):
    rows, seq = x_ref.shape[1:]
    r = lax.broadcasted_iota(jnp.int32, (chunk, chunk), 0)
    c = lax.broadcasted_iota(jnp.int32, (chunk, chunk), 1)
    upper = (r <= c).astype(F32)
    carry = jnp.zeros((rows, 1), F32)
    for i in range(seq // chunk):
        sl = slice(i * chunk, (i + 1) * chunk)
        cs = jnp.dot(x_ref[0, :, sl], upper, precision=HI, preferred_element_type=F32) + carry
        o_ref[0, :, sl] = cs
        carry = cs[:, chunk - 1:chunk]


def _cumsum_rows(x):
    batch, rows, seq = x.shape
    chunk = 256
    assert seq % chunk == 0
    return pl.pallas_call(
        functools.partial(_cumsum_kernel, chunk=chunk),
        grid=(batch,),
        in_specs=[pl.BlockSpec((1, rows, seq), lambda b: (b, 0, 0))],
        out_specs=pl.BlockSpec((1, rows, seq), lambda b: (b, 0, 0)),
        out_shape=jax.ShapeDtypeStruct(x.shape, F32),
        compiler_params=_params("parallel"),
        name="cumsum",
    )(x)


def _page_copies(pt_ref, row, slot, srcs, bufs, sems, n_pages, page):
    copies = []
    for p in range(n_pages):
        idx = pt_ref[row * n_pages + p]
        rows = pl.ds(p * page, page)
        for a, ((src, lead), buf) in enumerate(zip(srcs, bufs)):
            src_page = src.at[lead + (idx,)]
            if len(buf.shape) == 4:
                for j in range(buf.shape[1]):
                    copies.append(pltpu.make_async_copy(src_page.at[:, j, :], buf.at[slot, j, rows, :], sems.at[a, slot]))
            else:
                copies.append(pltpu.make_async_copy(src_page, buf.at[slot, rows], sems.at[a, slot]))
    return copies


def _fetch_rows(pt_ref, srcs, bufs, sems, n_pages, page):
    b = pl.program_id(0)
    nb = pl.num_programs(0)
    slot = lax.rem(b, 2)

    def start_all(copies):
        for n, c in enumerate(copies):
            c.start(priority=n % 2)

    @pl.when(b == 0)
    def _():
        start_all(_page_copies(pt_ref, b, slot, srcs, bufs, sems, n_pages, page))

    @pl.when(b + 1 < nb)
    def _():
        start_all(_page_copies(pt_ref, b + 1, 1 - slot, srcs, bufs, sems, n_pages, page))

    for c in _page_copies(pt_ref, b, slot, srcs, bufs, sems, n_pages, page):
        c.wait()
    return slot


def _append_new_row(buf, slot, past, new):
    first = lax.broadcasted_iota(jnp.int32, (SUBLANES, HEAD_DIM), 0) == 0
    for j in range(new.shape[0]):
        buf[slot, j, past:past + SUBLANES, :] = jnp.where(first, new[j:j + 1, :], 0.0)


def _tail_bias(heads):
    first = lax.broadcasted_iota(jnp.int32, (SUBLANES, heads), 0) == 0
    return jnp.where(first, 0.0, NEG_INF)


def _group_queries(q, j, group):
    r = lax.broadcasted_iota(jnp.int32, q.shape, 0)
    return jnp.where(r // group == j, q, jnp.zeros_like(q))


def _decode_scores(kb, q, n_kv):
    group = q.shape[0] // n_kv
    s = None
    for j0 in range(0, n_kv, 2):
        js = range(j0, min(j0 + 2, n_kv))
        k2 = jnp.concatenate([kb[j].astype(BF16) for j in js], axis=1)
        q2 = jnp.concatenate([_group_queries(q, j, group) for j in js], axis=1)
        sj = lax.dot_general(k2, q2, NT_DIMS, preferred_element_type=F32)
        s = sj if s is None else s + sj
    return s


def _decode_softmax_pv(s, vb, n_kv):
    heads = s.shape[1]
    group = heads // n_kv
    m = jnp.max(s, axis=0, keepdims=True)
    p = jnp.exp2(s - m)
    l = jnp.sum(p, axis=0, keepdims=True)
    p16 = p.astype(BF16)
    o_t = jnp.concatenate(
        [lax.dot_general(vb[j].astype(BF16), p16, TN_DIMS, preferred_element_type=F32) for j in range(n_kv)],
        axis=0) / l
    o = o_t.T
    return jnp.concatenate(
        [o[h:h + 1, (h // group) * HEAD_DIM:(h // group + 1) * HEAD_DIM] for h in range(heads)], axis=0)


def _moba_decode_kernel(pt_ref, q_ref, knew_ref, vnew_ref, alibi_ref, k_hbm, v_hbm, o_ref, kbuf, vbuf, sems,
                        *, layer, n_kv, n_pages, page):
    slot = _fetch_rows(pt_ref, ((k_hbm, (layer,)), (v_hbm, (layer,))), (kbuf, vbuf), sems, n_pages, page)
    past = n_pages * page
    nblk = past // MOBA_BLOCK
    _append_new_row(kbuf, slot, past, knew_ref[0])
    _append_new_row(vbuf, slot, past, vnew_ref[0])
    kb, vb = kbuf.at[slot], vbuf.at[slot]
    q = q_ref[0]
    heads = q.shape[0]
    group = heads // n_kv

    g = None
    for j in range(n_kv):
        means = jnp.concatenate(
            [jnp.sum(kb[j, n * MOBA_BLOCK:(n + 1) * MOBA_BLOCK, :], axis=0, keepdims=True) for n in range(nblk)],
            axis=0) * (1.0 / MOBA_BLOCK)
        gj = lax.dot_general(means, _group_queries(q, j, group).astype(F32), NT_DIMS, precision=HI,
                             preferred_element_type=F32)
        g = gj if g is None else g + gj
    blk = lax.broadcasted_iota(jnp.int32, g.shape, 0)
    cnt = jnp.zeros(g.shape, jnp.int32)
    for m in range(nblk):
        gm = g[m:m + 1, :]
        cnt = cnt + ((gm > g) | ((gm == g) & (blk > m))).astype(jnp.int32)
    sel = cnt < MOBA_TOPK
    keep = jnp.concatenate([jnp.broadcast_to(sel[n:n + 1, :], (MOBA_BLOCK, heads)) for n in range(nblk)]
                           + [jnp.ones((SUBLANES, heads), jnp.bool_)], axis=0)
    s = jnp.where(keep, _decode_scores(kb, q, n_kv) + alibi_ref[...], NEG_INF)
    o_ref[0] = _decode_softmax_pv(s, vb, n_kv).astype(o_ref.dtype)


def _fox_decode_kernel(pt_ref, q_ref, knew_ref, vnew_ref, fnew_ref, k_hbm, v_hbm, c_hbm, o_ref, kbuf, vbuf, cbuf, sems,
                       *, layer, n_kv, n_pages, page):
    slot = _fetch_rows(pt_ref, ((k_hbm, (layer,)), (v_hbm, (layer,)), (c_hbm, ())), (kbuf, vbuf, cbuf), sems,
                       n_pages, page)
    past = n_pages * page
    _append_new_row(kbuf, slot, past, knew_ref[0])
    _append_new_row(vbuf, slot, past, vnew_ref[0])
    kb, vb = kbuf.at[slot], vbuf.at[slot]
    q = q_ref[0]
    heads = q.shape[0]

    carry = jnp.zeros((1, heads), F32)
    cums = []
    for j in range(n_pages):
        cs = cbuf[slot, j * page:(j + 1) * page, :] + carry
        cums.append(cs)
        carry = cs[page - 1:page, :]
    c_now = carry + fnew_ref[0]
    bias = [(c_now - cs) * LOG2E for cs in cums]
    s = _decode_scores(kb, q, n_kv) + jnp.concatenate(bias + [_tail_bias(heads)], axis=0)
    o_ref[0] = _decode_softmax_pv(s, vb, n_kv).astype(o_ref.dtype)


def _decode_attention(body, layer, page_table, q16, k_new, v_new, extra, extra_per_row, caches, n_kv):
    rows, n_pages = page_table.shape
    heads = q16.shape[1] // HEAD_DIM
    page = caches[0].shape[2]
    past = n_pages * page
    assert past % MOBA_BLOCK == 0
    row_spec = lambda shape: pl.BlockSpec((1,) + shape, lambda b, pt: (b, 0, 0))
    in_specs = [row_spec((heads, HEAD_DIM)), row_spec((n_kv, HEAD_DIM)), row_spec((n_kv, HEAD_DIM))]
    if extra_per_row:
        in_specs.append(row_spec((1, heads)))
        extra = extra.reshape(rows, 1, heads)
    else:
        in_specs.append(pl.BlockSpec(extra.shape, lambda b, pt: (0, 0)))
    in_specs += [pl.BlockSpec(memory_space=pl.ANY)] * len(caches)
    scratch = [pltpu.VMEM((2, n_kv, past + SUBLANES, HEAD_DIM) if c.ndim == 5 else (2, past, c.shape[2]), F32)
               for c in caches]
    scratch.append(pltpu.SemaphoreType.DMA((len(caches), 2)))
    return pl.pallas_call(
        functools.partial(body, layer=layer, n_kv=n_kv, n_pages=n_pages, page=page),
        grid_spec=pltpu.PrefetchScalarGridSpec(
            num_scalar_prefetch=1,
            grid=(rows,),
            in_specs=in_specs,
            out_specs=pl.BlockSpec((1, heads, HEAD_DIM), lambda b, pt: (b, 0, 0)),
            scratch_shapes=scratch,
        ),
        out_shape=jax.ShapeDtypeStruct((rows, heads, HEAD_DIM), F32),
        compiler_params=_params("arbitrary"),
        name="decode_attention",
    )(page_table.reshape(-1), q16.reshape(rows, heads, HEAD_DIM), k_new, v_new, extra, *caches
      ).reshape(rows, heads * HEAD_DIM)


def _page_cumsum_kernel(x_ref, o_ref):
    page = x_ref.shape[1]
    r = lax.broadcasted_iota(jnp.int32, (page, page), 0)
    c = lax.broadcasted_iota(jnp.int32, (page, page), 1)
    o_ref[...] = jnp.dot(x_ref[...], (r <= c).astype(F32), precision=HI, preferred_element_type=F32)


def _page_cumsum(logf_cache):
    pool, page, heads = logf_cache.shape
    x = logf_cache.transpose(0, 2, 1).reshape(pool * heads, page)
    tm = _row_tile(pool * heads, 2048)
    out = pl.pallas_call(
        _page_cumsum_kernel,
        grid=(pool * heads // tm,),
        in_specs=[pl.BlockSpec((tm, page), lambda i: (i, 0))],
        out_specs=pl.BlockSpec((tm, page), lambda i: (i, 0)),
        out_shape=jax.ShapeDtypeStruct((pool * heads, page), F32),
        compiler_params=_params("parallel"),
        name="page_cumsum",
    )(x)
    return out.reshape(pool, heads, page).transpose(0, 2, 1)


def _merge_kernel(x_ref, oa_ref, ob_ref, z_ref, g_ref, wa_ref, wb_ref, wo_ref, y_ref):
    wa_w = oa_ref.shape[1]
    d = x_ref.shape[1]

    def branch(o_ref, z, w_ref):
        u = o_ref[...] * (z * jax.nn.sigmoid(z))
        return jnp.dot(u.astype(BF16), w_ref[...], preferred_element_type=F32)

    ya = branch(oa_ref, z_ref[:, :wa_w], wa_ref)
    yb = branch(ob_ref, z_ref[:, wa_w:], wb_ref)
    mixed = g_ref[:, :d] * ya + g_ref[:, d:] * yb
    y_ref[...] = x_ref[...] + jnp.dot(mixed.astype(BF16), wo_ref[...], preferred_element_type=F32)


def _merge(x, oa, ob, z, gates, wa, wb, wo):
    m, d = x.shape
    tm = _row_tile(m, 256)
    rows = lambda width: pl.BlockSpec((tm, width), lambda i: (i, 0))
    resident = lambda w: pl.BlockSpec(w.shape, lambda i: (0, 0), pipeline_mode=pl.Buffered(1))
    return pl.pallas_call(
        _merge_kernel,
        grid=(m // tm,),
        in_specs=[rows(d), rows(oa.shape[1]), rows(ob.shape[1]), rows(z.shape[1]), rows(gates.shape[1]),
                  resident(wa), resident(wb), resident(wo)],
        out_specs=rows(d),
        out_shape=jax.ShapeDtypeStruct((m, d), F32),
        compiler_params=_params("parallel"),
        name="merge",
    )(x, oa, ob, z, gates, wa, wb, wo)


def _project(x2d, g_norm, wt, wts, dims, w16s=None):
    d, wa_w, wb_w, kva, kvb, hb = dims
    n_kv_a, n_kv_b = kva // HEAD_DIM, kvb // HEAD_DIM
    tile = wa_w
    assert wa_w == wb_w == 2 * kva == 2 * kvb
    branch = 2 * wa_w + 2 * kva
    f_row = 2 * branch
    emitted = {}

    def call(name, body, x, rows, tn, extras, outs, **kw):
        res = _proj(body, x, wt, rows, tn, extras, outs, w16=None if w16s is None else w16s[name], **kw)
        if w16s is None:
            *res, emitted[name] = res
        return res

    h, ka, va, ka16, va16 = call("kva", functools.partial(_proj_kv_kernel, n_kv=n_kv_a, n_forget=0), x2d, (wa_w, 0, 1),
                                 tile, [(wts["g_ka"], tile, 0)],
                                 [(F32, (n_kv_a, HEAD_DIM)), (F32, (n_kv_a, HEAD_DIM)), (BF16, (kva,)),
                                  (BF16, (kva,))], norm_gain=g_norm.reshape(1, d))
    (q16,) = call("q", _proj_q_kernel, h, (0, branch, 2), tile, [(wts["g_q"], tile, 0)], [(BF16, (wa_w + wb_w,))])
    kb, vb, kb16, vb16, logf = call("kvb", functools.partial(_proj_kv_kernel, n_kv=n_kv_b, n_forget=hb), h,
                                    (branch + wb_w, 0, 1), tile, [(wts["g_kb"], tile, 0), (wts["b_f"], LANES, 0)],
                                    [(F32, (n_kv_b, HEAD_DIM)), (F32, (n_kv_b, HEAD_DIM)), (BF16, (kvb,)),
                                     (BF16, (kvb,)), (F32, (hb,))], forget_row=f_row)
    (z,) = call("z", _proj_plain_kernel, h, (wa_w + 2 * kva, branch, 2), tile, [], [(F32, (wa_w + wb_w,))])
    gate_tile = min(tile, 2 * d)
    (gates,) = call("gates", _proj_gate_kernel, h, (f_row + hb, gate_tile, 2 * d // gate_tile), gate_tile,
                    [(wts["b_g"], gate_tile, 0)], [(F32, (2 * d,))])
    return dict(q16=q16, ka=ka, va=va, ka16=ka16, va16=va16, kb=kb, vb=vb, kb16=kb16, vb16=vb16, logf=logf, z=z,
                gates=gates), (emitted if w16s is None else w16s)


def _layer_weights(b_forget, b_merge, gq_a, gk_a, gq_b, gk_b, wa, wb, wo, dims):
    d, wa_w, wb_w, kva, kvb, hb = dims
    return dict(
        g_q=jnp.concatenate([jnp.tile(gq_a, wa_w // HEAD_DIM), jnp.tile(gq_b, wb_w // HEAD_DIM)]).reshape(1, -1),
        g_ka=jnp.concatenate([jnp.tile(gk_a, kva // HEAD_DIM), jnp.zeros((kva,), F32)]).reshape(1, -1),
        g_kb=jnp.concatenate([jnp.tile(gk_b, kvb // HEAD_DIM), jnp.zeros((kvb,), F32)]).reshape(1, -1),
        b_f=jnp.concatenate([b_forget, jnp.zeros((LANES - hb,), F32)]).reshape(1, -1),
        b_g=b_merge.reshape(1, 2 * d),
        wa=wa.astype(BF16), wb=wb.astype(BF16), wo=wo.astype(BF16),
    )


def kernel(x_prompt, x_sample, cache_k_moba, cache_v_moba, cache_k_fox, cache_v_fox, cache_logf_fox, page_table,
           g_norm, w_in, b_forget, b_merge, gq_moba, gk_moba, gq_fox, gk_fox, w_branch_moba, w_branch_fox, w_out):
    depth = w_in.shape[0]
    batch, seq, d = x_prompt.shape
    rows, dec_seq, _ = x_sample.shape
    assert dec_seq == 1 and seq % MOBA_BLOCK == 0
    n_kv_a, n_kv_b = cache_k_moba.shape[3], cache_k_fox.shape[3]
    hb = b_forget.shape[1]
    wa_w, wb_w = w_branch_moba.shape[1], w_branch_fox.shape[1]
    ha = wa_w // HEAD_DIM
    assert wb_w == hb * HEAD_DIM
    kva, kvb = n_kv_a * HEAD_DIM, n_kv_b * HEAD_DIM
    slopes = 2.0 ** (-8.0 * jnp.arange(1, ha + 1, dtype=F32) / ha)
    past = page_table.shape[1] * cache_k_moba.shape[2]
    pos = jnp.arange(past + SUBLANES)[:, None]
    alibi_decode = jnp.where(pos < past, (-LOG2E) * slopes[None, :] * (past - pos).astype(F32),
                             jnp.where(pos == past, 0.0, NEG_INF))

    xp = x_prompt.reshape(batch * seq, d)
    xs = x_sample.reshape(rows, d)
    outs_p, outs_s = [], []
    for l in range(depth):
        dims = (d, wa_w, wb_w, kva, kvb, hb)
        wt = w_in[l].T
        wts = _layer_weights(b_forget[l], b_merge[l], gq_moba[l], gk_moba[l], gq_fox[l], gk_fox[l],
                             w_branch_moba[l], w_branch_fox[l], w_out[l], dims)
        pp, w16s = _project(xp, g_norm[l], wt, wts, dims)
        kmean = _kmean(pp["ka"], batch, seq)
        oa = _moba_prompt(pp["q16"], 0, pp["ka16"], pp["va16"], kmean, slopes, batch, seq, ha, n_kv_a)
        group_b = hb // n_kv_b
        c_row = _cumsum_rows(pp["logf"].reshape(batch, seq, hb).transpose(0, 2, 1))
        c_col = c_row.reshape(batch, n_kv_b, group_b, seq).transpose(1, 0, 3, 2).reshape(n_kv_b, batch * seq, group_b)
        ob = _fox_prompt(pp["q16"], wa_w, pp["kb16"], pp["vb16"], c_col,
                         c_row.reshape(batch, n_kv_b, group_b, seq), batch, seq, hb, n_kv_b)
        xp = _merge(xp, oa, ob, pp["z"], pp["gates"], wts["wa"], wts["wb"], wts["wo"])
        outs_p.append((pp["ka"], pp["va"], pp["kb"], pp["vb"], pp["logf"]))
        ps, _ = _project(xs, g_norm[l], wt, wts, dims, w16s)
        oa_s = _decode_attention(_moba_decode_kernel, l, page_table, ps["q16"][:, :wa_w], ps["ka"], ps["va"],
                                 alibi_decode, False, [cache_k_moba, cache_v_moba], n_kv_a)
        ob_s = _decode_attention(_fox_decode_kernel, l, page_table, ps["q16"][:, wa_w:], ps["kb"], ps["vb"],
                                 ps["logf"], True, [cache_k_fox, cache_v_fox, _page_cumsum(cache_logf_fox[l])], n_kv_b)
        xs = _merge(xs, oa_s, ob_s, ps["z"], ps["gates"], wts["wa"], wts["wb"], wts["wo"])
        outs_s.append((ps["ka"], ps["va"], ps["kb"], ps["vb"], ps["logf"]))

    def stacked(outs, i, shape):
        return jnp.stack([o[i].reshape(shape) for o in outs])

    res = [xp.reshape(batch, seq, d), xs.reshape(rows, 1, d)]
    for outs, lead in ((outs_p, (batch, seq)), (outs_s, (rows, 1))):
        res += [stacked(outs, 0, lead + (n_kv_a, HEAD_DIM)), stacked(outs, 1, lead + (n_kv_a, HEAD_DIM)),
                stacked(outs, 2, lead + (n_kv_b, HEAD_DIM)), stacked(outs, 3, lead + (n_kv_b, HEAD_DIM)),
                stacked(outs, 4, lead + (hb,))]
    return tuple(res)
```

```python
import functools

import jax
import jax.numpy as jnp
from jax import lax
from jax.experimental import pallas as pl
from jax.experimental.pallas import tpu as pltpu

HEAD_DIM = 128
MOBA_BLOCK = 256
MOBA_TOPK = 3
NORM_EPS = 1e-6
NEG_INF = -1e30
M_INIT = 0.5 * NEG_INF
SCALE = HEAD_DIM ** -0.5
LOG2E = 1.4426950408889634
LANES = 128
SUBLANES = 8
VMEM_LIMIT = 52 * 1024 * 1024
ATTN_TILE = 512

F32 = jnp.float32
BF16 = jnp.bfloat16
HI = lax.Precision.HIGHEST
NT_DIMS = (((1,), (1,)), ((), ()))
TN_DIMS = (((0,), (0,)), ((), ()))


def _params(*sem):
    return pltpu.CompilerParams(dimension_semantics=sem, vmem_limit_bytes=VMEM_LIMIT)


def _row_tile(m, want):
    t = min(m, want)
    assert m % t == 0
    return t


def _head_norm(a, g):
    ms = jnp.mean(a * a, axis=-1, keepdims=True)
    return a * lax.rsqrt(ms + NORM_EPS) * g


def _log_sigmoid(x):
    return jnp.minimum(x, 0.0) - jnp.log1p(jnp.exp(-jnp.abs(x)))


def _proj_q_kernel(h_ref, w_ref, g_ref, o_ref):
    acc = lax.dot_general(h_ref[...], w_ref[...], NT_DIMS, preferred_element_type=F32)
    for j in range(acc.shape[1] // HEAD_DIM):
        sl = slice(j * HEAD_DIM, (j + 1) * HEAD_DIM)
        o_ref[:, sl] = (_head_norm(acc[:, sl], g_ref[:, sl]) * (SCALE * LOG2E)).astype(o_ref.dtype)


def _proj_kv_kernel(h_ref, w_ref, g_ref, *refs, n_kv, n_forget):
    if n_forget:
        wf_ref, bf_ref, *refs = refs
    k32_hbm, v32_hbm, k16_ref, v16_ref, *refs = refs
    if n_forget:
        logf_ref, *refs = refs
    kbuf, vbuf, sems = refs
    kvw = n_kv * HEAD_DIM
    tm = h_ref.shape[0]
    i = pl.program_id(1)
    last = pl.num_programs(1) - 1
    slot = lax.rem(i, 2)

    def writebacks(step, slot):
        rows = pl.ds(pl.multiple_of(step * tm, tm), tm)
        return [pltpu.make_async_copy(buf.at[slot, :, j * HEAD_DIM:(j + 1) * HEAD_DIM], dst.at[rows, j, :],
                                      sems.at[a, slot])
                for a, (buf, dst) in enumerate(((kbuf, k32_hbm), (vbuf, v32_hbm))) for j in range(n_kv)]

    @pl.when(i >= 2)
    def _():
        for c in writebacks(i - 2, slot):
            c.wait()

    h = h_ref[...]
    acc = lax.dot_general(h, w_ref[...], NT_DIMS, preferred_element_type=F32)
    for j in range(n_kv):
        sl = slice(j * HEAD_DIM, (j + 1) * HEAD_DIM)
        kn = _head_norm(acc[:, sl], g_ref[:, sl])
        kbuf[slot, :, sl] = kn
        k16_ref[:, sl] = kn.astype(BF16)
        v = acc[:, kvw + j * HEAD_DIM:kvw + (j + 1) * HEAD_DIM]
        vbuf[slot, :, sl] = v
        v16_ref[:, sl] = v.astype(BF16)
    if n_forget:
        f = lax.dot_general(h, wf_ref[...].astype(BF16), NT_DIMS, preferred_element_type=F32)
        logf_ref[...] = _log_sigmoid(f[:, :n_forget] + bf_ref[:, :n_forget])
    for c in writebacks(i, slot):
        c.start()

    @pl.when((i == last) & (i >= 1))
    def _():
        for c in writebacks(i - 1, 1 - slot):
            c.wait()

    @pl.when(i == last)
    def _():
        for c in writebacks(i, slot):
            c.wait()


def _proj_plain_kernel(h_ref, w_ref, o_ref):
    o_ref[...] = lax.dot_general(h_ref[...], w_ref[...], NT_DIMS, preferred_element_type=F32).astype(o_ref.dtype)


def _proj_gate_kernel(h_ref, w_ref, b_ref, o_ref):
    acc = lax.dot_general(h_ref[...], w_ref[...], NT_DIMS, preferred_element_type=F32)
    o_ref[...] = jax.nn.sigmoid(acc + b_ref[...]).astype(o_ref.dtype)


def _with_cached_weights(body, h_ref, w_ref, *refs):
    *refs, w16_sc = refs

    @pl.when(pl.program_id(1) == 0)
    def _():
        w16_sc[...] = w_ref[...].astype(BF16)

    body(h_ref, w16_sc, *refs)


def _with_rmsnorm(body, n_inputs, x_ref, w_ref, gn_ref, *refs):
    ins, (h_ref, *rest) = refs[:n_inputs], refs[n_inputs:]
    x = x_ref[...]
    ms = jnp.mean(x * x, axis=-1, keepdims=True)
    h_ref[...] = (x * lax.rsqrt(ms + NORM_EPS) * gn_ref[...]).astype(h_ref.dtype)
    body(h_ref, w_ref, *ins, *rest)


def _weight_rows(wt, n_rows, first, stride=0):
    return pl.BlockSpec((pl.Element(n_rows), pl.Element(wt.shape[1])),
                        lambda j, i: (pl.multiple_of(first + stride * j, SUBLANES), 0))


def _proj(body, h, wt, rows, tn, extras, outs, forget_row=None, norm_gain=None):
    m, d = h.shape
    first, stride, count = rows
    assert first % SUBLANES == 0 and stride % SUBLANES == 0
    tm = _row_tile(m, 1024 if norm_gain is None else 512)
    in_specs = [pl.BlockSpec((tm, d), lambda j, i: (i, 0)), _weight_rows(wt, tn, first, stride)]
    args = [h, wt]
    if norm_gain is not None:
        assert count == 1
        in_specs.append(pl.BlockSpec((1, d), lambda j, i: (0, 0)))
        args.append(norm_gain)
        outs = [(BF16, (d,))] + list(outs)
        body = functools.partial(_with_rmsnorm, body, len(extras) + (forget_row is not None))
    for arr, width, base in extras[:1]:
        in_specs.append(pl.BlockSpec((arr.shape[0], width), lambda j, i, base=base: (0, base + j)))
        args.append(arr)
    if forget_row is not None:
        in_specs.append(_weight_rows(wt, LANES, forget_row))
        args.append(wt)
    for arr, width, base in extras[1:]:
        in_specs.append(pl.BlockSpec((arr.shape[0], width), lambda j, i, base=base: (0, base + j)))
        args.append(arr)
    out_specs, out_shape, staging = [], [], []
    for dt, trail in outs:
        if len(trail) == 2:
            assert count == 1
            out_specs.append(pl.BlockSpec(memory_space=pl.ANY))
            staging.append(pltpu.VMEM((2, tm, trail[0] * trail[1]), dt))
        else:
            out_specs.append(pl.BlockSpec((tm, trail[0] // count), lambda j, i: (i, j)))
        out_shape.append(jax.ShapeDtypeStruct((m,) + trail, dt))
    if staging:
        staging.append(pltpu.SemaphoreType.DMA((len(staging), 2)))
    return pl.pallas_call(
        functools.partial(_with_cached_weights, body),
        grid=(count, m // tm),
        in_specs=in_specs,
        out_specs=out_specs,
        out_shape=out_shape,
        scratch_shapes=staging + [pltpu.VMEM((tn, d), BF16)],
        compiler_params=_params("parallel", "arbitrary"),
        name="proj",
    )(*args)


def _kmean_kernel(k_ref, o_ref, *, nblk, n_kv):
    o_ref[...] = jnp.zeros_like(o_ref)
    for j in range(n_kv):
        for n in range(nblk):
            blk = k_ref[pl.ds(n * MOBA_BLOCK * n_kv + j, MOBA_BLOCK, stride=n_kv), :]
            o_ref[0, j, n:n + 1, :] = jnp.sum(blk, axis=0, keepdims=True) * (1.0 / MOBA_BLOCK)


def _kmean(k32, batch, seq):
    n_kv = k32.shape[1]
    nblk = seq // MOBA_BLOCK
    assert nblk <= LANES
    k32 = k32.reshape(batch * seq * n_kv, HEAD_DIM)
    return pl.pallas_call(
        functools.partial(_kmean_kernel, nblk=nblk, n_kv=n_kv),
        grid=(batch,),
        in_specs=[pl.BlockSpec((seq * n_kv, HEAD_DIM), lambda b: (b, 0))],
        out_specs=pl.BlockSpec((1, n_kv, LANES, HEAD_DIM), lambda b: (b, 0, 0, 0)),
        out_shape=jax.ShapeDtypeStruct((batch, n_kv, LANES, HEAD_DIM), F32),
        compiler_params=_params("parallel"),
        name="kmean",
    )(k32)


def _split3(x):
    hi = x.astype(BF16).astype(F32)
    r = x - hi
    mid = r.astype(BF16).astype(F32)
    return hi, mid, (r - mid).astype(BF16).astype(F32)


N_SPLIT = 3


def _interleave_terms(terms, lane):
    hi, mid, lo = terms
    return jnp.where(lane % N_SPLIT == 0, hi, jnp.where(lane % N_SPLIT == 1, mid, lo))


def _store_wide_queries(qw_sc, q_ref, coef_rows):
    tq = q_ref.shape[0]
    for h, coef in enumerate(coef_rows):
        qw_sc[h * tq:(h + 1) * tq, :HEAD_DIM] = q_ref[:, h * HEAD_DIM:(h + 1) * HEAD_DIM]
        qw_sc[h * tq:(h + 1) * tq, HEAD_DIM:] = jnp.broadcast_to(coef.astype(BF16), (tq, LANES))


def _qk_logits_into(buf, block, k, kx, qw_sc):
    s_ref, mx_ref = buf
    s = lax.dot_general(jnp.concatenate([k, kx], axis=1), qw_sc[...], NT_DIMS, preferred_element_type=F32)
    s_ref[...] = s
    for c in range(s.shape[0] // block):
        mx_ref[c:c + 1, :] = jnp.max(s[c * block:(c + 1) * block], axis=0, keepdims=True)


def _pipelined_tiles(qi, s_refs, logits_into, process):
    s0, s1 = s_refs
    logits_into(s0, 0)

    def pair(kp, carry):
        t = 2 * kp
        logits_into(s1, t + 1)
        process(s0, t, False)
        logits_into(s0, t + 2)
        process(s1, t + 1, False)
        return carry

    lax.fori_loop(0, qi // 2, pair, 0)

    @pl.when(qi % 2 == 0)
    def _():
        process(s0, qi, True)

    @pl.when(qi % 2 == 1)
    def _():
        logits_into(s1, qi)
        process(s0, qi - 1, False)
        process(s1, qi, True)


def _softmax_tile_t(head_parts, v, m_sc, l_sc, acc_sc):
    heads = range(len(head_parts))
    tq = head_parts[0][0][0].shape[1]
    m_prev = [m_sc[h] for h in heads]
    m_new = []
    for h in heads:
        m = m_prev[h]
        for u, rb, umax in head_parts[h]:
            m = jnp.maximum(m, (jnp.max(u, axis=0, keepdims=True) if umax is None else umax) + rb)
        m_new.append(m)
    alphas, p16s = [], []
    for h in heads:
        alpha = jnp.exp2(m_prev[h] - m_new[h])
        l_new = alpha * l_sc[h]
        ps = []
        for u, rb, _ in head_parts[h]:
            p = jnp.exp2(u - (m_new[h] - rb))
            l_new = l_new + jnp.sum(p, axis=0, keepdims=True)
            ps.append(p.astype(v.dtype))
        p16s.append(ps[0] if len(ps) == 1 else jnp.concatenate(ps, axis=0))
        alphas.append(alpha)
        l_sc[h] = l_new
        m_sc[h] = m_new[h]
    pv = lax.dot_general(v, jnp.concatenate(p16s, axis=1), TN_DIMS, preferred_element_type=F32)
    for h in heads:
        acc_sc[h] = alphas[h] * acc_sc[h] + pv[:, h * tq:(h + 1) * tq]


def _init_softmax_state(m_sc, l_sc, acc_sc):
    m_sc[...] = jnp.full(m_sc.shape, M_INIT, F32)
    l_sc[...] = jnp.zeros(l_sc.shape, F32)
    acc_sc[...] = jnp.zeros(acc_sc.shape, F32)


def _write_heads(o_ref, l_sc, acc_sc, group):
    for h in range(group):
        o_ref[:, h * HEAD_DIM:(h + 1) * HEAD_DIM] = (acc_sc[h] / l_sc[h]).T.astype(o_ref.dtype)


def _moba_prompt_kernel(slopes_ref, q_ref, k_ref, v_ref, kmean_ref, o_ref, m_sc, l_sc, acc_sc, selb_sc, qw_sc, kx_sc,
                        s0_sc, s1_sc, mx0_sc, mx1_sc, *, group, nblk):
    tq = q_ref.shape[0]
    per_tile = tq // MOBA_BLOCK
    kvh = pl.program_id(1)
    qi = pl.program_id(2)
    _init_softmax_state(m_sc, l_sc, acc_sc)

    qcol = lax.broadcasted_iota(jnp.int32, (1, tq), 1)
    own = qi * per_tile + qcol // MOBA_BLOCK
    blk = lax.broadcasted_iota(jnp.int32, (nblk, tq), 0)
    km = kmean_ref[0, 0, :-(-nblk // SUBLANES) * SUBLANES, :]
    for h in range(group):
        qh = q_ref[:, h * HEAD_DIM:(h + 1) * HEAD_DIM].astype(F32)
        g = lax.dot_general(km, qh, NT_DIMS, precision=HI, preferred_element_type=F32)[:nblk]
        cnt = jnp.zeros((nblk, tq), jnp.int32)
        for m in range(nblk):
            gm = g[m:m + 1, :]
            beats = (gm > g) | ((gm == g) & (blk > m))
            cnt = cnt + (beats & (own > m)).astype(jnp.int32)
        sel = (blk < own) & (cnt < MOBA_TOPK)
        selb_sc[h] = jnp.where(sel, 0.0, NEG_INF)

    slope2 = [slopes_ref[kvh * group + h] * LOG2E for h in range(group)]
    lane = lax.broadcasted_iota(jnp.int32, (1, LANES), 1)
    coefs = []
    for h in range(group):
        terms = _split3(jnp.zeros((1, LANES), F32) + slope2[h])
        coefs.append(jnp.where(lane < 2 * N_SPLIT, _interleave_terms(terms, lane), 0.0))
    _store_wide_queries(qw_sc, q_ref, coefs)
    koff = lax.broadcasted_iota(jnp.int32, (tq, LANES), 0)
    klane = lax.broadcasted_iota(jnp.int32, (tq, LANES), 1)
    kx_sc[...] = jnp.where(klane < N_SPLIT, (koff // MOBA_BLOCK) * MOBA_BLOCK,
                           jnp.where(klane < 2 * N_SPLIT, koff % MOBA_BLOCK, 0)).astype(F32).astype(BF16)

    def logits_into(buf, ki):
        _qk_logits_into(buf, MOBA_BLOCK, k_ref[pl.ds(pl.multiple_of(ki * tq, tq), tq), :], kx_sc[...], qw_sc)

    def process(buf, ki, diagonal):
        s_ref, mx_ref = buf
        parts = []
        for h in range(group):
            query_bias = -slope2[h] * (((qi - ki) * tq).astype(F32) + qcol.astype(F32))
            parts.append([])
            for c in range(per_tile):
                kb = ki * per_tile + c
                u = s_ref[c * MOBA_BLOCK:(c + 1) * MOBA_BLOCK, h * tq:(h + 1) * tq]
                selrow = selb_sc[h, pl.ds(kb, 1), :]
                umax = mx_ref[c:c + 1, h * tq:(h + 1) * tq]
                if diagonal:
                    rel = (lax.broadcasted_iota(jnp.int32, (MOBA_BLOCK, tq), 1)
                           - lax.broadcasted_iota(jnp.int32, (MOBA_BLOCK, tq), 0))
                    u = jnp.where(rel >= c * MOBA_BLOCK, u, NEG_INF)
                    selrow = jnp.where(own > kb, selrow, 0.0)
                    umax = None
                parts[h].append((u, selrow + query_bias, umax))
        _softmax_tile_t(parts, v_ref[pl.ds(pl.multiple_of(ki * tq, tq), tq), :], m_sc, l_sc, acc_sc)

    _pipelined_tiles(qi, ((s0_sc, mx0_sc), (s1_sc, mx1_sc)), logits_into, process)
    _write_heads(o_ref, l_sc, acc_sc, group)


def _moba_prompt(q16, q_col0, k16, v16, kmean, slopes, batch, seq, n_heads, n_kv):
    group = n_heads // n_kv
    tq = _row_tile(seq, ATTN_TILE)
    assert tq % MOBA_BLOCK == 0
    nq = seq // tq
    nblk = seq // MOBA_BLOCK
    gw = group * HEAD_DIM
    qb0 = q_col0 // gw
    return pl.pallas_call(
        functools.partial(_moba_prompt_kernel, group=group, nblk=nblk),
        grid=(batch, n_kv, nq),
        in_specs=[
            pl.BlockSpec(memory_space=pltpu.SMEM),
            pl.BlockSpec((tq, gw), lambda b, j, i: (b * nq + i, qb0 + j)),
            pl.BlockSpec((seq, HEAD_DIM), lambda b, j, i: (b, j)),
            pl.BlockSpec((seq, HEAD_DIM), lambda b, j, i: (b, j)),
            pl.BlockSpec((1, 1, LANES, HEAD_DIM), lambda b, j, i: (b, j, 0, 0)),
        ],
        out_specs=pl.BlockSpec((tq, gw), lambda b, j, i: (b * nq + i, j)),
        out_shape=jax.ShapeDtypeStruct((batch * seq, n_heads * HEAD_DIM), F32),
        scratch_shapes=[
            pltpu.VMEM((group, 1, tq), F32),
            pltpu.VMEM((group, 1, tq), F32),
            pltpu.VMEM((group, HEAD_DIM, tq), F32),
            pltpu.VMEM((group, nblk, tq), F32),
            pltpu.VMEM((group * tq, 2 * HEAD_DIM), BF16),
            pltpu.VMEM((tq, LANES), BF16),
            pltpu.VMEM((tq, group * tq), F32),
            pltpu.VMEM((tq, group * tq), F32),
            pltpu.VMEM((SUBLANES, group * tq), F32),
            pltpu.VMEM((SUBLANES, group * tq), F32),
        ],
        compiler_params=_params("parallel", "parallel", "arbitrary"),
        name="moba_prompt",
    )(slopes, q16, k16, v16, kmean)


def _fox_prompt_kernel(q_ref, k_ref, v_ref, ccol_ref, crow_ref, o_ref, m_sc, l_sc, acc_sc, qw_sc, kx_sc, s0_sc, s1_sc,
                       mx0_sc, mx1_sc, *, group):
    tq = q_ref.shape[0]
    seq = k_ref.shape[0]
    qi = pl.program_id(2)
    _init_softmax_state(m_sc, l_sc, acc_sc)
    lane = lax.broadcasted_iota(jnp.int32, (1, LANES), 1)
    _store_wide_queries(qw_sc, q_ref, [jnp.where(lane // N_SPLIT == h, -1.0, 0.0) for h in range(group)])

    @pl.when(qi == 0)
    def _():
        klane = lax.broadcasted_iota(jnp.int32, (seq, LANES), 1)
        c2 = ccol_ref[0] * LOG2E
        x = jnp.zeros((seq, LANES), F32)
        for h in range(group):
            x = jnp.where(klane // N_SPLIT == h, c2[:, h:h + 1], x)
        kx_sc[...] = _interleave_terms(_split3(x), klane).astype(BF16)

    def logits_into(buf, ki):
        off = pl.multiple_of(ki * tq, tq)
        _qk_logits_into(buf, tq, k_ref[pl.ds(off, tq), :], kx_sc[pl.ds(off, tq), :], qw_sc)

    def cq(h):
        return crow_ref[0, 0, h:h + 1, pl.ds(pl.multiple_of(qi * tq, tq), tq)] * LOG2E

    def process(buf, ki, diagonal):
        s_ref, mx_ref = buf
        u = [s_ref[:, h * tq:(h + 1) * tq] for h in range(group)]
        umax = [mx_ref[0:1, h * tq:(h + 1) * tq] for h in range(group)]
        if diagonal:
            krow = lax.broadcasted_iota(jnp.int32, (tq, tq), 0)
            qcol = lax.broadcasted_iota(jnp.int32, (tq, tq), 1)
            u = [jnp.where(qcol >= krow, uh, NEG_INF) for uh in u]
            umax = [None] * group
        _softmax_tile_t([[(u[h], cq(h), umax[h])] for h in range(group)],
                        v_ref[pl.ds(pl.multiple_of(ki * tq, tq), tq), :], m_sc, l_sc, acc_sc)

    _pipelined_tiles(qi, ((s0_sc, mx0_sc), (s1_sc, mx1_sc)), logits_into, process)
    _write_heads(o_ref, l_sc, acc_sc, group)


def _fox_prompt(q16, q_col0, k16, v16, c_col, c_row, batch, seq, n_heads, n_kv):
    group = n_heads // n_kv
    tq = _row_tile(seq, ATTN_TILE)
    nq = seq // tq
    gw = group * HEAD_DIM
    qb0 = q_col0 // gw
    return pl.pallas_call(
        functools.partial(_fox_prompt_kernel, group=group),
        grid=(batch, n_kv, nq),
        in_specs=[
            pl.BlockSpec((tq, gw), lambda b, j, i: (b * nq + i, qb0 + j)),
            pl.BlockSpec((seq, HEAD_DIM), lambda b, j, i: (b, j)),
            pl.BlockSpec((seq, HEAD_DIM), lambda b, j, i: (b, j)),
            pl.BlockSpec((1, seq, group), lambda b, j, i: (j, b, 0)),
            pl.BlockSpec((1, 1, group, seq), lambda b, j, i: (b, j, 0, 0)),
        ],
        out_specs=pl.BlockSpec((tq, gw), lambda b, j, i: (b * nq + i, j)),
        out_shape=jax.ShapeDtypeStruct((batch * seq, n_heads * HEAD_DIM), F32),
        scratch_shapes=[
            pltpu.VMEM((group, 1, tq), F32),
            pltpu.VMEM((group, 1, tq), F32),
            pltpu.VMEM((group, HEAD_DIM, tq), F32),
            pltpu.VMEM((group * tq, 2 * HEAD_DIM), BF16),
            pltpu.VMEM((seq, LANES), BF16),
            pltpu.VMEM((tq, group * tq), F32),
            pltpu.VMEM((tq, group * tq), F32),
            pltpu.VMEM((SUBLANES, group * tq), F32),
            pltpu.VMEM((SUBLANES, group * tq), F32),
        ],
        compiler_params=_params("arbitrary", "arbitrary", "arbitrary"),
        name="fox_prompt",
    )(q16, k16, v16, c_col, c_row)


def _cumsum_kernel(x_ref, o_ref, *, chunk):
    rows, seq = x_ref.shape[1:]
    r = lax.broadcasted_iota(jnp.int32, (chunk, chunk), 0)
    c = lax.broadcasted_iota(jnp.int32, (chunk, chunk), 1)
    upper = (r <= c).astype(F32)
    carry = jnp.zeros((rows, 1), F32)
    for i in range(seq // chunk):
        sl = slice(i * chunk, (i + 1) * chunk)
        cs = jnp.dot(x_ref[0, :, sl], upper, precision=HI, preferred_element_type=F32) + carry
        o_ref[0, :, sl] = cs
        carry = cs[:, chunk - 1:chunk]


def _cumsum_rows(x):
    batch, rows, seq = x.shape
    chunk = 256
    assert seq % chunk == 0
    return pl.pallas_call(
        functools.partial(_cumsum_kernel, chunk=chunk),
        grid=(batch,),
        in_specs=[pl.BlockSpec((1, rows, seq), lambda b: (b, 0, 0))],
        out_specs=pl.BlockSpec((1, rows, seq), lambda b: (b, 0, 0)),
        out_shape=jax.ShapeDtypeStruct(x.shape, F32),
        compiler_params=_params("parallel"),
        name="cumsum",
    )(x)


def _page_copies(pt_ref, row, slot, srcs, bufs, sems, n_pages, page):
    copies = []
    for p in range(n_pages):
        idx = pt_ref[row * n_pages + p]
        rows = pl.ds(p * page, page)
        for a, ((src, lead), buf) in enumerate(zip(srcs, bufs)):
            src_page = src.at[lead + (idx,)]
            if len(buf.shape) == 4:
                for j in range(buf.shape[1]):
                    copies.append(pltpu.make_async_copy(src_page.at[:, j, :], buf.at[slot, j, rows, :], sems.at[a, slot]))
            else:
                copies.append(pltpu.make_async_copy(src_page, buf.at[slot, rows], sems.at[a, slot]))
    return copies


DECODE_BUFS = 3


def _fetch_rows(pt_ref, srcs, bufs, sems, n_pages, page):
    b = pl.program_id(0)
    nb = pl.num_programs(0)
    ahead = DECODE_BUFS - 1
    slot = lax.rem(b, DECODE_BUFS)

    def start_all(copies):
        for n, c in enumerate(copies):
            c.start(priority=n % 2)

    @pl.when(b == 0)
    def _():
        for r in range(ahead):
            start_all(_page_copies(pt_ref, r, r, srcs, bufs, sems, n_pages, page))

    @pl.when(b + ahead < nb)
    def _():
        start_all(_page_copies(pt_ref, b + ahead, lax.rem(b + ahead, DECODE_BUFS), srcs, bufs, sems, n_pages, page))

    for c in _page_copies(pt_ref, b, slot, srcs, bufs, sems, n_pages, page):
        c.wait()
    return slot


def _append_new_row(buf, slot, past, new):
    first = lax.broadcasted_iota(jnp.int32, (SUBLANES, HEAD_DIM), 0) == 0
    for j in range(new.shape[0]):
        buf[slot, j, past:past + SUBLANES, :] = jnp.where(first, new[j:j + 1, :], 0.0)


def _tail_bias(heads):
    first = lax.broadcasted_iota(jnp.int32, (SUBLANES, heads), 0) == 0
    return jnp.where(first, 0.0, NEG_INF)


def _group_queries(q, j, group):
    r = lax.broadcasted_iota(jnp.int32, q.shape, 0)
    return jnp.where(r // group == j, q, jnp.zeros_like(q))


def _decode_scores(kb, q, n_kv):
    group = q.shape[0] // n_kv
    s = None
    for j0 in range(0, n_kv, 2):
        js = range(j0, min(j0 + 2, n_kv))
        k2 = jnp.concatenate([kb[j].astype(BF16) for j in js], axis=1)
        q2 = jnp.concatenate([_group_queries(q, j, group) for j in js], axis=1)
        sj = lax.dot_general(k2, q2, NT_DIMS, preferred_element_type=F32)
        s = sj if s is None else s + sj
    return s


def _decode_softmax_pv(s, vb, n_kv):
    heads = s.shape[1]
    group = heads // n_kv
    m = jnp.max(s, axis=0, keepdims=True)
    p = jnp.exp2(s - m)
    l = jnp.sum(p, axis=0, keepdims=True)
    p16 = p.astype(BF16)
    o_t = jnp.concatenate(
        [lax.dot_general(vb[j].astype(BF16), p16, TN_DIMS, preferred_element_type=F32) for j in range(n_kv)],
        axis=0) / l
    o = o_t.T
    return jnp.concatenate(
        [o[h:h + 1, (h // group) * HEAD_DIM:(h // group + 1) * HEAD_DIM] for h in range(heads)], axis=0)


def _moba_decode_kernel(pt_ref, q_ref, knew_ref, vnew_ref, alibi_ref, k_hbm, v_hbm, o_ref, kbuf, vbuf, sems,
                        *, layer, n_kv, n_pages, page):
    slot = _fetch_rows(pt_ref, ((k_hbm, (layer,)), (v_hbm, (layer,))), (kbuf, vbuf), sems, n_pages, page)
    past = n_pages * page
    nblk = past // MOBA_BLOCK
    _append_new_row(kbuf, slot, past, knew_ref[0])
    _append_new_row(vbuf, slot, past, vnew_ref[0])
    kb, vb = kbuf.at[slot], vbuf.at[slot]
    q = q_ref[0]
    heads = q.shape[0]
    group = heads // n_kv

    g = None
    for j in range(n_kv):
        means = jnp.concatenate(
            [jnp.sum(kb[j, n * MOBA_BLOCK:(n + 1) * MOBA_BLOCK, :], axis=0, keepdims=True) for n in range(nblk)],
            axis=0) * (1.0 / MOBA_BLOCK)
        gj = lax.dot_general(means, _group_queries(q, j, group).astype(F32), NT_DIMS, precision=HI,
                             preferred_element_type=F32)
        g = gj if g is None else g + gj
    blk = lax.broadcasted_iota(jnp.int32, g.shape, 0)
    cnt = jnp.zeros(g.shape, jnp.int32)
    for m in range(nblk):
        gm = g[m:m + 1, :]
        cnt = cnt + ((gm > g) | ((gm == g) & (blk > m))).astype(jnp.int32)
    sel = cnt < MOBA_TOPK
    keep = jnp.concatenate([jnp.broadcast_to(sel[n:n + 1, :], (MOBA_BLOCK, heads)) for n in range(nblk)]
                           + [jnp.ones((SUBLANES, heads), jnp.bool_)], axis=0)
    s = jnp.where(keep, _decode_scores(kb, q, n_kv) + alibi_ref[...], NEG_INF)
    o_ref[0] = _decode_softmax_pv(s, vb, n_kv).astype(o_ref.dtype)


def _fox_decode_kernel(pt_ref, q_ref, knew_ref, vnew_ref, fnew_ref, k_hbm, v_hbm, c_hbm, o_ref, kbuf, vbuf, cbuf, sems,
                       *, layer, n_kv, n_pages, page):
    slot = _fetch_rows(pt_ref, ((k_hbm, (layer,)), (v_hbm, (layer,)), (c_hbm, ())), (kbuf, vbuf, cbuf), sems,
                       n_pages, page)
    past = n_pages * page
    _append_new_row(kbuf, slot, past, knew_ref[0])
    _append_new_row(vbuf, slot, past, vnew_ref[0])
    kb, vb = kbuf.at[slot], vbuf.at[slot]
    q = q_ref[0]
    heads = q.shape[0]

    carry = jnp.zeros((1, heads), F32)
    cums = []
    for j in range(n_pages):
        cs = cbuf[slot, j * page:(j + 1) * page, :] + carry
        cums.append(cs)
        carry = cs[page - 1:page, :]
    c_now = carry + fnew_ref[0]
    bias = [(c_now - cs) * LOG2E for cs in cums]
    s = _decode_scores(kb, q, n_kv) + jnp.concatenate(bias + [_tail_bias(heads)], axis=0)
    o_ref[0] = _decode_softmax_pv(s, vb, n_kv).astype(o_ref.dtype)


def _decode_attention(body, layer, page_table, q16, k_new, v_new, extra, extra_per_row, caches, n_kv):
    rows, n_pages = page_table.shape
    heads = q16.shape[1] // HEAD_DIM
    page = caches[0].shape[2]
    past = n_pages * page
    assert past % MOBA_BLOCK == 0
    row_spec = lambda shape: pl.BlockSpec((1,) + shape, lambda b, pt: (b, 0, 0))
    in_specs = [row_spec((heads, HEAD_DIM)), row_spec((n_kv, HEAD_DIM)), row_spec((n_kv, HEAD_DIM))]
    if extra_per_row:
        in_specs.append(row_spec((1, heads)))
        extra = extra.reshape(rows, 1, heads)
    else:
        in_specs.append(pl.BlockSpec(extra.shape, lambda b, pt: (0, 0)))
    in_specs += [pl.BlockSpec(memory_space=pl.ANY)] * len(caches)
    assert rows >= DECODE_BUFS - 1
    scratch = [pltpu.VMEM((DECODE_BUFS, n_kv, past + SUBLANES, HEAD_DIM) if c.ndim == 5
                          else (DECODE_BUFS, past, c.shape[2]), F32) for c in caches]
    scratch.append(pltpu.SemaphoreType.DMA((len(caches), DECODE_BUFS)))
    return pl.pallas_call(
        functools.partial(body, layer=layer, n_kv=n_kv, n_pages=n_pages, page=page),
        grid_spec=pltpu.PrefetchScalarGridSpec(
            num_scalar_prefetch=1,
            grid=(rows,),
            in_specs=in_specs,
            out_specs=pl.BlockSpec((1, heads, HEAD_DIM), lambda b, pt: (b, 0, 0)),
            scratch_shapes=scratch,
        ),
        out_shape=jax.ShapeDtypeStruct((rows, heads, HEAD_DIM), F32),
        compiler_params=_params("arbitrary"),
        name="decode_attention",
    )(page_table.reshape(-1), q16.reshape(rows, heads, HEAD_DIM), k_new, v_new, extra, *caches
      ).reshape(rows, heads * HEAD_DIM)


def _page_cumsum_kernel(x_ref, o_ref):
    page = x_ref.shape[1]
    r = lax.broadcasted_iota(jnp.int32, (page, page), 0)
    c = lax.broadcasted_iota(jnp.int32, (page, page), 1)
    o_ref[...] = jnp.dot(x_ref[...], (r <= c).astype(F32), precision=HI, preferred_element_type=F32)


def _page_cumsum(logf_cache):
    pool, page, heads = logf_cache.shape
    x = logf_cache.transpose(0, 2, 1).reshape(pool * heads, page)
    tm = _row_tile(pool * heads, 2048)
    out = pl.pallas_call(
        _page_cumsum_kernel,
        grid=(pool * heads // tm,),
        in_specs=[pl.BlockSpec((tm, page), lambda i: (i, 0))],
        out_specs=pl.BlockSpec((tm, page), lambda i: (i, 0)),
        out_shape=jax.ShapeDtypeStruct((pool * heads, page), F32),
        compiler_params=_params("parallel"),
        name="page_cumsum",
    )(x)
    return out.reshape(pool, heads, page).transpose(0, 2, 1)


def _merge_kernel(x_ref, oa_ref, ob_ref, z_ref, g_ref, wa_ref, wb_ref, wo_ref, y_ref):
    wa_w = oa_ref.shape[1]
    d = x_ref.shape[1]

    def branch(o_ref, z, w_ref):
        u = o_ref[...] * (z * jax.nn.sigmoid(z))
        return jnp.dot(u.astype(BF16), w_ref[...], preferred_element_type=F32)

    ya = branch(oa_ref, z_ref[:, :wa_w], wa_ref)
    yb = branch(ob_ref, z_ref[:, wa_w:], wb_ref)
    mixed = g_ref[:, :d] * ya + g_ref[:, d:] * yb
    y_ref[...] = x_ref[...] + jnp.dot(mixed.astype(BF16), wo_ref[...], preferred_element_type=F32)


def _merge(x, oa, ob, z, gates, wa, wb, wo):
    m, d = x.shape
    tm = _row_tile(m, 256)
    rows = lambda width: pl.BlockSpec((tm, width), lambda i: (i, 0))
    resident = lambda w: pl.BlockSpec(w.shape, lambda i: (0, 0), pipeline_mode=pl.Buffered(1))
    return pl.pallas_call(
        _merge_kernel,
        grid=(m // tm,),
        in_specs=[rows(d), rows(oa.shape[1]), rows(ob.shape[1]), rows(z.shape[1]), rows(gates.shape[1]),
                  resident(wa), resident(wb), resident(wo)],
        out_specs=rows(d),
        out_shape=jax.ShapeDtypeStruct((m, d), F32),
        compiler_params=_params("parallel"),
        name="merge",
    )(x, oa, ob, z, gates, wa, wb, wo)


def _project(x2d, g_norm, wt, wts, dims):
    d, wa_w, wb_w, kva, kvb, hb = dims
    n_kv_a, n_kv_b = kva // HEAD_DIM, kvb // HEAD_DIM
    tile = wa_w
    assert wa_w == wb_w == 2 * kva == 2 * kvb
    branch = 2 * wa_w + 2 * kva
    f_row = 2 * branch
    h, ka, va, ka16, va16 = _proj(functools.partial(_proj_kv_kernel, n_kv=n_kv_a, n_forget=0), x2d, wt, (wa_w, 0, 1),
                                  tile, [(wts["g_ka"], tile, 0)],
                                  [(F32, (n_kv_a, HEAD_DIM)), (F32, (n_kv_a, HEAD_DIM)), (BF16, (kva,)),
                                   (BF16, (kva,))], norm_gain=g_norm.reshape(1, d))
    (q16,) = _proj(_proj_q_kernel, h, wt, (0, branch, 2), tile, [(wts["g_q"], tile, 0)], [(BF16, (wa_w + wb_w,))])
    kb, vb, kb16, vb16, logf = _proj(functools.partial(_proj_kv_kernel, n_kv=n_kv_b, n_forget=hb), h, wt,
                                     (branch + wb_w, 0, 1), tile, [(wts["g_kb"], tile, 0), (wts["b_f"], LANES, 0)],
                                     [(F32, (n_kv_b, HEAD_DIM)), (F32, (n_kv_b, HEAD_DIM)), (BF16, (kvb,)),
                                      (BF16, (kvb,)), (F32, (hb,))], forget_row=f_row)
    (z,) = _proj(_proj_plain_kernel, h, wt, (wa_w + 2 * kva, branch, 2), tile, [], [(F32, (wa_w + wb_w,))])
    gate_tile = min(tile, 2 * d)
    (gates,) = _proj(_proj_gate_kernel, h, wt, (f_row + hb, gate_tile, 2 * d // gate_tile), gate_tile,
                     [(wts["b_g"], gate_tile, 0)], [(F32, (2 * d,))])
    return dict(q16=q16, ka=ka, va=va, ka16=ka16, va16=va16, kb=kb, vb=vb, kb16=kb16, vb16=vb16, logf=logf, z=z,
                gates=gates)


def _layer_weights(b_forget, b_merge, gq_a, gk_a, gq_b, gk_b, wa, wb, wo, dims):
    d, wa_w, wb_w, kva, kvb, hb = dims
    return dict(
        g_q=jnp.concatenate([jnp.tile(gq_a, wa_w // HEAD_DIM), jnp.tile(gq_b, wb_w // HEAD_DIM)]).reshape(1, -1),
        g_ka=jnp.concatenate([jnp.tile(gk_a, kva // HEAD_DIM), jnp.zeros((kva,), F32)]).reshape(1, -1),
        g_kb=jnp.concatenate([jnp.tile(gk_b, kvb // HEAD_DIM), jnp.zeros((kvb,), F32)]).reshape(1, -1),
        b_f=jnp.concatenate([b_forget, jnp.zeros((LANES - hb,), F32)]).reshape(1, -1),
        b_g=b_merge.reshape(1, 2 * d),
        wa=wa.astype(BF16), wb=wb.astype(BF16), wo=wo.astype(BF16),
    )


def kernel(x_prompt, x_sample, cache_k_moba, cache_v_moba, cache_k_fox, cache_v_fox, cache_logf_fox, page_table,
           g_norm, w_in, b_forget, b_merge, gq_moba, gk_moba, gq_fox, gk_fox, w_branch_moba, w_branch_fox, w_out):
    depth = w_in.shape[0]
    batch, seq, d = x_prompt.shape
    rows, dec_seq, _ = x_sample.shape
    assert dec_seq == 1 and seq % MOBA_BLOCK == 0
    n_kv_a, n_kv_b = cache_k_moba.shape[3], cache_k_fox.shape[3]
    hb = b_forget.shape[1]
    wa_w, wb_w = w_branch_moba.shape[1], w_branch_fox.shape[1]
    ha = wa_w // HEAD_DIM
    assert wb_w == hb * HEAD_DIM
    kva, kvb = n_kv_a * HEAD_DIM, n_kv_b * HEAD_DIM
    slopes = 2.0 ** (-8.0 * jnp.arange(1, ha + 1, dtype=F32) / ha)
    past = page_table.shape[1] * cache_k_moba.shape[2]
    pos = jnp.arange(past + SUBLANES)[:, None]
    alibi_decode = jnp.where(pos < past, (-LOG2E) * slopes[None, :] * (past - pos).astype(F32),
                             jnp.where(pos == past, 0.0, NEG_INF))

    xp = x_prompt.reshape(batch * seq, d)
    xs = x_sample.reshape(rows, d)
    outs_p, outs_s = [], []
    for l in range(depth):
        dims = (d, wa_w, wb_w, kva, kvb, hb)
        wt = w_in[l].T
        wts = _layer_weights(b_forget[l], b_merge[l], gq_moba[l], gk_moba[l], gq_fox[l], gk_fox[l],
                             w_branch_moba[l], w_branch_fox[l], w_out[l], dims)
        pp = _project(xp, g_norm[l], wt, wts, dims)
        kmean = _kmean(pp["ka"], batch, seq)
        oa = _moba_prompt(pp["q16"], 0, pp["ka16"], pp["va16"], kmean, slopes, batch, seq, ha, n_kv_a)
        group_b = hb // n_kv_b
        c_row = _cumsum_rows(pp["logf"].reshape(batch, seq, hb).transpose(0, 2, 1))
        c_col = c_row.reshape(batch, n_kv_b, group_b, seq).transpose(1, 0, 3, 2).reshape(n_kv_b, batch * seq, group_b)
        ob = _fox_prompt(pp["q16"], wa_w, pp["kb16"], pp["vb16"], c_col,
                         c_row.reshape(batch, n_kv_b, group_b, seq), batch, seq, hb, n_kv_b)
        xp = _merge(xp, oa, ob, pp["z"], pp["gates"], wts["wa"], wts["wb"], wts["wo"])
        outs_p.append((pp["ka"], pp["va"], pp["kb"], pp["vb"], pp["logf"]))
        ps = _project(xs, g_norm[l], wt, wts, dims)
        oa_s = _decode_attention(_moba_decode_kernel, l, page_table, ps["q16"][:, :wa_w], ps["ka"], ps["va"],
                                 alibi_decode, False, [cache_k_moba, cache_v_moba], n_kv_a)
        ob_s = _decode_attention(_fox_decode_kernel, l, page_table, ps["q16"][:, wa_w:], ps["kb"], ps["vb"],
                                 ps["logf"], True, [cache_k_fox, cache_v_fox, _page_cumsum(cache_logf_fox[l])], n_kv_b)
        xs = _merge(xs, oa_s, ob_s, ps["z"], ps["gates"], wts["wa"], wts["wb"], wts["wo"])
        outs_s.append((ps["ka"], ps["va"], ps["kb"], ps["vb"], ps["logf"]))

    def stacked(outs, i, shape):
        return jnp.stack([o[i].reshape(shape) for o in outs])

    res = [xp.reshape(batch, seq, d), xs.reshape(rows, 1, d)]
    for outs, lead in ((outs_p, (batch, seq)), (outs_s, (rows, 1))):
        res += [stacked(outs, 0, lead + (n_kv_a, HEAD_DIM)), stacked(outs, 1, lead + (n_kv_a, HEAD_DIM)),
                stacked(outs, 2, lead + (n_kv_b, HEAD_DIM)), stacked(outs, 3, lead + (n_kv_b, HEAD_DIM)),
                stacked(outs, 4, lead + (hb,))]
    return tuple(res)
```
